```python
import math
import jax, jax.numpy as jnp
from jax import lax
import numpy as np

D_MODEL = 1024
BATCH = 16
SEQ = 4096
DEPTH = 1

N_META = 16
CHUNK = 128
RET_HEADS = 4
RET_DK = 128
RET_DV = 128
DSA_HEADS = 8
DSA_DH = 64
IDX_HEADS = 8
IDX_DH = 64
TOPK_MAX = 256
D_FF = 4 * D_MODEL
N_BUCKETS = 32
MAX_DISTANCE = 128
ROPE_BASE = 10000.0
EPS = 1e-6
NEG = -1e30

RET_W = RET_HEADS * RET_DV
DSA_W = DSA_HEADS * DSA_DH
D_MIX = RET_W + DSA_W
COL_SIZES = (RET_HEADS * RET_DK, RET_HEADS * RET_DK, RET_W, RET_W,
             DSA_W, DSA_DH, DSA_DH, IDX_HEADS * IDX_DH, IDX_DH, IDX_HEADS)
D_IN = sum(COL_SIZES)
SPLIT_POINTS = tuple(int(v) for v in np.cumsum(COL_SIZES)[:-1])

kernel_name = "hymba_retention_dsa_hybrid"


def rmsnorm(x, w):
    xf = x.astype(jnp.float32)
    y = xf * lax.rsqrt(jnp.mean(xf * xf, axis=-1, keepdims=True) + EPS)
    return y * w.astype(jnp.float32)


def rotary(x, pos):
    half = x.shape[-1] // 2
    inv = ROPE_BASE ** (-jnp.arange(half, dtype=jnp.float32) / half)
    ang = pos.astype(jnp.float32)[:, None] * inv[None, :]
    cos = jnp.cos(ang)[None, :, None, :]
    sin = jnp.sin(ang)[None, :, None, :]
    x1, x2 = x[..., :half], x[..., half:]
    return jnp.concatenate([x1 * cos - x2 * sin, x1 * sin + x2 * cos], axis=-1)


def retention(q, k, v, g, gain, pos):
    B, T, H, _ = q.shape
    pad = CHUNK - N_META
    q = rotary(q.astype(jnp.float32), pos)
    k = rotary(k.astype(jnp.float32), pos) * (RET_DK ** -0.5)
    v = v.astype(jnp.float32)
    padw = ((0, 0), (pad, 0), (0, 0), (0, 0))
    nc = (T + pad) // CHUNK
    qc = jnp.pad(q, padw).reshape(B, nc, CHUNK, H, RET_DK)
    kc = jnp.pad(k, padw).reshape(B, nc, CHUNK, H, RET_DK)
    vc = jnp.pad(v, padw).reshape(B, nc, CHUNK, H, RET_DV)

    log_gamma = jnp.log(1.0 - jnp.exp2(-5.0 - jnp.arange(H, dtype=jnp.float32)))
    idx = jnp.arange(CHUNK, dtype=jnp.float32)
    diff = idx[:, None] - idx[None, :]
    decay_mask = jnp.where(diff[None] >= 0,
                           jnp.exp(log_gamma[:, None, None] * jnp.maximum(diff, 0.0)[None]),
                           0.0)
    scores = jnp.einsum('bnchd,bnmhd->bnhcm', qc, kc) * decay_mask[None, None]
    o_inner = jnp.einsum('bnhcm,bnmhe->bnche', scores, vc)
    zeta = jnp.exp(log_gamma[None, :] * (CHUNK - 1 - idx)[:, None])
    kv = jnp.einsum('bnmhd,bnmhe,mh->bnhde', kc, vc, zeta)
    chunk_decay = jnp.exp(log_gamma * CHUNK)[None, :, None, None]

    def step(S, kv_c):
        return S * chunk_decay + kv_c, S

    _, s_prev = lax.scan(step, jnp.zeros_like(kv[:, 0]), jnp.moveaxis(kv, 1, 0))
    s_prev = jnp.moveaxis(s_prev, 0, 1)
    xi = jnp.exp(log_gamma[None, :] * (idx + 1.0)[:, None])
    o_cross = jnp.einsum('bnchd,ch,bnhde->bnche', qc, xi, s_prev)
    o = (o_inner + o_cross).reshape(B, nc * CHUNK, H, RET_DV)[:, pad:]
    o = rmsnorm(o, gain.reshape(H, RET_DV))
    o = jax.nn.silu(g.astype(jnp.float32)) * o
    return o.reshape(B, T, H * RET_DV)


def t5_bucket(dist):
    max_exact = N_BUCKETS // 2
    d_f = jnp.maximum(dist, 1).astype(jnp.float32)
    large = max_exact + (jnp.log(d_f / max_exact) / math.log(MAX_DISTANCE / max_exact)
                         * (N_BUCKETS - max_exact)).astype(jnp.int32)
    large = jnp.minimum(large, N_BUCKETS - 1)
    return jnp.where(dist < max_exact, dist, large)


def sparse_attention(q, k, v, iq, ik, iw, q_gain, k_gain, rel_bias, pos, topk):
    B, T, H, dh = q.shape
    pad = CHUNK - N_META
    nb = (T + pad) // CHUNK
    qn = rmsnorm(q, q_gain) * (dh ** -0.5)
    kn = rmsnorm(k, k_gain)
    vf = v.astype(jnp.float32)
    ikf = ik.astype(jnp.float32)
    iqf = iq.astype(jnp.float32) * (IDX_DH ** -0.5)
    iwf = iw.astype(jnp.float32) * (IDX_HEADS ** -0.5)

    def to_blocks(a):
        a = jnp.pad(a, ((0, 0), (pad, 0)) + ((0, 0),) * (a.ndim - 2))
        a = a.reshape((B, nb, CHUNK) + a.shape[2:])
        return jnp.moveaxis(a, 1, 0)

    pos_blocks = jnp.pad(pos, (pad, 0), constant_values=-1).reshape(nb, CHUNK)
    key_pos = pos
    gather = jax.vmap(lambda arr, ii: arr[ii])

    def block(args):
        qb, iqb, iwb, pb = args
        s = jnp.einsum('bchd,bsd->bchs', iqb, ikf)
        score = jnp.einsum('bch,bchs->bcs', iwb, jax.nn.relu(s))
        causal = key_pos[None, :] <= pb[:, None]
        score = jnp.where(causal[None], score, NEG)
        _, sel = lax.top_k(score, topk)
        sel_pos = key_pos[sel]
        valid = sel_pos <= pb[None, :, None]
        k_sel = gather(kn, sel)
        v_sel = gather(vf, sel)
        logits = jnp.einsum('bchd,bckd->bchk', qb, k_sel)
        dist = jnp.maximum(pb[None, :, None] - sel_pos, 0)
        bias = jnp.transpose(rel_bias.astype(jnp.float32)[t5_bucket(dist)], (0, 1, 3, 2))
        logits = jnp.where(valid[:, :, None, :], logits + bias, NEG)
        p = jax.nn.softmax(logits, axis=-1)
        return jnp.einsum('bchk,bckd->bchd', p, v_sel)

    out = lax.map(block, (to_blocks(qn), to_blocks(iqf), to_blocks(iwf), pos_blocks))
    out = jnp.moveaxis(out, 0, 1).reshape(B, nb * CHUNK, H * dh)[:, pad:]
    return out


def setup_inputs(seed: int = 0) -> dict:
    key = jax.random.key(seed)
    ks = jax.random.split(key, 12)
    f32 = jnp.float32
    nrm = lambda k, s: jax.random.normal(k, s, dtype=f32)
    return {
        "x": nrm(ks[0], (BATCH, SEQ, D_MODEL)),
        "meta_tokens": nrm(ks[1], (N_META, D_MODEL)),
        "norm1_w": 1.0 + 0.02 * nrm(ks[2], (DEPTH, D_MODEL)),
        "w_in": nrm(ks[3], (DEPTH, D_MODEL, D_IN)) * D_MODEL ** -0.5,
        "ret_norm_w": 1.0 + 0.02 * nrm(ks[4], (DEPTH, RET_W)),
        "q_norm_w": 1.0 + 0.02 * nrm(ks[5], (DEPTH, DSA_DH)),
        "k_norm_w": 1.0 + 0.02 * nrm(ks[6], (DEPTH, DSA_DH)),
        "rel_bias": 0.5 * nrm(ks[7], (N_BUCKETS, DSA_HEADS)),
        "w_out": nrm(ks[8], (DEPTH, D_MIX, D_MODEL)) * D_MIX ** -0.5,
        "norm2_w": 1.0 + 0.02 * nrm(ks[9], (DEPTH, D_MODEL)),
        "w_ff1": nrm(ks[10], (DEPTH, D_MODEL, D_FF)) * D_MODEL ** -0.5,
        "w_ff2": nrm(ks[11], (DEPTH, D_FF, D_MODEL)) * D_FF ** -0.5,
    }


def reference(x, meta_tokens, norm1_w, w_in, ret_norm_w, q_norm_w, k_norm_w,
              rel_bias, w_out, norm2_w, w_ff1, w_ff2):
    B, L, D = x.shape
    topk = min(TOPK_MAX, L // 4)
    meta = jnp.broadcast_to(meta_tokens.astype(x.dtype)[None], (B, N_META, D))
    h = jnp.concatenate([meta, x], axis=1)
    T = h.shape[1]
    pos = jnp.arange(T, dtype=jnp.int32)
    for l in range(DEPTH):
        u = rmsnorm(h, norm1_w[l]).astype(h.dtype)
        proj = u @ w_in[l]
        rq, rk, rv, rg, dq, dk, dv, iq, ik, iw = jnp.split(proj, SPLIT_POINTS, axis=-1)
        y_ret = retention(rq.reshape(B, T, RET_HEADS, RET_DK),
                          rk.reshape(B, T, RET_HEADS, RET_DK),
                          rv.reshape(B, T, RET_HEADS, RET_DV),
                          rg.reshape(B, T, RET_HEADS, RET_DV),
                          ret_norm_w[l], pos)
        y_dsa = sparse_attention(dq.reshape(B, T, DSA_HEADS, DSA_DH), dk, dv,
                                 iq.reshape(B, T, IDX_HEADS, IDX_DH), ik, iw,
                                 q_norm_w[l], k_norm_w[l], rel_bias, pos, topk)
        y = jnp.concatenate([y_ret, y_dsa], axis=-1).astype(h.dtype)
        h = h + y @ w_out[l]
        u = rmsnorm(h, norm2_w[l]).astype(h.dtype)
        h = h + jnp.square(jax.nn.relu(u @ w_ff1[l])) @ w_ff2[l]
    return h[:, N_META:]
```

```python
import functools
import math

import numpy as np
import jax
import jax.numpy as jnp
from jax import lax
from jax.experimental import pallas as pl
from jax.experimental.pallas import tpu as pltpu

N_META = 16
CHUNK = 128
RET_HEADS = 4
RET_DK = 128
DSA_HEADS = 8
DSA_DH = 64
IDX_HEADS = 8
TOPK_MAX = 256
N_BUCKETS = 32
MAX_DISTANCE = 128
ROPE_BASE = 10000.0
EPS = 1e-6
NEG = -1e30
PAD = CHUNK - N_META
HALF = 64
INT_MIN = -(2 ** 31)

FFN_CHUNK = 1024
ROW_TILE = 256
VMEM_LIMIT = 56 * 1024 * 1024


def _dot(a, b):
    return jnp.dot(a, b, preferred_element_type=jnp.float32)


def _dot_nt(a, b):
    return lax.dot_general(a, b, (((1,), (1,)), ((), ())), preferred_element_type=jnp.float32)


def _dot_tn(a, b):
    return lax.dot_general(a, b, (((0,), (0,)), ((), ())), preferred_element_type=jnp.float32)


def _bucket_ranges():
    max_exact = N_BUCKETS // 2
    d = np.arange(0, 2 * CHUNK)
    large = max_exact + (np.log(np.maximum(d, 1) / max_exact) / math.log(MAX_DISTANCE / max_exact)
                         * (N_BUCKETS - max_exact)).astype(np.int64)
    bucket = np.where(d < max_exact, d, np.minimum(large, N_BUCKETS - 1))
    out = []
    for b in range(N_BUCKETS - 1):
        idx = np.nonzero(bucket == b)[0]
        out.append((int(idx.min()), int(idx.max())))
    return out


def _rope_kernel(cos_ref, sin_ref):
    n = pl.program_id(0)
    row = lax.broadcasted_iota(jnp.int32, (CHUNK, 128), 0)
    lane = lax.broadcasted_iota(jnp.int32, (CHUNK, 128), 1)
    pos = (n * CHUNK + row - PAD).astype(jnp.float32)
    frac = (lane % HALF).astype(jnp.float32) / HALF
    inv = jnp.exp(-frac * math.log(ROPE_BASE))
    ang = pos * inv
    cos_ref[...] = jnp.cos(ang)
    s = jnp.sin(ang)
    sin_ref[...] = jnp.where(lane < HALF, -s, s)


def _rope_tables(nb):
    return pl.pallas_call(
        _rope_kernel,
        grid=(nb,),
        out_specs=[pl.BlockSpec((CHUNK, 128), lambda n: (n, 0))] * 2,
        out_shape=[jax.ShapeDtypeStruct((nb * CHUNK, 128), jnp.float32)] * 2,
        name="rope_tables",
    )()


def _bias_kernel(rb_ref, out_ref):
    row = lax.broadcasted_iota(jnp.int32, (CHUNK, CHUNK), 0)
    col = lax.broadcasted_iota(jnp.int32, (CHUNK, CHUNK), 1)
    ranges = _bucket_ranges()
    for h in range(DSA_HEADS):
        far = jnp.full((CHUNK, CHUNK), rb_ref[N_BUCKETS - 1, h], jnp.float32)
        out_ref[2, h] = far
        for t in range(2):
            dist = row - col + t * CHUNK
            tile = far
            for b, (lo, hi) in enumerate(ranges):
                tile = jnp.where((dist >= lo) & (dist <= hi), rb_ref[b, h], tile)
            out_ref[t, h] = tile


def _bias_tables(rel_bias):
    return pl.pallas_call(
        _bias_kernel,
        in_specs=[pl.BlockSpec(memory_space=pltpu.SMEM)],
        out_specs=pl.BlockSpec(memory_space=pltpu.VMEM),
        out_shape=jax.ShapeDtypeStruct((3, DSA_HEADS, CHUNK, CHUNK), jnp.float32),
        name="bias_tables",
    )(rel_bias.astype(jnp.float32))


def _inproj_kernel(x_ref, meta_ref, n1_ref, wr_ref, wqi_ref, wk_ref, wv_ref, ww_ref,
                   qs_ref, ks_ref,
                   r_ref, qi_ref, k1_ref, k2_ref, v_ref, iw_ref):
    n = pl.program_id(1)
    src = jnp.where(n == 0, meta_ref[...], x_ref[0])
    ms = jnp.mean(src * src, axis=-1, keepdims=True)
    u = (src * lax.rsqrt(ms + EPS) * n1_ref[...]).astype(jnp.bfloat16)

    r_ref[0] = _dot(u, wr_ref[...])

    lane = lax.broadcasted_iota(jnp.int32, (CHUNK, 128), 1)
    lo = lane < HALF

    def head_rms(t):
        ssq = jnp.sum(jnp.where(lo, t * t, 0.0), axis=-1, keepdims=True)
        return lax.rsqrt(ssq / DSA_DH + EPS)

    pq = _dot(u, wqi_ref[...])
    for h in range(DSA_HEADS):
        t = pq[:, h * 128:(h + 1) * 128]
        scale = jnp.where(lo, head_rms(t), 1.0) * qs_ref[...]
        qi_ref[0, h] = (t * scale).astype(jnp.bfloat16)

    pk = _dot(u, wk_ref[...])
    kn = pk * head_rms(pk) * ks_ref[...]
    k1_ref[0] = jnp.where(lo, kn, 0.0).astype(jnp.bfloat16)
    k2_ref[0] = jnp.where(lo, 0.0, pk).astype(jnp.bfloat16)

    pv = _dot(u, wv_ref[...])
    v_ref[0] = jnp.where(lo, pv, jnp.where(lane == HALF, 1.0, 0.0)).astype(jnp.bfloat16)

    iw_ref[0] = _dot(u, ww_ref[...]) * (IDX_HEADS ** -0.5)


def _inproj(x, meta_pad, n1, wr, wqi, wk, wv, ww, qscale, kscale):
    B, L, D = x.shape
    nb = L // CHUNK + 1
    tp = nb * CHUNK
    const = lambda shape: pl.BlockSpec(shape, lambda b, n: (0,) * len(shape))
    blk = lambda w: pl.BlockSpec((1, CHUNK, w), lambda b, n: (b, n, 0))
    return pl.pallas_call(
        _inproj_kernel,
        grid=(B, nb),
        in_specs=[
            pl.BlockSpec((1, CHUNK, D), lambda b, n: (b, jnp.maximum(n - 1, 0), 0)),
            const(meta_pad.shape), const(n1.shape), const(wr.shape), const(wqi.shape),
            const(wk.shape), const(wv.shape), const(ww.shape), const(qscale.shape),
            const(kscale.shape),
        ],
        out_specs=[
            blk(wr.shape[1]),
            pl.BlockSpec((1, DSA_HEADS, CHUNK, 128), lambda b, n: (b, 0, n, 0)),
            blk(128), blk(128), blk(128), blk(128),
        ],
        out_shape=[
            jax.ShapeDtypeStruct((B, tp, wr.shape[1]), jnp.float32),
            jax.ShapeDtypeStruct((B, DSA_HEADS, tp, 128), jnp.bfloat16),
            jax.ShapeDtypeStruct((B, tp, 128), jnp.bfloat16),
            jax.ShapeDtypeStruct((B, tp, 128), jnp.bfloat16),
            jax.ShapeDtypeStruct((B, tp, 128), jnp.bfloat16),
            jax.ShapeDtypeStruct((B, tp, 128), jnp.float32),
        ],
        compiler_params=pltpu.CompilerParams(
            dimension_semantics=("arbitrary", "arbitrary"), vmem_limit_bytes=VMEM_LIMIT),
        name="in_projection",
    )(x, meta_pad, n1, wr, wqi, wk, wv, ww, qscale, kscale)


def _retention_kernel(r_ref, cos_ref, sin_ref, gain_ref, y_ref, state_ref):
    n = pl.program_id(1)

    @pl.when(n == 0)
    def _():
        state_ref[...] = jnp.zeros_like(state_ref)

    cos2 = cos_ref[...]
    sin2 = sin_ref[...]
    row = lax.broadcasted_iota(jnp.int32, (CHUNK, CHUNK), 0)
    col = lax.broadcasted_iota(jnp.int32, (CHUNK, CHUNK), 1)
    rowf = row.astype(jnp.float32)
    diff = (row - col).astype(jnp.float32)
    w = RET_HEADS * RET_DK
    bf = jnp.bfloat16

    for h in range(RET_HEADS):
        log_gamma = math.log(1.0 - 2.0 ** (-5.0 - h))
        sl = slice(h * 128, (h + 1) * 128)
        q = r_ref[0, :, sl]
        k = r_ref[0, :, w + h * 128: w + (h + 1) * 128]
        v = r_ref[0, :, 2 * w + h * 128: 2 * w + (h + 1) * 128].astype(bf)
        g = r_ref[0, :, 3 * w + h * 128: 3 * w + (h + 1) * 128]

        qr = q * cos2 + pltpu.roll(q, HALF, 1) * sin2
        kr = (k * cos2 + pltpu.roll(k, HALF, 1) * sin2) * (RET_DK ** -0.5)

        decay = jnp.where(diff >= 0, jnp.exp(log_gamma * jnp.maximum(diff, 0.0)), 0.0)
        scores = _dot_nt(qr.astype(bf), kr.astype(bf)) * decay
        o = _dot(scores.astype(bf), v)

        state = state_ref[h]
        xi = jnp.exp(log_gamma * (rowf + 1.0))
        o = o + _dot((qr * xi).astype(bf), state.astype(bf))

        zeta = jnp.exp(log_gamma * (CHUNK - 1.0 - rowf))
        kv = _dot_tn((kr * zeta).astype(bf), v)
        state_ref[h] = state * math.exp(log_gamma * CHUNK) + kv

        ms = jnp.mean(o * o, axis=-1, keepdims=True)
        on = o * lax.rsqrt(ms + EPS) * gain_ref[:, sl]
        gate = g * (1.0 / (1.0 + jnp.exp(-g)))
        y_ref[0, :, sl] = (gate * on).astype(y_ref.dtype)


def _retention(r_all, cos2, sin2, gain):
    B, tp, _ = r_all.shape
    nb = tp // CHUNK
    w = RET_HEADS * RET_DK
    return pl.pallas_call(
        _retention_kernel,
        grid=(B, nb),
        in_specs=[
            pl.BlockSpec((1, CHUNK, 4 * w), lambda b, n: (b, n, 0)),
            pl.BlockSpec((CHUNK, 128), lambda b, n: (n, 0)),
            pl.BlockSpec((CHUNK, 128), lambda b, n: (n, 0)),
            pl.BlockSpec((1, w), lambda b, n: (0, 0)),
        ],
        out_specs=pl.BlockSpec((1, CHUNK, w), lambda b, n: (b, jnp.maximum(n - 1, 0), 0)),
        out_shape=jax.ShapeDtypeStruct((B, tp - CHUNK, w), jnp.bfloat16),
        scratch_shapes=[pltpu.VMEM((RET_HEADS, RET_DK, RET_DK), jnp.float32)],
        compiler_params=pltpu.CompilerParams(
            dimension_semantics=("arbitrary", "arbitrary"), vmem_limit_bytes=VMEM_LIMIT),
        name="retention",
    )(r_all, cos2, sin2, gain)


def _dsa_kernel(qi_ref, iw_ref, k1_ref, k2_ref, v_ref, bias_ref, tri_ref, y_ref,
                key_ref, iwb_ref, m_ref, acc_ref, *, topk, group):
    n = pl.program_id(1) + 1
    nkb = n + 1
    bf = jnp.bfloat16
    H = DSA_HEADS

    qi = qi_ref[0].reshape(H * CHUNK, 128)
    iw = iw_ref[0]
    for h in range(IDX_HEADS):
        iwb_ref[h] = jnp.broadcast_to(iw[:, h:h + 1], (CHUNK, CHUNK))

    row = lax.broadcasted_iota(jnp.int32, (CHUNK, CHUNK), 0)
    col = lax.broadcasted_iota(jnp.int32, (CHUNK, CHUNK), 1)
    q_idx = n * CHUNK + row

    def score_block(j, _):
        off = pl.multiple_of(j * CHUNK, CHUNK)
        s = _dot_nt(qi, k2_ref[0, pl.ds(off, CHUNK), :]).reshape(H, CHUNK, CHUNK)
        sc = iwb_ref[0] * jnp.maximum(s[0], 0.0)
        for h in range(1, IDX_HEADS):
            sc = sc + iwb_ref[h] * jnp.maximum(s[h], 0.0)
        k_idx = j * CHUNK + col
        valid = (k_idx <= q_idx) & (k_idx >= PAD)
        sc = jnp.where(sc == 0.0, 0.0, sc)
        sc = jnp.where(valid, sc, NEG)
        bits = lax.bitcast_convert_type(sc, jnp.int32)
        key_ref[j] = jnp.where(bits < 0, bits ^ jnp.int32(0x7FFFFFFF), bits)
        return 0

    lax.fori_loop(0, nkb, score_block, 0)

    neg_bits = int(np.float32(NEG).view(np.int32))
    neg_key = neg_bits ^ 0x7FFFFFFF
    if neg_key >= 2 ** 31:
        neg_key -= 2 ** 32
    ngroups = (nkb + group - 1) // group

    def fill_block(j, _):
        key_ref[j] = jnp.full((CHUNK, CHUNK), neg_key, jnp.int32)
        return 0

    lax.fori_loop(nkb, ngroups * group, fill_block, 0)

    def bisect(i, lo):
        cand = lo + jnp.left_shift(jnp.int32(1), 31 - i)
        cand_b = jnp.broadcast_to(cand, (CHUNK, CHUNK))

        def count_group(gidx, part):
            for u in range(group):
                part = part + jnp.where(key_ref[gidx * group + u] >= cand_b, 1.0, 0.0)
            return part

        part = lax.fori_loop(0, ngroups, count_group, jnp.zeros((CHUNK, CHUNK), jnp.float32))
        cnt = jnp.sum(part, axis=-1, keepdims=True)
        return jnp.where(cnt >= topk, cand, lo)

    thr = lax.fori_loop(0, 32, bisect, jnp.full((CHUNK, 1), INT_MIN, jnp.int32))
    thr_b = jnp.broadcast_to(thr, (CHUNK, CHUNK))

    def count_gt(j, part):
        return part + jnp.where(key_ref[j] > thr_b, 1.0, 0.0)

    n_gt = jnp.sum(lax.fori_loop(0, nkb, count_gt, jnp.zeros((CHUNK, CHUNK), jnp.float32)),
                   axis=-1, keepdims=True)
    n_tie = jnp.where(thr == neg_key, 0.0, topk - n_gt)
    n_tie_b = jnp.broadcast_to(n_tie, (CHUNK, CHUNK))

    m_ref[...] = jnp.full_like(m_ref, NEG)
    acc_ref[...] = jnp.zeros_like(acc_ref)

    def attend(j, ties_before):
        off = pl.multiple_of(j * CHUNK, CHUNK)
        key = key_ref[j]
        eq = key == thr_b
        eqf = jnp.where(eq, 1.0, 0.0)
        before = ties_before + _dot(eqf.astype(bf), tri_ref[...])
        sel = (key > thr_b) | (eq & (before < n_tie_b))

        s = _dot_nt(qi, k1_ref[0, pl.ds(off, CHUNK), :]).reshape(H, CHUNK, CHUNK)
        s = s + bias_ref[jnp.minimum(n - j, 2)]
        s = jnp.where(sel[None], s, NEG)
        m_prev = m_ref[...]
        m_new = jnp.maximum(m_prev, jnp.max(s, axis=-1, keepdims=True))
        alpha = jnp.exp(m_prev - m_new)
        p = jnp.exp(s - m_new)
        pv = _dot(p.reshape(H * CHUNK, CHUNK).astype(bf), v_ref[0, pl.ds(off, CHUNK), :])
        acc_ref[...] = alpha.reshape(H * CHUNK, 1) * acc_ref[...] + pv
        m_ref[...] = m_new
        return ties_before + jnp.sum(eqf, axis=-1, keepdims=True)

    lax.fori_loop(0, nkb, attend, jnp.zeros((CHUNK, CHUNK), jnp.float32))

    lo = col < HALF
    for t in range(H // 2):
        tiles = []
        for h in (2 * t, 2 * t + 1):
            a = acc_ref[h * CHUNK:(h + 1) * CHUNK, :]
            tiles.append(a / jnp.broadcast_to(a[:, HALF:HALF + 1], (CHUNK, CHUNK)))
        y_ref[0, :, t * 128:(t + 1) * 128] = jnp.where(
            lo, tiles[0], pltpu.roll(tiles[1], HALF, 1)).astype(y_ref.dtype)


def _sparse_attention(qi, iw, k1, k2, vx, bias_tbl, tri, topk):
    B, H, tp, _ = qi.shape
    nb = tp // CHUNK
    group = 4
    nkeyblk = ((nb + group - 1) // group) * group
    kspec = pl.BlockSpec((1, tp, 128), lambda b, n: (b, 0, 0))
    return pl.pallas_call(
        functools.partial(_dsa_kernel, topk=topk, group=group),
        grid=(B, nb - 1),
        in_specs=[
            pl.BlockSpec((1, H, CHUNK, 128), lambda b, n: (b, 0, n + 1, 0)),
            pl.BlockSpec((1, CHUNK, 128), lambda b, n: (b, n + 1, 0)),
            kspec, kspec, kspec,
            pl.BlockSpec(bias_tbl.shape, lambda b, n: (0, 0, 0, 0)),
            pl.BlockSpec(tri.shape, lambda b, n: (0, 0)),
        ],
        out_specs=pl.BlockSpec((1, CHUNK, H * DSA_DH), lambda b, n: (b, n, 0)),
        out_shape=jax.ShapeDtypeStruct((B, tp - CHUNK, H * DSA_DH), jnp.bfloat16),
        scratch_shapes=[
            pltpu.VMEM((nkeyblk, CHUNK, CHUNK), jnp.int32),
            pltpu.VMEM((IDX_HEADS, CHUNK, CHUNK), jnp.float32),
            pltpu.VMEM((H, CHUNK, 1), jnp.float32),
            pltpu.VMEM((H * CHUNK, 128), jnp.float32),
        ],
        compiler_params=pltpu.CompilerParams(
            dimension_semantics=("arbitrary", "arbitrary"), vmem_limit_bytes=VMEM_LIMIT),
        name="sparse_attention",
    )(qi, iw, k1, k2, vx, bias_tbl, tri)


def _mlp_kernel(x_ref, yr_ref, yd_ref, wor_ref, wod_ref, n2_ref, w1_ref, w2_ref, o_ref):
    h1 = x_ref[...] + _dot(yr_ref[...], wor_ref[...]) + _dot(yd_ref[...], wod_ref[...])
    ms = jnp.mean(h1 * h1, axis=-1, keepdims=True)
    u = (h1 * lax.rsqrt(ms + EPS) * n2_ref[...]).astype(jnp.bfloat16)
    o_ref[...] = h1
    d_ff = w1_ref.shape[1]
    for c in range(d_ff // FFN_CHUNK):
        sl = slice(c * FFN_CHUNK, (c + 1) * FFN_CHUNK)
        f = jnp.maximum(_dot(u, w1_ref[:, sl]), 0.0)
        o_ref[...] += _dot((f * f).astype(jnp.bfloat16), w2_ref[sl, :])


def _out_mlp(x2, yr, yd, wor, wod, n2, w1, w2):
    rows, D = x2.shape
    const = lambda a: pl.BlockSpec(a.shape, lambda i: (0, 0))
    tile = lambda w: pl.BlockSpec((ROW_TILE, w), lambda i: (i, 0))
    return pl.pallas_call(
        _mlp_kernel,
        grid=(rows // ROW_TILE,),
        in_specs=[tile(D), tile(yr.shape[1]), tile(yd.shape[1]),
                  const(wor), const(wod), const(n2), const(w1), const(w2)],
        out_specs=tile(D),
        out_shape=jax.ShapeDtypeStruct((rows, D), jnp.float32),
        compiler_params=pltpu.CompilerParams(
            dimension_semantics=("arbitrary",), vmem_limit_bytes=VMEM_LIMIT),
        name="out_mlp",
    )(x2, yr, yd, wor, wod, n2, w1, w2)


def kernel(x, meta_tokens, norm1_w, w_in, ret_norm_w, q_norm_w, k_norm_w, rel_bias,
           w_out, norm2_w, w_ff1, w_ff2):
    B, L, D = x.shape
    assert L % ROW_TILE == 0 and w_in.shape[0] == 1
    topk = min(TOPK_MAX, L // 4)
    nb = L // CHUNK + 1
    bf = jnp.bfloat16
    f32 = jnp.float32

    rw = RET_HEADS * RET_DK
    dw = DSA_HEADS * DSA_DH
    sizes = (rw, rw, rw, rw, dw, DSA_DH, DSA_DH, IDX_HEADS * DSA_DH, DSA_DH, IDX_HEADS)
    offs = np.concatenate([[0], np.cumsum(sizes)])
    col = lambda i: w_in[0][:, int(offs[i]):int(offs[i + 1])]
    wr = jnp.concatenate([col(0), col(1), col(2), col(3)], axis=1).astype(bf)
    wq = col(4).reshape(D, DSA_HEADS, DSA_DH)
    wiq = col(7).reshape(D, IDX_HEADS, DSA_DH)
    wqi = jnp.concatenate([wq, wiq], axis=2).reshape(D, DSA_HEADS * 128).astype(bf)
    wk = jnp.concatenate([col(5), col(8)], axis=1).astype(bf)
    wv = jnp.concatenate([col(6), jnp.zeros((D, 128 - DSA_DH), f32)], axis=1).astype(bf)
    ww = jnp.concatenate([col(9), jnp.zeros((D, 128 - IDX_HEADS), f32)], axis=1).astype(bf)

    idx_scale = jnp.full((DSA_DH,), DSA_DH ** -0.5, f32)
    qscale = jnp.concatenate([q_norm_w[0].astype(f32) * DSA_DH ** -0.5, idx_scale])[None]
    kscale = jnp.concatenate([k_norm_w[0].astype(f32), jnp.ones((DSA_DH,), f32)])[None]
    meta_pad = jnp.concatenate([jnp.zeros((PAD, D), x.dtype), meta_tokens.astype(x.dtype)], axis=0)

    cos2, sin2 = _rope_tables(nb)
    bias_tbl = _bias_tables(rel_bias)
    tri = jnp.asarray(np.triu(np.ones((CHUNK, CHUNK), np.float32), k=1), dtype=bf)

    r_all, qi, k1, k2, vx, iw = _inproj(x, meta_pad, norm1_w[0][None].astype(f32),
                                        wr, wqi, wk, wv, ww, qscale, kscale)
    y_ret = _retention(r_all, cos2, sin2, ret_norm_w[0][None].astype(f32))
    y_dsa = _sparse_attention(qi, iw, k1, k2, vx, bias_tbl, tri, topk)

    wo = w_out[0].astype(bf)
    out = _out_mlp(x.reshape(B * L, D), y_ret.reshape(B * L, rw), y_dsa.reshape(B * L, dw),
                   wo[:rw], wo[rw:], norm2_w[0][None].astype(f32),
                   w_ff1[0].astype(bf), w_ff2[0].astype(bf))
    return out.reshape(B, L, D)
```

```python
import functools
import math

import numpy as np
import jax
import jax.numpy as jnp
from jax import lax
from jax.experimental import pallas as pl
from jax.experimental.pallas import tpu as pltpu

N_META = 16
CHUNK = 128
RET_HEADS = 4
RET_DK = 128
DSA_HEADS = 8
DSA_DH = 64
IDX_HEADS = 8
TOPK_MAX = 256
N_BUCKETS = 32
MAX_DISTANCE = 128
ROPE_BASE = 10000.0
EPS = 1e-6
NEG = -1e30
PAD = CHUNK - N_META
HALF = 64
INT_MIN = -(2 ** 31)
LOG2E = math.log2(math.e)

FFN_CHUNK = 1024
ROW_TILE = 256
VMEM_LIMIT = 56 * 1024 * 1024


def _dot(a, b):
    return jnp.dot(a, b, preferred_element_type=jnp.float32)


def _dot_nt(a, b):
    return lax.dot_general(a, b, (((1,), (1,)), ((), ())), preferred_element_type=jnp.float32)


def _dot_tn(a, b):
    return lax.dot_general(a, b, (((0,), (0,)), ((), ())), preferred_element_type=jnp.float32)


def _bucket_ranges():
    max_exact = N_BUCKETS // 2
    d = np.arange(0, 2 * CHUNK)
    large = max_exact + (np.log(np.maximum(d, 1) / max_exact) / math.log(MAX_DISTANCE / max_exact)
                         * (N_BUCKETS - max_exact)).astype(np.int64)
    bucket = np.where(d < max_exact, d, np.minimum(large, N_BUCKETS - 1))
    out = []
    for b in range(N_BUCKETS - 1):
        idx = np.nonzero(bucket == b)[0]
        out.append((int(idx.min()), int(idx.max())))
    return out


def _rope_kernel(cos_ref, sin_ref):
    n = pl.program_id(0)
    row = lax.broadcasted_iota(jnp.int32, (CHUNK, 128), 0)
    lane = lax.broadcasted_iota(jnp.int32, (CHUNK, 128), 1)
    pos = (n * CHUNK + row - PAD).astype(jnp.float32)
    frac = (lane % HALF).astype(jnp.float32) / HALF
    inv = jnp.exp(-frac * math.log(ROPE_BASE))
    ang = pos * inv
    cos_ref[...] = jnp.cos(ang)
    s = jnp.sin(ang)
    sin_ref[...] = jnp.where(lane < HALF, -s, s)


def _rope_tables(nb):
    return pl.pallas_call(
        _rope_kernel,
        grid=(nb,),
        out_specs=[pl.BlockSpec((CHUNK, 128), lambda n: (n, 0))] * 2,
        out_shape=[jax.ShapeDtypeStruct((nb * CHUNK, 128), jnp.float32)] * 2,
        name="rope_tables",
    )()


def _bias_kernel(rb_ref, out_ref):
    row = lax.broadcasted_iota(jnp.int32, (CHUNK, CHUNK), 0)
    col = lax.broadcasted_iota(jnp.int32, (CHUNK, CHUNK), 1)
    ranges = _bucket_ranges()
    out_ref[2] = jnp.zeros(out_ref.shape[1:], jnp.float32)
    for h in range(DSA_HEADS):
        far = rb_ref[N_BUCKETS - 1, h]
        for t in range(2):
            dist = col - row + t * CHUNK
            tile = jnp.zeros((CHUNK, CHUNK), jnp.float32)
            for b, (lo, hi) in enumerate(ranges):
                tile = jnp.where((dist >= lo) & (dist <= hi), (rb_ref[b, h] - far) * LOG2E, tile)
            out_ref[t, :, h * 128:(h + 1) * 128] = tile


def _bias_tables(rel_bias):
    return pl.pallas_call(
        _bias_kernel,
        in_specs=[pl.BlockSpec(memory_space=pltpu.SMEM)],
        out_specs=pl.BlockSpec(memory_space=pltpu.VMEM),
        out_shape=jax.ShapeDtypeStruct((3, CHUNK, DSA_HEADS * 128), jnp.float32),
        name="bias_tables",
    )(rel_bias.astype(jnp.float32))


def _inproj_kernel(x_ref, meta_ref, n1_ref, wr_ref, wqi_ref, wk_ref, wv_ref, ww_ref,
                   qs_ref, ks_ref,
                   r_ref, qit_ref, k1_ref, k2_ref, vt_ref, iwt_ref):
    n = pl.program_id(1)
    last = pl.num_programs(1) - 1
    src = jnp.where(n == 0, meta_ref[...], x_ref[0])
    src = jnp.where(n == last, 0.0, src)
    ms = jnp.mean(src * src, axis=-1, keepdims=True)
    u = (src * lax.rsqrt(ms + EPS) * n1_ref[...]).astype(jnp.bfloat16)

    r_ref[0] = _dot(u, wr_ref[...])

    lane = lax.broadcasted_iota(jnp.int32, (CHUNK, 128), 1)
    lo = lane < HALF

    def head_rms(t):
        ssq = jnp.sum(jnp.where(lo, t * t, 0.0), axis=-1, keepdims=True)
        return lax.rsqrt(ssq / DSA_DH + EPS)

    pq = _dot(u, wqi_ref[...])
    for h in range(DSA_HEADS):
        t = pq[:, h * 128:(h + 1) * 128]
        scale = jnp.where(lo, head_rms(t), 1.0) * qs_ref[...]
        qit_ref[0, 0, :, h * 128:(h + 1) * 128] = (t * scale).T.astype(jnp.bfloat16)

    pk = _dot(u, wk_ref[...])
    kn = pk * head_rms(pk) * ks_ref[...]
    k1_ref[0] = jnp.where(lo, kn, 0.0).astype(jnp.bfloat16)
    k2_ref[0] = jnp.where(lo, 0.0, pk).astype(jnp.bfloat16)

    pv = _dot(u, wv_ref[...])
    vx = jnp.where(lo, pv, jnp.where(lane == HALF, 1.0, 0.0))
    vt_ref[0] = vx.T.astype(jnp.bfloat16)

    iw = _dot(u, ww_ref[...]) * (IDX_HEADS ** -0.5)
    iwt_ref[0, 0] = iw.T[0:IDX_HEADS, :]


def _inproj(x, meta_pad, n1, wr, wqi, wk, wv, ww, qscale, kscale):
    B, L, D = x.shape
    nb = L // CHUNK + 1
    nbp = nb + (nb % 2)
    tp = nbp * CHUNK
    const = lambda shape: pl.BlockSpec(shape, lambda b, n: (0,) * len(shape))
    blk = lambda w: pl.BlockSpec((1, CHUNK, w), lambda b, n: (b, n, 0))
    return pl.pallas_call(
        _inproj_kernel,
        grid=(B, nbp),
        in_specs=[
            pl.BlockSpec((1, CHUNK, D), lambda b, n: (b, jnp.clip(n - 1, 0, nb - 2), 0)),
            const(meta_pad.shape), const(n1.shape), const(wr.shape), const(wqi.shape),
            const(wk.shape), const(wv.shape), const(ww.shape), const(qscale.shape),
            const(kscale.shape),
        ],
        out_specs=[
            blk(wr.shape[1]),
            pl.BlockSpec((1, 1, 128, DSA_HEADS * 128), lambda b, n: (b, n, 0, 0)),
            blk(128), blk(128),
            pl.BlockSpec((1, 128, CHUNK), lambda b, n: (b, 0, n)),
            pl.BlockSpec((1, 1, IDX_HEADS, CHUNK), lambda b, n: (b, n, 0, 0)),
        ],
        out_shape=[
            jax.ShapeDtypeStruct((B, tp, wr.shape[1]), jnp.float32),
            jax.ShapeDtypeStruct((B, nbp, 128, DSA_HEADS * 128), jnp.bfloat16),
            jax.ShapeDtypeStruct((B, tp, 128), jnp.bfloat16),
            jax.ShapeDtypeStruct((B, tp, 128), jnp.bfloat16),
            jax.ShapeDtypeStruct((B, 128, tp), jnp.bfloat16),
            jax.ShapeDtypeStruct((B, nbp, IDX_HEADS, CHUNK), jnp.float32),
        ],
        compiler_params=pltpu.CompilerParams(
            dimension_semantics=("arbitrary", "arbitrary"), vmem_limit_bytes=VMEM_LIMIT),
        name="in_projection",
    )(x, meta_pad, n1, wr, wqi, wk, wv, ww, qscale, kscale)


def _retention_kernel(r_ref, cos_ref, sin_ref, gain_ref, y_ref, state_ref):
    n = pl.program_id(1)

    @pl.when(n == 0)
    def _():
        state_ref[...] = jnp.zeros_like(state_ref)

    cos2 = cos_ref[...]
    sin2 = sin_ref[...]
    row = lax.broadcasted_iota(jnp.int32, (CHUNK, CHUNK), 0)
    col = lax.broadcasted_iota(jnp.int32, (CHUNK, CHUNK), 1)
    rowf = row.astype(jnp.float32)
    diff = (row - col).astype(jnp.float32)
    w = RET_HEADS * RET_DK
    bf = jnp.bfloat16

    for h in range(RET_HEADS):
        log_gamma = math.log(1.0 - 2.0 ** (-5.0 - h))
        sl = slice(h * 128, (h + 1) * 128)
        q = r_ref[0, :, sl]
        k = r_ref[0, :, w + h * 128: w + (h + 1) * 128]
        v = r_ref[0, :, 2 * w + h * 128: 2 * w + (h + 1) * 128].astype(bf)
        g = r_ref[0, :, 3 * w + h * 128: 3 * w + (h + 1) * 128]

        qr = q * cos2 + pltpu.roll(q, HALF, 1) * sin2
        kr = (k * cos2 + pltpu.roll(k, HALF, 1) * sin2) * (RET_DK ** -0.5)

        decay = jnp.where(diff >= 0, jnp.exp(log_gamma * jnp.maximum(diff, 0.0)), 0.0)
        scores = _dot_nt(qr.astype(bf), kr.astype(bf)) * decay
        o = _dot(scores.astype(bf), v)

        state = state_ref[h]
        xi = jnp.exp(log_gamma * (rowf + 1.0))
        o = o + _dot((qr * xi).astype(bf), state.astype(bf))

        zeta = jnp.exp(log_gamma * (CHUNK - 1.0 - rowf))
        kv = _dot_tn((kr * zeta).astype(bf), v)
        state_ref[h] = state * math.exp(log_gamma * CHUNK) + kv

        ms = jnp.mean(o * o, axis=-1, keepdims=True)
        on = o * lax.rsqrt(ms + EPS) * gain_ref[:, sl]
        gate = g * (1.0 / (1.0 + jnp.exp(-g)))
        y_ref[0, :, sl] = (gate * on).astype(y_ref.dtype)


def _retention(r_all, nb, cos2, sin2, gain):
    B = r_all.shape[0]
    tp = nb * CHUNK
    w = RET_HEADS * RET_DK
    return pl.pallas_call(
        _retention_kernel,
        grid=(B, nb),
        in_specs=[
            pl.BlockSpec((1, CHUNK, 4 * w), lambda b, n: (b, n, 0)),
            pl.BlockSpec((CHUNK, 128), lambda b, n: (n, 0)),
            pl.BlockSpec((CHUNK, 128), lambda b, n: (n, 0)),
            pl.BlockSpec((1, w), lambda b, n: (0, 0)),
        ],
        out_specs=pl.BlockSpec((1, CHUNK, w), lambda b, n: (b, jnp.maximum(n - 1, 0), 0)),
        out_shape=jax.ShapeDtypeStruct((B, tp - CHUNK, w), jnp.bfloat16),
        scratch_shapes=[pltpu.VMEM((RET_HEADS, RET_DK, RET_DK), jnp.float32)],
        compiler_params=pltpu.CompilerParams(
            dimension_semantics=("arbitrary", "arbitrary"), vmem_limit_bytes=VMEM_LIMIT),
        name="retention",
    )(r_all, cos2, sin2, gain)


def _dsa_kernel(qit_ref, iwt_ref, k1_ref, k2_ref, vt_ref, bias_ref, tri_ref, y_ref,
                key_ref, m_ref, acc_ref, *, topk):
    n = pl.program_id(1) + 1
    npair = (n + 2) // 2
    ngroup = (npair + 1) // 2
    bf = jnp.bfloat16
    H = DSA_HEADS
    KP = 2 * CHUNK

    qit = qit_ref[0, 0]
    row = lax.broadcasted_iota(jnp.int32, (KP, CHUNK), 0)
    col = lax.broadcasted_iota(jnp.int32, (KP, CHUNK), 1)
    q_idx = n * CHUNK + col
    hs = lambda h: slice(h * 128, (h + 1) * 128)

    iw_rows = [jnp.broadcast_to(iwt_ref[0, 0, h:h + 1, :], (KP, CHUNK)) for h in range(IDX_HEADS)]

    def score_pair(g, _):
        off = pl.multiple_of(g * KP, KP)
        s = _dot(k2_ref[0, pl.ds(off, KP), :], qit)
        sc = iw_rows[0] * jnp.maximum(s[:, hs(0)], 0.0)
        for h in range(1, IDX_HEADS):
            sc = sc + iw_rows[h] * jnp.maximum(s[:, hs(h)], 0.0)
        k_idx = g * KP + row
        valid = (k_idx <= q_idx) & (k_idx >= PAD)
        sc = jnp.where(sc == 0.0, 0.0, sc)
        sc = jnp.where(valid, sc, NEG)
        bits = lax.bitcast_convert_type(sc, jnp.int32)
        key_ref[g] = jnp.where(bits < 0, bits ^ jnp.int32(0x7FFFFFFF), bits)
        return 0

    lax.fori_loop(0, npair, score_pair, 0)

    neg_bits = int(np.float32(NEG).view(np.int32))
    neg_key = neg_bits ^ 0x7FFFFFFF
    if neg_key >= 2 ** 31:
        neg_key -= 2 ** 32

    def fill_pair(g, _):
        key_ref[g] = jnp.full((KP, CHUNK), neg_key, jnp.int32)
        return 0

    lax.fori_loop(npair, 2 * ngroup, fill_pair, 0)

    def count(pred_tile):
        return jnp.where(pred_tile, 1.0, 0.0).reshape(KP // 8, 8, CHUNK).sum(axis=0)

    def bisect(i, lo):
        cand = lo + jnp.left_shift(jnp.int32(1), 31 - i)
        cand_b = jnp.broadcast_to(cand, (KP, CHUNK))

        def count_group(gg, part):
            part = part + count(key_ref[2 * gg] >= cand_b)
            return part + count(key_ref[2 * gg + 1] >= cand_b)

        part = lax.fori_loop(0, ngroup, count_group, jnp.zeros((8, CHUNK), jnp.float32))
        cnt = jnp.sum(part, axis=0, keepdims=True)
        return jnp.where(cnt >= topk, cand, lo)

    thr = lax.fori_loop(0, 32, bisect, jnp.full((1, CHUNK), INT_MIN, jnp.int32))
    thr_b = jnp.broadcast_to(thr, (KP, CHUNK))

    def count_gt(g, part):
        return part + count(key_ref[g] > thr_b)

    n_gt = jnp.sum(lax.fori_loop(0, npair, count_gt, jnp.zeros((8, CHUNK), jnp.float32)),
                   axis=0, keepdims=True)
    n_tie = jnp.where(thr == neg_key, 0.0, topk - n_gt)
    n_tie_b = jnp.broadcast_to(n_tie, (KP, CHUNK))

    m_ref[...] = jnp.full_like(m_ref, NEG)
    acc_ref[...] = jnp.zeros_like(acc_ref)
    va = acc_ref.shape[0]

    def attend(g, ties_before, near):
        off = pl.multiple_of(g * KP, KP)
        key = key_ref[g]
        eq = key == thr_b
        eqf = jnp.where(eq, 1.0, 0.0)
        before = ties_before + _dot(tri_ref[...], eqf.astype(bf))
        sel = (key > thr_b) | (eq & (before < n_tie_b))

        s_all = _dot(k1_ref[0, pl.ds(off, KP), :], qit)
        ps = []
        for h in range(H):
            s = s_all[:, hs(h)]
            if near:
                ta = jnp.clip(n - 2 * g, 0, 2)
                tb = jnp.clip(n - 2 * g - 1, 0, 2)
                s = s + jnp.concatenate([bias_ref[ta, :, hs(h)], bias_ref[tb, :, hs(h)]], axis=0)
            s = jnp.where(sel, s, NEG)
            m_prev = m_ref[h:h + 1, :]
            m_new = jnp.maximum(m_prev, jnp.max(s, axis=0, keepdims=True))
            m_ref[h:h + 1, :] = m_new
            acc_ref[:, hs(h)] = acc_ref[:, hs(h)] * jnp.exp2(m_prev - m_new)
            ps.append(jnp.exp2(s - m_new).astype(bf))
        p = jnp.concatenate(ps, axis=1)
        acc_ref[...] += _dot(vt_ref[0, 0:va, pl.ds(off, KP)], p)
        return ties_before + jnp.sum(eqf, axis=0, keepdims=True)

    nfar = (n - 1) // 2
    ties = lax.fori_loop(0, nfar, functools.partial(attend, near=False),
                         jnp.zeros((KP, CHUNK), jnp.float32))
    lax.fori_loop(nfar, npair, functools.partial(attend, near=True), ties)

    lo = lax.broadcasted_iota(jnp.int32, (CHUNK, CHUNK), 1) < HALF
    for t in range(H // 2):
        tiles = []
        for h in (2 * t, 2 * t + 1):
            a = acc_ref[:, hs(h)] * (1.0 / acc_ref[HALF:HALF + 1, hs(h)])
            a = jnp.concatenate([a, jnp.zeros((CHUNK - va, CHUNK), jnp.float32)], axis=0)
            tiles.append(a.T)
        y_ref[0, :, hs(t)] = jnp.where(lo, tiles[0], pltpu.roll(tiles[1], HALF, 1)).astype(y_ref.dtype)


def _sparse_attention(qit, iwt, k1, k2, vt, bias_tbl, tri, topk, nb):
    B, nbp, _, hw = qit.shape
    tp = nbp * CHUNK
    npair_max = nbp // 2
    ngroup_max = (npair_max + 1) // 2
    kspec = pl.BlockSpec((1, tp, 128), lambda b, n: (b, 0, 0))
    acc_rows = 80
    return pl.pallas_call(
        functools.partial(_dsa_kernel, topk=topk),
        grid=(B, nb - 1),
        in_specs=[
            pl.BlockSpec((1, 1, 128, hw), lambda b, n: (b, n + 1, 0, 0)),
            pl.BlockSpec((1, 1, IDX_HEADS, CHUNK), lambda b, n: (b, n + 1, 0, 0)),
            kspec, kspec,
            pl.BlockSpec((1, 128, tp), lambda b, n: (b, 0, 0)),
            pl.BlockSpec(bias_tbl.shape, lambda b, n: (0, 0, 0)),
            pl.BlockSpec(tri.shape, lambda b, n: (0, 0)),
        ],
        out_specs=pl.BlockSpec((1, CHUNK, DSA_HEADS * DSA_DH), lambda b, n: (b, n, 0)),
        out_shape=jax.ShapeDtypeStruct((B, (nb - 1) * CHUNK, DSA_HEADS * DSA_DH), jnp.bfloat16),
        scratch_shapes=[
            pltpu.VMEM((2 * ngroup_max, 2 * CHUNK, CHUNK), jnp.int32),
            pltpu.VMEM((DSA_HEADS, CHUNK), jnp.float32),
            pltpu.VMEM((acc_rows, hw), jnp.float32),
        ],
        compiler_params=pltpu.CompilerParams(
            dimension_semantics=("arbitrary", "arbitrary"), vmem_limit_bytes=VMEM_LIMIT),
        name="sparse_attention",
    )(qit, iwt, k1, k2, vt, bias_tbl, tri)


def _mlp_kernel(x_ref, yr_ref, yd_ref, wor_ref, wod_ref, n2_ref, w1_ref, w2_ref, o_ref):
    h1 = x_ref[...] + _dot(yr_ref[...], wor_ref[...]) + _dot(yd_ref[...], wod_ref[...])
    ms = jnp.mean(h1 * h1, axis=-1, keepdims=True)
    u = (h1 * lax.rsqrt(ms + EPS) * n2_ref[...]).astype(jnp.bfloat16)
    o_ref[...] = h1
    d_ff = w1_ref.shape[1]
    for c in range(d_ff // FFN_CHUNK):
        sl = slice(c * FFN_CHUNK, (c + 1) * FFN_CHUNK)
        f = jnp.maximum(_dot(u, w1_ref[:, sl]), 0.0)
        o_ref[...] += _dot((f * f).astype(jnp.bfloat16), w2_ref[sl, :])


def _out_mlp(x2, yr, yd, wor, wod, n2, w1, w2):
    rows, D = x2.shape
    const = lambda a: pl.BlockSpec(a.shape, lambda i: (0, 0))
    tile = lambda w: pl.BlockSpec((ROW_TILE, w), lambda i: (i, 0))
    return pl.pallas_call(
        _mlp_kernel,
        grid=(rows // ROW_TILE,),
        in_specs=[tile(D), tile(yr.shape[1]), tile(yd.shape[1]),
                  const(wor), const(wod), const(n2), const(w1), const(w2)],
        out_specs=tile(D),
        out_shape=jax.ShapeDtypeStruct((rows, D), jnp.float32),
        compiler_params=pltpu.CompilerParams(
            dimension_semantics=("arbitrary",), vmem_limit_bytes=VMEM_LIMIT),
        name="out_mlp",
    )(x2, yr, yd, wor, wod, n2, w1, w2)


def kernel(x, meta_tokens, norm1_w, w_in, ret_norm_w, q_norm_w, k_norm_w, rel_bias,
           w_out, norm2_w, w_ff1, w_ff2):
    B, L, D = x.shape
    assert L % ROW_TILE == 0 and w_in.shape[0] == 1
    topk = min(TOPK_MAX, L // 4)
    nb = L // CHUNK + 1
    bf = jnp.bfloat16
    f32 = jnp.float32

    rw = RET_HEADS * RET_DK
    dw = DSA_HEADS * DSA_DH
    sizes = (rw, rw, rw, rw, dw, DSA_DH, DSA_DH, IDX_HEADS * DSA_DH, DSA_DH, IDX_HEADS)
    offs = np.concatenate([[0], np.cumsum(sizes)])
    col = lambda i: w_in[0][:, int(offs[i]):int(offs[i + 1])]
    wr = jnp.concatenate([col(0), col(1), col(2), col(3)], axis=1).astype(bf)
    wq = col(4).reshape(D, DSA_HEADS, DSA_DH)
    wiq = col(7).reshape(D, IDX_HEADS, DSA_DH)
    wqi = jnp.concatenate([wq, wiq], axis=2).reshape(D, DSA_HEADS * 128).astype(bf)
    wk = jnp.concatenate([col(5), col(8)], axis=1).astype(bf)
    wv = jnp.concatenate([col(6), jnp.zeros((D, 128 - DSA_DH), f32)], axis=1).astype(bf)
    ww = jnp.concatenate([col(9), jnp.zeros((D, 128 - IDX_HEADS), f32)], axis=1).astype(bf)

    idx_scale = jnp.full((DSA_DH,), DSA_DH ** -0.5, f32)
    qscale = jnp.concatenate([q_norm_w[0].astype(f32) * (DSA_DH ** -0.5 * LOG2E), idx_scale])[None]
    kscale = jnp.concatenate([k_norm_w[0].astype(f32), jnp.ones((DSA_DH,), f32)])[None]
    meta_pad = jnp.concatenate([jnp.zeros((PAD, D), x.dtype), meta_tokens.astype(x.dtype)], axis=0)

    cos2, sin2 = _rope_tables(nb)
    bias_tbl = _bias_tables(rel_bias)
    tri = jnp.asarray(np.tril(np.ones((2 * CHUNK, 2 * CHUNK), np.float32), k=-1), dtype=bf)

    r_all, qit, k1, k2, vt, iwt = _inproj(x, meta_pad, norm1_w[0][None].astype(f32),
                                          wr, wqi, wk, wv, ww, qscale, kscale)
    y_ret = _retention(r_all, nb, cos2, sin2, ret_norm_w[0][None].astype(f32))
    y_dsa = _sparse_attention(qit, iwt, k1, k2, vt, bias_tbl, tri, topk, nb)

    wo = w_out[0].astype(bf)
    out = _out_mlp(x.reshape(B * L, D), y_ret.reshape(B * L, rw), y_dsa.reshape(B * L, dw),
                   wo[:rw], wo[rw:], norm2_w[0][None].astype(f32),
                   w_ff1[0].astype(bf), w_ff2[0].astype(bf))
    return out.reshape(B, L, D)
```

```python
import functools
import math

import numpy as np
import jax
import jax.numpy as jnp
from jax import lax
from jax.experimental import pallas as pl
from jax.experimental.pallas import tpu as pltpu

N_META = 16
CHUNK = 128
RET_HEADS = 4
RET_DK = 128
DSA_HEADS = 8
DSA_DH = 64
IDX_HEADS = 8
TOPK_MAX = 256
N_BUCKETS = 32
MAX_DISTANCE = 128
ROPE_BASE = 10000.0
EPS = 1e-6
NEG = -1e30
PAD = CHUNK - N_META
HALF = 64
INT_MIN = -(2 ** 31)
LOG2E = math.log2(math.e)

FFN_CHUNK = 1024
ROW_TILE = 256
VMEM_LIMIT = 56 * 1024 * 1024


def _dot(a, b):
    return jnp.dot(a, b, preferred_element_type=jnp.float32)


def _dot_nt(a, b):
    return lax.dot_general(a, b, (((1,), (1,)), ((), ())), preferred_element_type=jnp.float32)


def _dot_tn(a, b):
    return lax.dot_general(a, b, (((0,), (0,)), ((), ())), preferred_element_type=jnp.float32)


def _bucket_ranges():
    max_exact = N_BUCKETS // 2
    d = np.arange(0, 2 * CHUNK)
    large = max_exact + (np.log(np.maximum(d, 1) / max_exact) / math.log(MAX_DISTANCE / max_exact)
                         * (N_BUCKETS - max_exact)).astype(np.int64)
    bucket = np.where(d < max_exact, d, np.minimum(large, N_BUCKETS - 1))
    out = []
    for b in range(N_BUCKETS - 1):
        idx = np.nonzero(bucket == b)[0]
        out.append((int(idx.min()), int(idx.max())))
    return out


def _rope_kernel(cos_ref, sin_ref):
    n = pl.program_id(0)
    row = lax.broadcasted_iota(jnp.int32, (CHUNK, 128), 0)
    lane = lax.broadcasted_iota(jnp.int32, (CHUNK, 128), 1)
    pos = (n * CHUNK + row - PAD).astype(jnp.float32)
    frac = (lane % HALF).astype(jnp.float32) / HALF
    inv = jnp.exp(-frac * math.log(ROPE_BASE))
    ang = pos * inv
    cos_ref[...] = jnp.cos(ang)
    s = jnp.sin(ang)
    sin_ref[...] = jnp.where(lane < HALF, -s, s)


def _rope_tables(nb):
    return pl.pallas_call(
        _rope_kernel,
        grid=(nb,),
        out_specs=[pl.BlockSpec((CHUNK, 128), lambda n: (n, 0))] * 2,
        out_shape=[jax.ShapeDtypeStruct((nb * CHUNK, 128), jnp.float32)] * 2,
        name="rope_tables",
    )()


def _bias_kernel(rb_ref, out_ref):
    row = lax.broadcasted_iota(jnp.int32, (CHUNK, CHUNK), 0)
    col = lax.broadcasted_iota(jnp.int32, (CHUNK, CHUNK), 1)
    ranges = _bucket_ranges()
    out_ref[2] = jnp.zeros(out_ref.shape[1:], jnp.float32)
    for h in range(DSA_HEADS):
        far = rb_ref[N_BUCKETS - 1, h]
        for t in range(2):
            dist = col - row + t * CHUNK
            tile = jnp.zeros((CHUNK, CHUNK), jnp.float32)
            for b, (lo, hi) in enumerate(ranges):
                tile = jnp.where((dist >= lo) & (dist <= hi), (rb_ref[b, h] - far) * LOG2E, tile)
            out_ref[t, :, h * 128:(h + 1) * 128] = tile


def _bias_tables(rel_bias):
    return pl.pallas_call(
        _bias_kernel,
        in_specs=[pl.BlockSpec(memory_space=pltpu.SMEM)],
        out_specs=pl.BlockSpec(memory_space=pltpu.VMEM),
        out_shape=jax.ShapeDtypeStruct((3, CHUNK, DSA_HEADS * 128), jnp.float32),
        name="bias_tables",
    )(rel_bias.astype(jnp.float32))


def _inproj_kernel(x_ref, meta_ref, n1_ref, wr_ref, wqi_ref, wk_ref, wv_ref, ww_ref,
                   qs_ref, ks_ref,
                   r_ref, qit_ref, k1_ref, k2_ref, vt_ref, iwt_ref):
    n = pl.program_id(1)
    last = pl.num_programs(1) - 1
    src = jnp.where(n == 0, meta_ref[...], x_ref[0])
    src = jnp.where(n == last, 0.0, src)
    ms = jnp.mean(src * src, axis=-1, keepdims=True)
    u = (src * lax.rsqrt(ms + EPS) * n1_ref[...]).astype(jnp.bfloat16)

    r_ref[0] = _dot(u, wr_ref[...])

    lane = lax.broadcasted_iota(jnp.int32, (CHUNK, 128), 1)
    lo = lane < HALF

    def head_rms(t):
        ssq = jnp.sum(jnp.where(lo, t * t, 0.0), axis=-1, keepdims=True)
        return lax.rsqrt(ssq / DSA_DH + EPS)

    pq = _dot(u, wqi_ref[...])
    for h in range(DSA_HEADS):
        t = pq[:, h * 128:(h + 1) * 128]
        scale = jnp.where(lo, head_rms(t), 1.0) * qs_ref[...]
        qit_ref[0, 0, :, h * 128:(h + 1) * 128] = (t * scale).T.astype(jnp.bfloat16)

    pk = _dot(u, wk_ref[...])
    kn = pk * head_rms(pk) * ks_ref[...]
    k1_ref[0] = jnp.where(lo, kn, 0.0).astype(jnp.bfloat16)
    k2_ref[0] = jnp.where(lo, 0.0, pk).astype(jnp.bfloat16)

    pv = _dot(u, wv_ref[...])
    vx = jnp.where(lo, pv, jnp.where(lane == HALF, 1.0, 0.0))
    vt_ref[0] = vx.T.astype(jnp.bfloat16)

    iw = _dot(u, ww_ref[...]) * (IDX_HEADS ** -0.5)
    iwt_ref[0, 0] = iw.T[0:IDX_HEADS, :]


def _inproj(x, meta_pad, n1, wr, wqi, wk, wv, ww, qscale, kscale):
    B, L, D = x.shape
    nb = L // CHUNK + 1
    nbp = nb + (nb % 2)
    tp = nbp * CHUNK
    const = lambda shape: pl.BlockSpec(shape, lambda b, n: (0,) * len(shape))
    blk = lambda w: pl.BlockSpec((1, CHUNK, w), lambda b, n: (b, n, 0))
    return pl.pallas_call(
        _inproj_kernel,
        grid=(B, nbp),
        in_specs=[
            pl.BlockSpec((1, CHUNK, D), lambda b, n: (b, jnp.clip(n - 1, 0, nb - 2), 0)),
            const(meta_pad.shape), const(n1.shape), const(wr.shape), const(wqi.shape),
            const(wk.shape), const(wv.shape), const(ww.shape), const(qscale.shape),
            const(kscale.shape),
        ],
        out_specs=[
            blk(wr.shape[1]),
            pl.BlockSpec((1, 1, 128, DSA_HEADS * 128), lambda b, n: (b, n, 0, 0)),
            blk(128), blk(128),
            pl.BlockSpec((1, 128, CHUNK), lambda b, n: (b, 0, n)),
            pl.BlockSpec((1, 1, IDX_HEADS, CHUNK), lambda b, n: (b, n, 0, 0)),
        ],
        out_shape=[
            jax.ShapeDtypeStruct((B, tp, wr.shape[1]), jnp.float32),
            jax.ShapeDtypeStruct((B, nbp, 128, DSA_HEADS * 128), jnp.bfloat16),
            jax.ShapeDtypeStruct((B, tp, 128), jnp.bfloat16),
            jax.ShapeDtypeStruct((B, tp, 128), jnp.bfloat16),
            jax.ShapeDtypeStruct((B, 128, tp), jnp.bfloat16),
            jax.ShapeDtypeStruct((B, nbp, IDX_HEADS, CHUNK), jnp.float32),
        ],
        compiler_params=pltpu.CompilerParams(
            dimension_semantics=("arbitrary", "arbitrary"), vmem_limit_bytes=VMEM_LIMIT),
        name="in_projection",
    )(x, meta_pad, n1, wr, wqi, wk, wv, ww, qscale, kscale)


def _retention_kernel(r_ref, cos_ref, sin_ref, gain_ref, y_ref, state_ref):
    n = pl.program_id(1)

    @pl.when(n == 0)
    def _():
        state_ref[...] = jnp.zeros_like(state_ref)

    cos2 = cos_ref[...]
    sin2 = sin_ref[...]
    row = lax.broadcasted_iota(jnp.int32, (CHUNK, CHUNK), 0)
    col = lax.broadcasted_iota(jnp.int32, (CHUNK, CHUNK), 1)
    rowf = row.astype(jnp.float32)
    diff = (row - col).astype(jnp.float32)
    w = RET_HEADS * RET_DK
    bf = jnp.bfloat16

    for h in range(RET_HEADS):
        log_gamma = math.log(1.0 - 2.0 ** (-5.0 - h))
        sl = slice(h * 128, (h + 1) * 128)
        q = r_ref[0, :, sl]
        k = r_ref[0, :, w + h * 128: w + (h + 1) * 128]
        v = r_ref[0, :, 2 * w + h * 128: 2 * w + (h + 1) * 128].astype(bf)
        g = r_ref[0, :, 3 * w + h * 128: 3 * w + (h + 1) * 128]

        qr = q * cos2 + pltpu.roll(q, HALF, 1) * sin2
        kr = (k * cos2 + pltpu.roll(k, HALF, 1) * sin2) * (RET_DK ** -0.5)

        decay = jnp.where(diff >= 0, jnp.exp(log_gamma * jnp.maximum(diff, 0.0)), 0.0)
        scores = _dot_nt(qr.astype(bf), kr.astype(bf)) * decay
        o = _dot(scores.astype(bf), v)

        state = state_ref[h]
        xi = jnp.exp(log_gamma * (rowf + 1.0))
        o = o + _dot((qr * xi).astype(bf), state.astype(bf))

        zeta = jnp.exp(log_gamma * (CHUNK - 1.0 - rowf))
        kv = _dot_tn((kr * zeta).astype(bf), v)
        state_ref[h] = state * math.exp(log_gamma * CHUNK) + kv

        ms = jnp.mean(o * o, axis=-1, keepdims=True)
        on = o * lax.rsqrt(ms + EPS) * gain_ref[:, sl]
        gate = g * (1.0 / (1.0 + jnp.exp(-g)))
        y_ref[0, :, sl] = (gate * on).astype(y_ref.dtype)


def _retention(r_all, nb, cos2, sin2, gain):
    B = r_all.shape[0]
    tp = nb * CHUNK
    w = RET_HEADS * RET_DK
    return pl.pallas_call(
        _retention_kernel,
        grid=(B, nb),
        in_specs=[
            pl.BlockSpec((1, CHUNK, 4 * w), lambda b, n: (b, n, 0)),
            pl.BlockSpec((CHUNK, 128), lambda b, n: (n, 0)),
            pl.BlockSpec((CHUNK, 128), lambda b, n: (n, 0)),
            pl.BlockSpec((1, w), lambda b, n: (0, 0)),
        ],
        out_specs=pl.BlockSpec((1, CHUNK, w), lambda b, n: (b, jnp.maximum(n - 1, 0), 0)),
        out_shape=jax.ShapeDtypeStruct((B, tp - CHUNK, w), jnp.bfloat16),
        scratch_shapes=[pltpu.VMEM((RET_HEADS, RET_DK, RET_DK), jnp.float32)],
        compiler_params=pltpu.CompilerParams(
            dimension_semantics=("arbitrary", "arbitrary"), vmem_limit_bytes=VMEM_LIMIT),
        name="retention",
    )(r_all, cos2, sin2, gain)


def _bit_planes(tile):
    a = [tile[r * 8:(r + 1) * 8, :] for r in range(32)]
    j, m = 16, 0x0000FFFF
    while j:
        for k in range(32):
            if (k & j) == 0:
                t = (a[k] ^ lax.shift_right_logical(a[k + j], j)) & m
                a[k] = a[k] ^ t
                a[k + j] = a[k + j] ^ (t << j)
        j >>= 1
        if j:
            m = m ^ ((m << j) & 0xFFFFFFFF)
    return a


def _sublane_total(x):
    x = x + pltpu.roll(x, 4, 0)
    x = x + pltpu.roll(x, 2, 0)
    return x + pltpu.roll(x, 1, 0)


def _dsa_kernel(qit_ref, iwt_ref, k1_ref, k2_ref, vt_ref, bias_ref, tri_ref, y_ref,
                key_ref, plane_ref, m_ref, acc_ref, *, topk):
    n = pl.program_id(1) + 1
    npair = (n + 2) // 2
    bf = jnp.bfloat16
    H = DSA_HEADS
    KP = 2 * CHUNK
    G = plane_ref.shape[1]

    @pl.when((pl.program_id(0) == 0) & (pl.program_id(1) == 0))
    def _():
        plane_ref[...] = jnp.zeros_like(plane_ref)

    row = lax.broadcasted_iota(jnp.int32, (KP, CHUNK), 0)
    col = lax.broadcasted_iota(jnp.int32, (KP, CHUNK), 1)
    q_idx = n * CHUNK + col
    hs = lambda h: slice(h * 128, (h + 1) * 128)
    hp = lambda t: slice(t * 256, (t + 1) * 256)

    iw_rows = [jnp.broadcast_to(iwt_ref[0, 0, h:h + 1, :], (KP, CHUNK)) for h in range(IDX_HEADS)]

    def score_pair(g, _):
        off = pl.multiple_of(g * KP, KP)
        kblk = k2_ref[0, pl.ds(off, KP), :]
        sc = None
        for t in range(H // 2):
            s = _dot(kblk, qit_ref[0, 0, :, hp(t)])
            for hh in range(2):
                term = iw_rows[2 * t + hh] * jnp.maximum(s[:, hs(hh)], 0.0)
                sc = term if sc is None else sc + term
        k_idx = g * KP + row
        valid = (k_idx <= q_idx) & (k_idx >= PAD)
        sc = jnp.where(sc == 0.0, 0.0, sc)
        sc = jnp.where(valid, sc, NEG)
        bits = lax.bitcast_convert_type(sc, jnp.int32)
        key = jnp.where(bits < 0, bits ^ jnp.int32(0x7FFFFFFF), bits)
        key_ref[g] = key
        planes = _bit_planes(key ^ jnp.int32(INT_MIN))
        for p in range(32):
            plane_ref[p, g] = planes[p]
        return 0

    lax.fori_loop(0, npair, score_pair, 0)

    neg_bits = int(np.float32(NEG).view(np.int32))
    neg_key = neg_bits ^ 0x7FFFFFFF
    if neg_key >= 2 ** 31:
        neg_key -= 2 ** 32
    g_idx = lax.broadcasted_iota(jnp.int32, (G, 8, CHUNK), 0)
    alive0 = jnp.where(g_idx < npair, jnp.int32(-1), jnp.int32(0))

    def ones_in(words):
        part = lax.population_count(words).astype(jnp.float32).sum(axis=0)
        return _sublane_total(part)

    def radix(p, carry):
        alive, above, thr, cnt = carry
        take1 = (above + cnt) >= topk
        above = jnp.where(take1, above, above + cnt)
        thr = jnp.where(take1, thr | lax.shift_right_logical(jnp.int32(INT_MIN), p), thr)
        drop = jnp.where(take1, jnp.int32(0), jnp.int32(-1))
        alive = alive & (plane_ref[p] ^ drop[None])
        cnt = ones_in(alive & plane_ref[jnp.minimum(p + 1, 31)])
        return alive, above, thr, cnt

    zero8 = jnp.zeros((8, CHUNK), jnp.float32)
    alive, n_gt, thr_u, _ = lax.fori_loop(
        0, 32, radix,
        (alive0, zero8, jnp.zeros((8, CHUNK), jnp.int32), ones_in(alive0 & plane_ref[0])))
    thr = thr_u ^ jnp.int32(INT_MIN)
    n_eq = ones_in(alive)
    masked_thr = thr == neg_key
    n_tie = jnp.where(masked_thr, 0.0, topk - n_gt)
    has_ties = jnp.max(jnp.where((n_eq > n_tie) & ~masked_thr, 1.0, 0.0)) > 0.5
    rep = lambda x: jnp.tile(x, (KP // 8, 1))
    thr_b = rep(thr)
    thr_sel_b = rep(jnp.where(masked_thr, jnp.int32(neg_key + 1), thr))
    n_tie_b = rep(n_tie)

    m_ref[...] = jnp.full_like(m_ref, NEG)
    acc_ref[...] = jnp.zeros_like(acc_ref)
    va = acc_ref.shape[0]

    def attend(g, ties_before, near, ties):
        off = pl.multiple_of(g * KP, KP)
        key = key_ref[g]
        if ties:
            eq = key == thr_b
            eqf = jnp.where(eq, 1.0, 0.0)
            before = ties_before + _dot(tri_ref[...], eqf.astype(bf))
            sel = (key > thr_b) | (eq & (before < n_tie_b))
            ties_before = ties_before + jnp.sum(eqf, axis=0, keepdims=True)
        else:
            sel = key >= thr_sel_b
        s_all = _dot(k1_ref[0, pl.ds(off, KP), :], qit_ref[0, 0])
        ps, alphas = [], []
        for h in range(H):
            s = s_all[:, hs(h)]
            if near:
                ta = jnp.clip(n - 2 * g, 0, 2)
                tb = jnp.clip(n - 2 * g - 1, 0, 2)
                s = s + jnp.concatenate([bias_ref[ta, :, hs(h)], bias_ref[tb, :, hs(h)]], axis=0)
            s = jnp.where(sel, s, NEG)
            m_prev = m_ref[h:h + 1, :]
            m_new = jnp.maximum(m_prev, jnp.max(s, axis=0, keepdims=True))
            m_ref[h:h + 1, :] = m_new
            alphas.append(jnp.exp2(m_prev - m_new))
            ps.append(jnp.exp2(s - m_new).astype(bf))
        pv = _dot(vt_ref[0, 0:va, pl.ds(off, KP)], jnp.concatenate(ps, axis=1))
        acc_ref[...] = acc_ref[...] * jnp.concatenate(alphas, axis=1) + pv
        return ties_before

    nfar = (n - 1) // 2

    def run(ties):
        def go():
            c = lax.fori_loop(0, nfar, functools.partial(attend, near=False, ties=ties),
                              jnp.zeros((KP, CHUNK), jnp.float32))
            lax.fori_loop(nfar, npair, functools.partial(attend, near=True, ties=ties), c)
        return go

    pl.when(has_ties)(run(True))
    pl.when(jnp.logical_not(has_ties))(run(False))

    lo = lax.broadcasted_iota(jnp.int32, (CHUNK, CHUNK), 1) < HALF
    for t in range(H // 2):
        tiles = []
        for h in (2 * t, 2 * t + 1):
            a = acc_ref[:, hs(h)] * (1.0 / acc_ref[HALF:HALF + 1, hs(h)])
            a = jnp.concatenate([a, jnp.zeros((CHUNK - va, CHUNK), jnp.float32)], axis=0)
            tiles.append(a.T)
        y_ref[0, :, hs(t)] = jnp.where(lo, tiles[0], pltpu.roll(tiles[1], HALF, 1)).astype(y_ref.dtype)


def _sparse_attention(qit, iwt, k1, k2, vt, bias_tbl, tri, topk, nb):
    B, nbp, _, hw = qit.shape
    tp = nbp * CHUNK
    npair_max = nbp // 2
    kspec = pl.BlockSpec((1, tp, 128), lambda b, n: (b, 0, 0))
    acc_rows = 80
    return pl.pallas_call(
        functools.partial(_dsa_kernel, topk=topk),
        grid=(B, nb - 1),
        in_specs=[
            pl.BlockSpec((1, 1, 128, hw), lambda b, n: (b, n + 1, 0, 0)),
            pl.BlockSpec((1, 1, IDX_HEADS, CHUNK), lambda b, n: (b, n + 1, 0, 0)),
            kspec, kspec,
            pl.BlockSpec((1, 128, tp), lambda b, n: (b, 0, 0)),
            pl.BlockSpec(bias_tbl.shape, lambda b, n: (0, 0, 0)),
            pl.BlockSpec(tri.shape, lambda b, n: (0, 0)),
        ],
        out_specs=pl.BlockSpec((1, CHUNK, DSA_HEADS * DSA_DH), lambda b, n: (b, n, 0)),
        out_shape=jax.ShapeDtypeStruct((B, (nb - 1) * CHUNK, DSA_HEADS * DSA_DH), jnp.bfloat16),
        scratch_shapes=[
            pltpu.VMEM((npair_max, 2 * CHUNK, CHUNK), jnp.int32),
            pltpu.VMEM((32, npair_max, 8, CHUNK), jnp.int32),
            pltpu.VMEM((DSA_HEADS, CHUNK), jnp.float32),
            pltpu.VMEM((acc_rows, hw), jnp.float32),
        ],
        compiler_params=pltpu.CompilerParams(
            dimension_semantics=("arbitrary", "arbitrary"), vmem_limit_bytes=VMEM_LIMIT),
        name="sparse_attention",
    )(qit, iwt, k1, k2, vt, bias_tbl, tri)


def _mlp_kernel(x_ref, yr_ref, yd_ref, wor_ref, wod_ref, n2_ref, w1_ref, w2_ref, o_ref):
    h1 = x_ref[...] + _dot(yr_ref[...], wor_ref[...]) + _dot(yd_ref[...], wod_ref[...])
    ms = jnp.mean(h1 * h1, axis=-1, keepdims=True)
    u = (h1 * lax.rsqrt(ms + EPS) * n2_ref[...]).astype(jnp.bfloat16)
    o_ref[...] = h1
    d_ff = w1_ref.shape[1]
    for c in range(d_ff // FFN_CHUNK):
        sl = slice(c * FFN_CHUNK, (c + 1) * FFN_CHUNK)
        f = jnp.maximum(_dot(u, w1_ref[:, sl]), 0.0)
        o_ref[...] += _dot((f * f).astype(jnp.bfloat16), w2_ref[sl, :])


def _out_mlp(x2, yr, yd, wor, wod, n2, w1, w2):
    rows, D = x2.shape
    const = lambda a: pl.BlockSpec(a.shape, lambda i: (0, 0))
    tile = lambda w: pl.BlockSpec((ROW_TILE, w), lambda i: (i, 0))
    return pl.pallas_call(
        _mlp_kernel,
        grid=(rows // ROW_TILE,),
        in_specs=[tile(D), tile(yr.shape[1]), tile(yd.shape[1]),
                  const(wor), const(wod), const(n2), const(w1), const(w2)],
        out_specs=tile(D),
        out_shape=jax.ShapeDtypeStruct((rows, D), jnp.float32),
        compiler_params=pltpu.CompilerParams(
            dimension_semantics=("arbitrary",), vmem_limit_bytes=VMEM_LIMIT),
        name="out_mlp",
    )(x2, yr, yd, wor, wod, n2, w1, w2)


def kernel(x, meta_tokens, norm1_w, w_in, ret_norm_w, q_norm_w, k_norm_w, rel_bias,
           w_out, norm2_w, w_ff1, w_ff2):
    B, L, D = x.shape
    assert L % ROW_TILE == 0 and w_in.shape[0] == 1
    topk = min(TOPK_MAX, L // 4)
    nb = L // CHUNK + 1
    bf = jnp.bfloat16
    f32 = jnp.float32

    rw = RET_HEADS * RET_DK
    dw = DSA_HEADS * DSA_DH
    sizes = (rw, rw, rw, rw, dw, DSA_DH, DSA_DH, IDX_HEADS * DSA_DH, DSA_DH, IDX_HEADS)
    offs = np.concatenate([[0], np.cumsum(sizes)])
    col = lambda i: w_in[0][:, int(offs[i]):int(offs[i + 1])]
    wr = jnp.concatenate([col(0), col(1), col(2), col(3)], axis=1).astype(bf)
    wq = col(4).reshape(D, DSA_HEADS, DSA_DH)
    wiq = col(7).reshape(D, IDX_HEADS, DSA_DH)
    wqi = jnp.concatenate([wq, wiq], axis=2).reshape(D, DSA_HEADS * 128).astype(bf)
    wk = jnp.concatenate([col(5), col(8)], axis=1).astype(bf)
    wv = jnp.concatenate([col(6), jnp.zeros((D, 128 - DSA_DH), f32)], axis=1).astype(bf)
    ww = jnp.concatenate([col(9), jnp.zeros((D, 128 - IDX_HEADS), f32)], axis=1).astype(bf)

    idx_scale = jnp.full((DSA_DH,), DSA_DH ** -0.5, f32)
    qscale = jnp.concatenate([q_norm_w[0].astype(f32) * (DSA_DH ** -0.5 * LOG2E), idx_scale])[None]
    kscale = jnp.concatenate([k_norm_w[0].astype(f32), jnp.ones((DSA_DH,), f32)])[None]
    meta_pad = jnp.concatenate([jnp.zeros((PAD, D), x.dtype), meta_tokens.astype(x.dtype)], axis=0)

    cos2, sin2 = _rope_tables(nb)
    bias_tbl = _bias_tables(rel_bias)
    tri = jnp.asarray(np.tril(np.ones((2 * CHUNK, 2 * CHUNK), np.float32), k=-1), dtype=bf)

    r_all, qit, k1, k2, vt, iwt = _inproj(x, meta_pad, norm1_w[0][None].astype(f32),
                                          wr, wqi, wk, wv, ww, qscale, kscale)
    y_ret = _retention(r_all, nb, cos2, sin2, ret_norm_w[0][None].astype(f32))
    y_dsa = _sparse_attention(qit, iwt, k1, k2, vt, bias_tbl, tri, topk, nb)

    wo = w_out[0].astype(bf)
    out = _out_mlp(x.reshape(B * L, D), y_ret.reshape(B * L, rw), y_dsa.reshape(B * L, dw),
                   wo[:rw], wo[rw:], norm2_w[0][None].astype(f32),
                   w_ff1[0].astype(bf), w_ff2[0].astype(bf))
    return out.reshape(B, L, D)
```

```python
import functools
import math

import numpy as np
import jax
import jax.numpy as jnp
from jax import lax
from jax.experimental import pallas as pl
from jax.experimental.pallas import tpu as pltpu

N_META = 16
CHUNK = 128
RET_HEADS = 4
RET_DK = 128
DSA_HEADS = 8
DSA_DH = 64
IDX_HEADS = 8
TOPK_MAX = 256
N_BUCKETS = 32
MAX_DISTANCE = 128
ROPE_BASE = 10000.0
EPS = 1e-6
NEG = -1e30
PAD = CHUNK - N_META
HALF = 64
INT_MIN = -(2 ** 31)
LOG2E = math.log2(math.e)

FFN_CHUNK = 1024
ROW_TILE = 256
VMEM_LIMIT = 56 * 1024 * 1024


def _dot(a, b):
    return jnp.dot(a, b, preferred_element_type=jnp.float32)


def _dot_nt(a, b):
    return lax.dot_general(a, b, (((1,), (1,)), ((), ())), preferred_element_type=jnp.float32)


def _dot_tn(a, b):
    return lax.dot_general(a, b, (((0,), (0,)), ((), ())), preferred_element_type=jnp.float32)


def _bucket_ranges():
    max_exact = N_BUCKETS // 2
    d = np.arange(0, 2 * CHUNK)
    large = max_exact + (np.log(np.maximum(d, 1) / max_exact) / math.log(MAX_DISTANCE / max_exact)
                         * (N_BUCKETS - max_exact)).astype(np.int64)
    bucket = np.where(d < max_exact, d, np.minimum(large, N_BUCKETS - 1))
    out = []
    for b in range(N_BUCKETS - 1):
        idx = np.nonzero(bucket == b)[0]
        out.append((int(idx.min()), int(idx.max())))
    return out


def _rope_kernel(cos_ref, sin_ref):
    n = pl.program_id(0)
    row = lax.broadcasted_iota(jnp.int32, (CHUNK, 128), 0)
    lane = lax.broadcasted_iota(jnp.int32, (CHUNK, 128), 1)
    pos = (n * CHUNK + row - PAD).astype(jnp.float32)
    frac = (lane % HALF).astype(jnp.float32) / HALF
    inv = jnp.exp(-frac * math.log(ROPE_BASE))
    ang = pos * inv
    cos_ref[...] = jnp.cos(ang)
    s = jnp.sin(ang)
    sin_ref[...] = jnp.where(lane < HALF, -s, s)


def _rope_tables(nb):
    return pl.pallas_call(
        _rope_kernel,
        grid=(nb,),
        out_specs=[pl.BlockSpec((CHUNK, 128), lambda n: (n, 0))] * 2,
        out_shape=[jax.ShapeDtypeStruct((nb * CHUNK, 128), jnp.float32)] * 2,
        name="rope_tables",
    )()


def _bias_kernel(rb_ref, out_ref):
    row = lax.broadcasted_iota(jnp.int32, (CHUNK, CHUNK), 0)
    col = lax.broadcasted_iota(jnp.int32, (CHUNK, CHUNK), 1)
    ranges = _bucket_ranges()
    out_ref[2] = jnp.zeros(out_ref.shape[1:], jnp.float32)
    for h in range(DSA_HEADS):
        far = rb_ref[N_BUCKETS - 1, h]
        for t in range(2):
            dist = col - row + t * CHUNK
            tile = jnp.zeros((CHUNK, CHUNK), jnp.float32)
            for b, (lo, hi) in enumerate(ranges):
                tile = jnp.where((dist >= lo) & (dist <= hi), (rb_ref[b, h] - far) * LOG2E, tile)
            out_ref[t, :, h * 128:(h + 1) * 128] = tile


def _bias_tables(rel_bias):
    return pl.pallas_call(
        _bias_kernel,
        in_specs=[pl.BlockSpec(memory_space=pltpu.SMEM)],
        out_specs=pl.BlockSpec(memory_space=pltpu.VMEM),
        out_shape=jax.ShapeDtypeStruct((3, CHUNK, DSA_HEADS * 128), jnp.float32),
        name="bias_tables",
    )(rel_bias.astype(jnp.float32))


def _inproj_kernel(x_ref, meta_ref, n1_ref, wr_ref, wqi_ref, wk_ref, wv_ref, ww_ref,
                   qs_ref, ks_ref,
                   r_ref, qit_ref, k1_ref, k2_ref, vt_ref, iwt_ref):
    n = pl.program_id(1)
    last = pl.num_programs(1) - 1
    src = jnp.where(n == 0, meta_ref[...], x_ref[0])
    src = jnp.where(n == last, 0.0, src)
    ms = jnp.mean(src * src, axis=-1, keepdims=True)
    u = (src * lax.rsqrt(ms + EPS) * n1_ref[...]).astype(jnp.bfloat16)

    r_ref[0] = _dot(u, wr_ref[...])

    lane = lax.broadcasted_iota(jnp.int32, (CHUNK, 128), 1)
    lo = lane < HALF

    def head_rms(t):
        ssq = jnp.sum(jnp.where(lo, t * t, 0.0), axis=-1, keepdims=True)
        return lax.rsqrt(ssq / DSA_DH + EPS)

    pq = _dot(u, wqi_ref[...])
    for h in range(DSA_HEADS):
        t = pq[:, h * 128:(h + 1) * 128]
        scale = jnp.where(lo, head_rms(t), 1.0) * qs_ref[...]
        qit_ref[0, 0, :, h * 128:(h + 1) * 128] = (t * scale).T.astype(jnp.bfloat16)

    pk = _dot(u, wk_ref[...])
    kn = pk * head_rms(pk) * ks_ref[...]
    k1_ref[0] = jnp.where(lo, kn, 0.0).astype(jnp.bfloat16)
    k2_ref[0] = jnp.where(lo, 0.0, pk).astype(jnp.bfloat16)

    pv = _dot(u, wv_ref[...])
    vx = jnp.where(lo, pv, jnp.where(lane == HALF, 1.0, 0.0))
    vt_ref[0] = vx.T.astype(jnp.bfloat16)

    iw = _dot(u, ww_ref[...]) * (IDX_HEADS ** -0.5)
    iwt_ref[0, 0] = iw.T[0:IDX_HEADS, :]


def _inproj(x, meta_pad, n1, wr, wqi, wk, wv, ww, qscale, kscale):
    B, L, D = x.shape
    nb = L // CHUNK + 1
    nbp = nb + (nb % 2)
    tp = nbp * CHUNK
    const = lambda shape: pl.BlockSpec(shape, lambda b, n: (0,) * len(shape))
    blk = lambda w: pl.BlockSpec((1, CHUNK, w), lambda b, n: (b, n, 0))
    return pl.pallas_call(
        _inproj_kernel,
        grid=(B, nbp),
        in_specs=[
            pl.BlockSpec((1, CHUNK, D), lambda b, n: (b, jnp.clip(n - 1, 0, nb - 2), 0)),
            const(meta_pad.shape), const(n1.shape), const(wr.shape), const(wqi.shape),
            const(wk.shape), const(wv.shape), const(ww.shape), const(qscale.shape),
            const(kscale.shape),
        ],
        out_specs=[
            blk(wr.shape[1]),
            pl.BlockSpec((1, 1, 128, DSA_HEADS * 128), lambda b, n: (b, n, 0, 0)),
            blk(128), blk(128),
            pl.BlockSpec((1, 128, CHUNK), lambda b, n: (b, 0, n)),
            pl.BlockSpec((1, 1, IDX_HEADS, CHUNK), lambda b, n: (b, n, 0, 0)),
        ],
        out_shape=[
            jax.ShapeDtypeStruct((B, tp, wr.shape[1]), jnp.float32),
            jax.ShapeDtypeStruct((B, nbp, 128, DSA_HEADS * 128), jnp.bfloat16),
            jax.ShapeDtypeStruct((B, tp, 128), jnp.bfloat16),
            jax.ShapeDtypeStruct((B, tp, 128), jnp.bfloat16),
            jax.ShapeDtypeStruct((B, 128, tp), jnp.bfloat16),
            jax.ShapeDtypeStruct((B, nbp, IDX_HEADS, CHUNK), jnp.float32),
        ],
        compiler_params=pltpu.CompilerParams(
            dimension_semantics=("arbitrary", "arbitrary"), vmem_limit_bytes=VMEM_LIMIT),
        name="in_projection",
    )(x, meta_pad, n1, wr, wqi, wk, wv, ww, qscale, kscale)


def _retention_kernel(r_ref, cos_ref, sin_ref, gain_ref, y_ref, state_ref):
    n = pl.program_id(1)

    @pl.when(n == 0)
    def _():
        state_ref[...] = jnp.zeros_like(state_ref)

    cos2 = cos_ref[...]
    sin2 = sin_ref[...]
    row = lax.broadcasted_iota(jnp.int32, (CHUNK, CHUNK), 0)
    col = lax.broadcasted_iota(jnp.int32, (CHUNK, CHUNK), 1)
    rowf = row.astype(jnp.float32)
    diff = (row - col).astype(jnp.float32)
    w = RET_HEADS * RET_DK
    bf = jnp.bfloat16

    for h in range(RET_HEADS):
        log_gamma = math.log(1.0 - 2.0 ** (-5.0 - h))
        sl = slice(h * 128, (h + 1) * 128)
        q = r_ref[0, :, sl]
        k = r_ref[0, :, w + h * 128: w + (h + 1) * 128]
        v = r_ref[0, :, 2 * w + h * 128: 2 * w + (h + 1) * 128].astype(bf)
        g = r_ref[0, :, 3 * w + h * 128: 3 * w + (h + 1) * 128]

        qr = q * cos2 + pltpu.roll(q, HALF, 1) * sin2
        kr = (k * cos2 + pltpu.roll(k, HALF, 1) * sin2) * (RET_DK ** -0.5)

        decay = jnp.where(diff >= 0, jnp.exp(log_gamma * jnp.maximum(diff, 0.0)), 0.0)
        scores = _dot_nt(qr.astype(bf), kr.astype(bf)) * decay
        o = _dot(scores.astype(bf), v)

        state = state_ref[h]
        xi = jnp.exp(log_gamma * (rowf + 1.0))
        o = o + _dot((qr * xi).astype(bf), state.astype(bf))

        zeta = jnp.exp(log_gamma * (CHUNK - 1.0 - rowf))
        kv = _dot_tn((kr * zeta).astype(bf), v)
        state_ref[h] = state * math.exp(log_gamma * CHUNK) + kv

        ms = jnp.mean(o * o, axis=-1, keepdims=True)
        on = o * lax.rsqrt(ms + EPS) * gain_ref[:, sl]
        gate = g * (1.0 / (1.0 + jnp.exp(-g)))
        y_ref[0, :, sl] = (gate * on).astype(y_ref.dtype)


def _retention(r_all, nb, cos2, sin2, gain):
    B = r_all.shape[0]
    tp = nb * CHUNK
    w = RET_HEADS * RET_DK
    return pl.pallas_call(
        _retention_kernel,
        grid=(B, nb),
        in_specs=[
            pl.BlockSpec((1, CHUNK, 4 * w), lambda b, n: (b, n, 0)),
            pl.BlockSpec((CHUNK, 128), lambda b, n: (n, 0)),
            pl.BlockSpec((CHUNK, 128), lambda b, n: (n, 0)),
            pl.BlockSpec((1, w), lambda b, n: (0, 0)),
        ],
        out_specs=pl.BlockSpec((1, CHUNK, w), lambda b, n: (b, jnp.maximum(n - 1, 0), 0)),
        out_shape=jax.ShapeDtypeStruct((B, tp - CHUNK, w), jnp.bfloat16),
        scratch_shapes=[pltpu.VMEM((RET_HEADS, RET_DK, RET_DK), jnp.float32)],
        compiler_params=pltpu.CompilerParams(
            dimension_semantics=("arbitrary", "arbitrary"), vmem_limit_bytes=VMEM_LIMIT),
        name="retention",
    )(r_all, cos2, sin2, gain)


def _bit_planes(tile):
    a = [tile[r * 8:(r + 1) * 8, :] for r in range(32)]
    j, m = 16, 0x0000FFFF
    while j:
        for k in range(32):
            if (k & j) == 0:
                t = (a[k] ^ lax.shift_right_logical(a[k + j], j)) & m
                a[k] = a[k] ^ t
                a[k + j] = a[k + j] ^ (t << j)
        j >>= 1
        if j:
            m = m ^ ((m << j) & 0xFFFFFFFF)
    return a


def _sublane_total(x):
    x = x + pltpu.roll(x, 4, 0)
    x = x + pltpu.roll(x, 2, 0)
    return x + pltpu.roll(x, 1, 0)


def _dsa_kernel(qit_ref, iwt_ref, k1_ref, k2_ref, vt_ref, bias_ref, tri_ref, y_ref,
                key_ref, plane_ref, m_ref, acc_ref, sa_ref, sb_ref, *, topk):
    n = pl.program_id(1) + 1
    npair = (n + 2) // 2
    bf = jnp.bfloat16
    H = DSA_HEADS
    KP = 2 * CHUNK
    G = plane_ref.shape[1]

    @pl.when((pl.program_id(0) == 0) & (pl.program_id(1) == 0))
    def _():
        plane_ref[...] = jnp.zeros_like(plane_ref)

    row = lax.broadcasted_iota(jnp.int32, (KP, CHUNK), 0)
    col = lax.broadcasted_iota(jnp.int32, (KP, CHUNK), 1)
    q_idx = n * CHUNK + col
    hs = lambda h: slice(h * 128, (h + 1) * 128)
    hp = lambda t: slice(t * 256, (t + 1) * 256)

    iw_rows = [jnp.broadcast_to(iwt_ref[0, 0, h:h + 1, :], (KP, CHUNK)) for h in range(IDX_HEADS)]

    def key_rows(g):
        last = k1_ref.shape[1] // KP - 1
        return pl.ds(pl.multiple_of(jnp.minimum(g, last) * KP, KP), KP)

    def score_pair(g):
        kblk = k2_ref[0, key_rows(g), :]
        sc = None
        for t in range(H // 2):
            s = _dot(kblk, qit_ref[0, 0, :, hp(t)])
            for hh in range(2):
                term = iw_rows[2 * t + hh] * jnp.maximum(s[:, hs(hh)], 0.0)
                sc = term if sc is None else sc + term
        k_idx = g * KP + row
        valid = (k_idx <= q_idx) & (k_idx >= PAD)
        sc = jnp.where(sc == 0.0, 0.0, sc)
        sc = jnp.where(valid, sc, NEG)
        bits = lax.bitcast_convert_type(sc, jnp.int32)
        key = jnp.where(bits < 0, bits ^ jnp.int32(0x7FFFFFFF), bits)
        key_ref[g] = key
        planes = _bit_planes(key ^ jnp.int32(INT_MIN))
        for p in range(32):
            plane_ref[p, g] = planes[p]

    def score_two(i, _):
        score_pair(2 * i)
        score_pair(2 * i + 1)
        return 0

    niter = (npair + 1) // 2
    lax.fori_loop(0, niter, score_two, 0)

    neg_bits = int(np.float32(NEG).view(np.int32))
    neg_key = neg_bits ^ 0x7FFFFFFF
    if neg_key >= 2 ** 31:
        neg_key -= 2 ** 32
    g_idx = lax.broadcasted_iota(jnp.int32, (G, 8, CHUNK), 0)
    alive0 = jnp.where(g_idx < npair, jnp.int32(-1), jnp.int32(0))

    def ones_in(words):
        part = lax.population_count(words).astype(jnp.float32).sum(axis=0)
        return _sublane_total(part)

    def radix(p, carry):
        alive, above, thr, cnt = carry
        take1 = (above + cnt) >= topk
        above = jnp.where(take1, above, above + cnt)
        thr = jnp.where(take1, thr | lax.shift_right_logical(jnp.int32(INT_MIN), p), thr)
        drop = jnp.where(take1, jnp.int32(0), jnp.int32(-1))
        alive = alive & (plane_ref[p] ^ drop[None])
        cnt = ones_in(alive & plane_ref[jnp.minimum(p + 1, 31)])
        return alive, above, thr, cnt

    zero8 = jnp.zeros((8, CHUNK), jnp.float32)
    alive, n_gt, thr_u, _ = lax.fori_loop(
        0, 32, radix,
        (alive0, zero8, jnp.zeros((8, CHUNK), jnp.int32), ones_in(alive0 & plane_ref[0])))
    thr = thr_u ^ jnp.int32(INT_MIN)
    n_eq = ones_in(alive)
    masked_thr = thr == neg_key
    n_tie = jnp.where(masked_thr, 0.0, topk - n_gt)
    has_ties = jnp.max(jnp.where((n_eq > n_tie) & ~masked_thr, 1.0, 0.0)) > 0.5
    rep = lambda x: jnp.tile(x, (KP // 8, 1))
    thr_b = rep(thr)
    thr_sel_b = rep(jnp.where(masked_thr, jnp.int32(neg_key + 1), thr))
    n_tie_b = rep(n_tie)

    m_ref[...] = jnp.full_like(m_ref, NEG)
    acc_ref[...] = jnp.zeros_like(acc_ref)
    va = acc_ref.shape[0]

    def logits(g, dst_ref):
        dst_ref[...] = _dot(k1_ref[0, key_rows(g), :], qit_ref[0, 0])

    def attend(g, s_ref, ties_before, near, ties):
        key = key_ref[g]
        if ties:
            eq = key == thr_b
            eqf = jnp.where(eq, 1.0, 0.0)
            before = ties_before + _dot(tri_ref[...], eqf.astype(bf))
            sel = (key > thr_b) | (eq & (before < n_tie_b))
            ties_before = ties_before + jnp.sum(eqf, axis=0, keepdims=True)
        else:
            sel = key >= thr_sel_b
        ps, alphas = [], []
        for h in range(H):
            s = s_ref[:, hs(h)]
            if near:
                ta = jnp.clip(n - 2 * g, 0, 2)
                tb = jnp.clip(n - 2 * g - 1, 0, 2)
                s = s + jnp.concatenate([bias_ref[ta, :, hs(h)], bias_ref[tb, :, hs(h)]], axis=0)
            s = jnp.where(sel, s, NEG)
            m_prev = m_ref[h:h + 1, :]
            m_new = jnp.maximum(m_prev, jnp.max(s, axis=0, keepdims=True))
            m_ref[h:h + 1, :] = m_new
            alphas.append(jnp.exp2(m_prev - m_new))
            ps.append(jnp.exp2(s - m_new).astype(bf))
        pv = _dot(vt_ref[0, 0:va, key_rows(g)], jnp.concatenate(ps, axis=1))
        acc_ref[...] = acc_ref[...] * jnp.concatenate(alphas, axis=1) + pv
        return ties_before

    def attend_two(i, ties_before, near, ties):
        logits(2 * i + 1, sb_ref)
        ties_before = attend(2 * i, sa_ref, ties_before, near, ties)
        logits(2 * i + 2, sa_ref)
        return attend(2 * i + 1, sb_ref, ties_before, near, ties)

    nfar_iter = ((n - 1) // 2) // 2

    def run(ties):
        def go():
            logits(0, sa_ref)
            c = lax.fori_loop(0, nfar_iter, functools.partial(attend_two, near=False, ties=ties),
                              jnp.zeros((KP, CHUNK), jnp.float32))
            lax.fori_loop(nfar_iter, niter, functools.partial(attend_two, near=True, ties=ties), c)
        return go

    pl.when(has_ties)(run(True))
    pl.when(jnp.logical_not(has_ties))(run(False))

    lo = lax.broadcasted_iota(jnp.int32, (CHUNK, CHUNK), 1) < HALF
    for t in range(H // 2):
        tiles = []
        for h in (2 * t, 2 * t + 1):
            a = acc_ref[:, hs(h)] * (1.0 / acc_ref[HALF:HALF + 1, hs(h)])
            a = jnp.concatenate([a, jnp.zeros((CHUNK - va, CHUNK), jnp.float32)], axis=0)
            tiles.append(a.T)
        y_ref[0, :, hs(t)] = jnp.where(lo, tiles[0], pltpu.roll(tiles[1], HALF, 1)).astype(y_ref.dtype)


def _sparse_attention(qit, iwt, k1, k2, vt, bias_tbl, tri, topk, nb):
    B, nbp, _, hw = qit.shape
    tp = nbp * CHUNK
    nstep = nbp // 2
    nstep += nstep % 2
    kspec = pl.BlockSpec((1, tp, 128), lambda b, n: (b, 0, 0))
    acc_rows = 80
    return pl.pallas_call(
        functools.partial(_dsa_kernel, topk=topk),
        grid=(B, nb - 1),
        in_specs=[
            pl.BlockSpec((1, 1, 128, hw), lambda b, n: (b, n + 1, 0, 0)),
            pl.BlockSpec((1, 1, IDX_HEADS, CHUNK), lambda b, n: (b, n + 1, 0, 0)),
            kspec, kspec,
            pl.BlockSpec((1, 128, tp), lambda b, n: (b, 0, 0)),
            pl.BlockSpec(bias_tbl.shape, lambda b, n: (0, 0, 0)),
            pl.BlockSpec(tri.shape, lambda b, n: (0, 0)),
        ],
        out_specs=pl.BlockSpec((1, CHUNK, DSA_HEADS * DSA_DH), lambda b, n: (b, n, 0)),
        out_shape=jax.ShapeDtypeStruct((B, (nb - 1) * CHUNK, DSA_HEADS * DSA_DH), jnp.bfloat16),
        scratch_shapes=[
            pltpu.VMEM((nstep, 2 * CHUNK, CHUNK), jnp.int32),
            pltpu.VMEM((32, nstep, 8, CHUNK), jnp.int32),
            pltpu.VMEM((DSA_HEADS, CHUNK), jnp.float32),
            pltpu.VMEM((acc_rows, hw), jnp.float32),
            pltpu.VMEM((2 * CHUNK, hw), jnp.float32),
            pltpu.VMEM((2 * CHUNK, hw), jnp.float32),
        ],
        compiler_params=pltpu.CompilerParams(
            dimension_semantics=("arbitrary", "arbitrary"), vmem_limit_bytes=VMEM_LIMIT),
        name="sparse_attention",
    )(qit, iwt, k1, k2, vt, bias_tbl, tri)


def _mlp_kernel(x_ref, yr_ref, yd_ref, wor_ref, wod_ref, n2_ref, w1_ref, w2_ref, o_ref):
    h1 = x_ref[...] + _dot(yr_ref[...], wor_ref[...]) + _dot(yd_ref[...], wod_ref[...])
    ms = jnp.mean(h1 * h1, axis=-1, keepdims=True)
    u = (h1 * lax.rsqrt(ms + EPS) * n2_ref[...]).astype(jnp.bfloat16)
    o_ref[...] = h1
    d_ff = w1_ref.shape[1]
    for c in range(d_ff // FFN_CHUNK):
        sl = slice(c * FFN_CHUNK, (c + 1) * FFN_CHUNK)
        f = jnp.maximum(_dot(u, w1_ref[:, sl]), 0.0)
        o_ref[...] += _dot((f * f).astype(jnp.bfloat16), w2_ref[sl, :])


def _out_mlp(x2, yr, yd, wor, wod, n2, w1, w2):
    rows, D = x2.shape
    const = lambda a: pl.BlockSpec(a.shape, lambda i: (0, 0))
    tile = lambda w: pl.BlockSpec((ROW_TILE, w), lambda i: (i, 0))
    return pl.pallas_call(
        _mlp_kernel,
        grid=(rows // ROW_TILE,),
        in_specs=[tile(D), tile(yr.shape[1]), tile(yd.shape[1]),
                  const(wor), const(wod), const(n2), const(w1), const(w2)],
        out_specs=tile(D),
        out_shape=jax.ShapeDtypeStruct((rows, D), jnp.float32),
        compiler_params=pltpu.CompilerParams(
            dimension_semantics=("arbitrary",), vmem_limit_bytes=VMEM_LIMIT),
        name="out_mlp",
    )(x2, yr, yd, wor, wod, n2, w1, w2)


def kernel(x, meta_tokens, norm1_w, w_in, ret_norm_w, q_norm_w, k_norm_w, rel_bias,
           w_out, norm2_w, w_ff1, w_ff2):
    B, L, D = x.shape
    assert L % ROW_TILE == 0 and w_in.shape[0] == 1
    topk = min(TOPK_MAX, L // 4)
    nb = L // CHUNK + 1
    bf = jnp.bfloat16
    f32 = jnp.float32

    rw = RET_HEADS * RET_DK
    dw = DSA_HEADS * DSA_DH
    sizes = (rw, rw, rw, rw, dw, DSA_DH, DSA_DH, IDX_HEADS * DSA_DH, DSA_DH, IDX_HEADS)
    offs = np.concatenate([[0], np.cumsum(sizes)])
    col = lambda i: w_in[0][:, int(offs[i]):int(offs[i + 1])]
    wr = jnp.concatenate([col(0), col(1), col(2), col(3)], axis=1).astype(bf)
    wq = col(4).reshape(D, DSA_HEADS, DSA_DH)
    wiq = col(7).reshape(D, IDX_HEADS, DSA_DH)
    wqi = jnp.concatenate([wq, wiq], axis=2).reshape(D, DSA_HEADS * 128).astype(bf)
    wk = jnp.concatenate([col(5), col(8)], axis=1).astype(bf)
    wv = jnp.concatenate([col(6), jnp.zeros((D, 128 - DSA_DH), f32)], axis=1).astype(bf)
    ww = jnp.concatenate([col(9), jnp.zeros((D, 128 - IDX_HEADS), f32)], axis=1).astype(bf)

    idx_scale = jnp.full((DSA_DH,), DSA_DH ** -0.5, f32)
    qscale = jnp.concatenate([q_norm_w[0].astype(f32) * (DSA_DH ** -0.5 * LOG2E), idx_scale])[None]
    kscale = jnp.concatenate([k_norm_w[0].astype(f32), jnp.ones((DSA_DH,), f32)])[None]
    meta_pad = jnp.concatenate([jnp.zeros((PAD, D), x.dtype), meta_tokens.astype(x.dtype)], axis=0)

    cos2, sin2 = _rope_tables(nb)
    bias_tbl = _bias_tables(rel_bias)
    tri = jnp.asarray(np.tril(np.ones((2 * CHUNK, 2 * CHUNK), np.float32), k=-1), dtype=bf)

    r_all, qit, k1, k2, vt, iwt = _inproj(x, meta_pad, norm1_w[0][None].astype(f32),
                                          wr, wqi, wk, wv, ww, qscale, kscale)
    y_ret = _retention(r_all, nb, cos2, sin2, ret_norm_w[0][None].astype(f32))
    y_dsa = _sparse_attention(qit, iwt, k1, k2, vt, bias_tbl, tri, topk, nb)

    wo = w_out[0].astype(bf)
    out = _out_mlp(x.reshape(B * L, D), y_ret.reshape(B * L, rw), y_dsa.reshape(B * L, dw),
                   wo[:rw], wo[rw:], norm2_w[0][None].astype(f32),
                   w_ff1[0].astype(bf), w_ff2[0].astype(bf))
    return out.reshape(B, L, D)
```

```python
import functools
import math

import numpy as np
import jax
import jax.numpy as jnp
from jax import lax
from jax.experimental import pallas as pl
from jax.experimental.pallas import tpu as pltpu

N_META = 16
CHUNK = 128
RET_HEADS = 4
RET_DK = 128
DSA_HEADS = 8
DSA_DH = 64
IDX_HEADS = 8
TOPK_MAX = 256
N_BUCKETS = 32
MAX_DISTANCE = 128
ROPE_BASE = 10000.0
EPS = 1e-6
NEG = -1e30
_NEG_BITS = int(np.array(NEG, np.float32).view(np.uint32))
NEG_BF16 = float(np.array((_NEG_BITS + 0x7FFF + ((_NEG_BITS >> 16) & 1)) & 0xFFFF0000,
                          np.uint32).view(np.float32))
PAD = CHUNK - N_META
HALF = 64
INT_MIN = -(2 ** 31)
LOG2E = math.log2(math.e)

FFN_CHUNK = 1024
ROW_TILE = 256
VMEM_LIMIT = 56 * 1024 * 1024


def _dot(a, b):
    return jnp.dot(a, b, preferred_element_type=jnp.float32)


def _dot_nt(a, b):
    return lax.dot_general(a, b, (((1,), (1,)), ((), ())), preferred_element_type=jnp.float32)


def _dot_tn(a, b):
    return lax.dot_general(a, b, (((0,), (0,)), ((), ())), preferred_element_type=jnp.float32)


def _bucket_ranges():
    max_exact = N_BUCKETS // 2
    d = np.arange(0, 2 * CHUNK)
    large = max_exact + (np.log(np.maximum(d, 1) / max_exact) / math.log(MAX_DISTANCE / max_exact)
                         * (N_BUCKETS - max_exact)).astype(np.int64)
    bucket = np.where(d < max_exact, d, np.minimum(large, N_BUCKETS - 1))
    out = []
    for b in range(N_BUCKETS - 1):
        idx = np.nonzero(bucket == b)[0]
        out.append((int(idx.min()), int(idx.max())))
    return out


def _rope_kernel(cos_ref, sin_ref):
    n = pl.program_id(0)
    row = lax.broadcasted_iota(jnp.int32, (CHUNK, 128), 0)
    lane = lax.broadcasted_iota(jnp.int32, (CHUNK, 128), 1)
    pos = (n * CHUNK + row - PAD).astype(jnp.float32)
    frac = (lane % HALF).astype(jnp.float32) / HALF
    inv = jnp.exp(-frac * math.log(ROPE_BASE))
    ang = pos * inv
    cos_ref[...] = jnp.cos(ang)
    s = jnp.sin(ang)
    sin_ref[...] = jnp.where(lane < HALF, -s, s)


def _rope_tables(nb):
    return pl.pallas_call(
        _rope_kernel,
        grid=(nb,),
        out_specs=[pl.BlockSpec((CHUNK, 128), lambda n: (n, 0))] * 2,
        out_shape=[jax.ShapeDtypeStruct((nb * CHUNK, 128), jnp.float32)] * 2,
        name="rope_tables",
    )()


def _bias_kernel(rb_ref, out_ref):
    row = lax.broadcasted_iota(jnp.int32, (CHUNK, CHUNK), 0)
    col = lax.broadcasted_iota(jnp.int32, (CHUNK, CHUNK), 1)
    ranges = _bucket_ranges()
    out_ref[2] = jnp.zeros(out_ref.shape[1:], jnp.float32)
    for h in range(DSA_HEADS):
        far = rb_ref[N_BUCKETS - 1, h]
        for t in range(2):
            dist = col - row + t * CHUNK
            tile = jnp.zeros((CHUNK, CHUNK), jnp.float32)
            for b, (lo, hi) in enumerate(ranges):
                tile = jnp.where((dist >= lo) & (dist <= hi), (rb_ref[b, h] - far) * LOG2E, tile)
            out_ref[t, :, h * 128:(h + 1) * 128] = tile


def _bias_tables(rel_bias):
    return pl.pallas_call(
        _bias_kernel,
        in_specs=[pl.BlockSpec(memory_space=pltpu.SMEM)],
        out_specs=pl.BlockSpec(memory_space=pltpu.VMEM),
        out_shape=jax.ShapeDtypeStruct((3, CHUNK, DSA_HEADS * 128), jnp.float32),
        name="bias_tables",
    )(rel_bias.astype(jnp.float32))


def _inproj_kernel(x_ref, meta_ref, n1_ref, wr_ref, wqi_ref, wk_ref, wv_ref, ww_ref,
                   qs_ref, ks_ref,
                   r_ref, qit_ref, k1_ref, k2_ref, vt_ref, iwt_ref):
    n = pl.program_id(1)
    last = pl.num_programs(1) - 1
    src = jnp.where(n == 0, meta_ref[...], x_ref[0])
    src = jnp.where(n == last, 0.0, src)
    ms = jnp.mean(src * src, axis=-1, keepdims=True)
    u = (src * lax.rsqrt(ms + EPS) * n1_ref[...]).astype(jnp.bfloat16)

    r_ref[0] = _dot(u, wr_ref[...])

    lane = lax.broadcasted_iota(jnp.int32, (CHUNK, 128), 1)
    lo = lane < HALF

    def head_rms(t):
        ssq = jnp.sum(jnp.where(lo, t * t, 0.0), axis=-1, keepdims=True)
        return lax.rsqrt(ssq / DSA_DH + EPS)

    pq = _dot(u, wqi_ref[...])
    for h in range(DSA_HEADS):
        t = pq[:, h * 128:(h + 1) * 128]
        scale = jnp.where(lo, head_rms(t), 1.0) * qs_ref[...]
        qit_ref[0, 0, :, h * 128:(h + 1) * 128] = (t * scale).T.astype(jnp.bfloat16)

    pk = _dot(u, wk_ref[...])
    kn = pk * head_rms(pk) * ks_ref[...]
    k1_ref[0] = jnp.where(lo, kn, 0.0).astype(jnp.bfloat16)
    k2_ref[0] = jnp.where(lo, 0.0, pk).astype(jnp.bfloat16)

    pv = _dot(u, wv_ref[...])
    vx = jnp.where(lo, pv, jnp.where(lane == HALF, 1.0, 0.0))
    vt_ref[0] = vx.T.astype(jnp.bfloat16)

    iw = _dot(u, ww_ref[...]) * (IDX_HEADS ** -0.5)
    iwt_ref[0, 0] = iw.T[0:IDX_HEADS, :]


def _inproj(x, meta_pad, n1, wr, wqi, wk, wv, ww, qscale, kscale):
    B, L, D = x.shape
    nb = L // CHUNK + 1
    nbp = nb + (nb % 2)
    tp = nbp * CHUNK
    const = lambda shape: pl.BlockSpec(shape, lambda b, n: (0,) * len(shape))
    blk = lambda w: pl.BlockSpec((1, CHUNK, w), lambda b, n: (b, n, 0))
    return pl.pallas_call(
        _inproj_kernel,
        grid=(B, nbp),
        in_specs=[
            pl.BlockSpec((1, CHUNK, D), lambda b, n: (b, jnp.clip(n - 1, 0, nb - 2), 0)),
            const(meta_pad.shape), const(n1.shape), const(wr.shape), const(wqi.shape),
            const(wk.shape), const(wv.shape), const(ww.shape), const(qscale.shape),
            const(kscale.shape),
        ],
        out_specs=[
            blk(wr.shape[1]),
            pl.BlockSpec((1, 1, 128, DSA_HEADS * 128), lambda b, n: (b, n, 0, 0)),
            blk(128), blk(128),
            pl.BlockSpec((1, 128, CHUNK), lambda b, n: (b, 0, n)),
            pl.BlockSpec((1, 1, IDX_HEADS, CHUNK), lambda b, n: (b, n, 0, 0)),
        ],
        out_shape=[
            jax.ShapeDtypeStruct((B, tp, wr.shape[1]), jnp.float32),
            jax.ShapeDtypeStruct((B, nbp, 128, DSA_HEADS * 128), jnp.bfloat16),
            jax.ShapeDtypeStruct((B, tp, 128), jnp.bfloat16),
            jax.ShapeDtypeStruct((B, tp, 128), jnp.bfloat16),
            jax.ShapeDtypeStruct((B, 128, tp), jnp.bfloat16),
            jax.ShapeDtypeStruct((B, nbp, IDX_HEADS, CHUNK), jnp.float32),
        ],
        compiler_params=pltpu.CompilerParams(
            dimension_semantics=("arbitrary", "arbitrary"), vmem_limit_bytes=VMEM_LIMIT),
        name="in_projection",
    )(x, meta_pad, n1, wr, wqi, wk, wv, ww, qscale, kscale)


def _retention_kernel(r_ref, cos_ref, sin_ref, gain_ref, y_ref, state_ref):
    n = pl.program_id(1)

    @pl.when(n == 0)
    def _():
        state_ref[...] = jnp.zeros_like(state_ref)

    cos2 = cos_ref[...]
    sin2 = sin_ref[...]
    row = lax.broadcasted_iota(jnp.int32, (CHUNK, CHUNK), 0)
    col = lax.broadcasted_iota(jnp.int32, (CHUNK, CHUNK), 1)
    rowf = row.astype(jnp.float32)
    diff = (row - col).astype(jnp.float32)
    w = RET_HEADS * RET_DK
    bf = jnp.bfloat16

    for h in range(RET_HEADS):
        log_gamma = math.log(1.0 - 2.0 ** (-5.0 - h))
        sl = slice(h * 128, (h + 1) * 128)
        q = r_ref[0, :, sl]
        k = r_ref[0, :, w + h * 128: w + (h + 1) * 128]
        v = r_ref[0, :, 2 * w + h * 128: 2 * w + (h + 1) * 128].astype(bf)
        g = r_ref[0, :, 3 * w + h * 128: 3 * w + (h + 1) * 128]

        qr = q * cos2 + pltpu.roll(q, HALF, 1) * sin2
        kr = (k * cos2 + pltpu.roll(k, HALF, 1) * sin2) * (RET_DK ** -0.5)

        decay = jnp.where(diff >= 0, jnp.exp(log_gamma * jnp.maximum(diff, 0.0)), 0.0)
        scores = _dot_nt(qr.astype(bf), kr.astype(bf)) * decay
        o = _dot(scores.astype(bf), v)

        state = state_ref[h]
        xi = jnp.exp(log_gamma * (rowf + 1.0))
        o = o + _dot((qr * xi).astype(bf), state.astype(bf))

        zeta = jnp.exp(log_gamma * (CHUNK - 1.0 - rowf))
        kv = _dot_tn((kr * zeta).astype(bf), v)
        state_ref[h] = state * math.exp(log_gamma * CHUNK) + kv

        ms = jnp.mean(o * o, axis=-1, keepdims=True)
        on = o * lax.rsqrt(ms + EPS) * gain_ref[:, sl]
        gate = g * (1.0 / (1.0 + jnp.exp(-g)))
        y_ref[0, :, sl] = (gate * on).astype(y_ref.dtype)


def _retention(r_all, nb, cos2, sin2, gain):
    B = r_all.shape[0]
    tp = nb * CHUNK
    w = RET_HEADS * RET_DK
    return pl.pallas_call(
        _retention_kernel,
        grid=(B, nb),
        in_specs=[
            pl.BlockSpec((1, CHUNK, 4 * w), lambda b, n: (b, n, 0)),
            pl.BlockSpec((CHUNK, 128), lambda b, n: (n, 0)),
            pl.BlockSpec((CHUNK, 128), lambda b, n: (n, 0)),
            pl.BlockSpec((1, w), lambda b, n: (0, 0)),
        ],
        out_specs=pl.BlockSpec((1, CHUNK, w), lambda b, n: (b, jnp.maximum(n - 1, 0), 0)),
        out_shape=jax.ShapeDtypeStruct((B, tp - CHUNK, w), jnp.bfloat16),
        scratch_shapes=[pltpu.VMEM((RET_HEADS, RET_DK, RET_DK), jnp.float32)],
        compiler_params=pltpu.CompilerParams(
            dimension_semantics=("arbitrary", "arbitrary"), vmem_limit_bytes=VMEM_LIMIT),
        name="retention",
    )(r_all, cos2, sin2, gain)


def _bit_planes(tile):
    a = [tile[r * 8:(r + 1) * 8, :] for r in range(32)]
    j, m = 16, 0x0000FFFF
    while j:
        for k in range(32):
            if (k & j) == 0:
                t = (a[k] ^ lax.shift_right_logical(a[k + j], j)) & m
                a[k] = a[k] ^ t
                a[k + j] = a[k + j] ^ (t << j)
        j >>= 1
        if j:
            m = m ^ ((m << j) & 0xFFFFFFFF)
    return a


def _sublane_total(x):
    x = x + pltpu.roll(x, 4, 0)
    x = x + pltpu.roll(x, 2, 0)
    return x + pltpu.roll(x, 1, 0)


def _dsa_kernel(qit_ref, iwt_ref, k1_ref, k2_ref, vt_ref, bias_ref, tri_ref, y_ref,
                key_ref, plane_ref, m_ref, acc_ref, sa_ref, sb_ref, *, topk):
    n = pl.program_id(1) + 1
    npair = (n + 2) // 2
    bf = jnp.bfloat16
    H = DSA_HEADS
    KP = 2 * CHUNK
    G = plane_ref.shape[1]

    @pl.when((pl.program_id(0) == 0) & (pl.program_id(1) == 0))
    def _():
        plane_ref[...] = jnp.zeros_like(plane_ref)

    row = lax.broadcasted_iota(jnp.int32, (KP, CHUNK), 0)
    col = lax.broadcasted_iota(jnp.int32, (KP, CHUNK), 1)
    q_idx = n * CHUNK + col
    hs = lambda h: slice(h * 128, (h + 1) * 128)
    hp = lambda t: slice(t * 256, (t + 1) * 256)

    iw_rows = [jnp.broadcast_to(iwt_ref[0, 0, h:h + 1, :], (KP, CHUNK)) for h in range(IDX_HEADS)]

    def key_rows(g):
        last = k1_ref.shape[1] // KP - 1
        return pl.ds(pl.multiple_of(jnp.minimum(g, last) * KP, KP), KP)

    def score_pair(g):
        kblk = k2_ref[0, key_rows(g), :]
        sc = None
        for t in range(H // 2):
            s = _dot(kblk, qit_ref[0, 0, :, hp(t)])
            for hh in range(2):
                term = iw_rows[2 * t + hh] * jnp.maximum(s[:, hs(hh)], 0.0)
                sc = term if sc is None else sc + term
        k_idx = g * KP + row
        valid = (k_idx <= q_idx) & (k_idx >= PAD)
        sc = jnp.where(sc == 0.0, 0.0, sc)
        sc = jnp.where(valid, sc, NEG)
        bits = lax.bitcast_convert_type(sc, jnp.int32)
        key = jnp.where(bits < 0, bits ^ jnp.int32(0x7FFFFFFF), bits)
        key_ref[g] = key
        planes = _bit_planes(key ^ jnp.int32(INT_MIN))
        for p in range(32):
            plane_ref[p, g] = planes[p]

    def score_two(i, _):
        score_pair(2 * i)
        score_pair(2 * i + 1)
        return 0

    niter = (npair + 1) // 2
    lax.fori_loop(0, niter, score_two, 0)

    neg_bits = int(np.float32(NEG).view(np.int32))
    neg_key = neg_bits ^ 0x7FFFFFFF
    if neg_key >= 2 ** 31:
        neg_key -= 2 ** 32
    g_idx = lax.broadcasted_iota(jnp.int32, (G, 8, CHUNK), 0)
    alive0 = jnp.where(g_idx < npair, jnp.int32(-1), jnp.int32(0))

    def ones_in(words):
        part = lax.population_count(words).astype(jnp.float32).sum(axis=0)
        return _sublane_total(part)

    def radix(p, carry):
        alive, above, thr, cnt = carry
        take1 = (above + cnt) >= topk
        above = jnp.where(take1, above, above + cnt)
        thr = jnp.where(take1, thr | lax.shift_right_logical(jnp.int32(INT_MIN), p), thr)
        drop = jnp.where(take1, jnp.int32(0), jnp.int32(-1))
        alive = alive & (plane_ref[p] ^ drop[None])
        cnt = ones_in(alive & plane_ref[jnp.minimum(p + 1, 31)])
        return alive, above, thr, cnt

    zero8 = jnp.zeros((8, CHUNK), jnp.float32)
    alive, n_gt, thr_u, _ = lax.fori_loop(
        0, 32, radix,
        (alive0, zero8, jnp.zeros((8, CHUNK), jnp.int32), ones_in(alive0 & plane_ref[0])))
    thr = thr_u ^ jnp.int32(INT_MIN)
    n_eq = ones_in(alive)
    masked_thr = thr == neg_key
    n_tie = jnp.where(masked_thr, 0.0, topk - n_gt)
    has_ties = jnp.max(jnp.where((n_eq > n_tie) & ~masked_thr, 1.0, 0.0)) > 0.5
    rep = lambda x: jnp.tile(x, (KP // 8, 1))
    thr_b = rep(thr)
    thr_sel_b = rep(jnp.where(masked_thr, jnp.int32(neg_key + 1), thr))
    n_tie_b = rep(n_tie)

    m_ref[...] = jnp.full_like(m_ref, NEG_BF16)
    acc_ref[...] = jnp.zeros_like(acc_ref)
    va = acc_ref.shape[0]

    def logits(g, dst_ref, near):
        s = _dot(k1_ref[0, key_rows(g), :], qit_ref[0, 0])
        if near:
            ta = jnp.clip(n - 2 * g, 0, 2)
            tb = jnp.clip(n - 2 * g - 1, 0, 2)
            s = s + jnp.concatenate([bias_ref[ta], bias_ref[tb]], axis=0)
        dst_ref[...] = s.astype(bf)

    def attend(g, s_ref, ties_before, ties):
        key = key_ref[g]
        if ties:
            eq = key == thr_b
            eqf = jnp.where(eq, 1.0, 0.0)
            before = ties_before + _dot(tri_ref[...], eqf.astype(bf))
            sel = (key > thr_b) | (eq & (before < n_tie_b))
            ties_before = ties_before + jnp.sum(eqf, axis=0, keepdims=True)
        else:
            sel = key >= thr_sel_b
        mask = jnp.where(sel, 0.0, NEG).astype(bf)
        ps, alphas = [], []
        for h in range(H):
            s = s_ref[:, hs(h)] + mask
            m_prev = m_ref[h:h + 1, :]
            m_new = jnp.maximum(m_prev, jnp.max(s, axis=0, keepdims=True).astype(jnp.float32))
            m_ref[h:h + 1, :] = m_new
            alphas.append(jnp.exp2(m_prev - m_new))
            ps.append(jnp.exp2(s - m_new.astype(bf)))
        pv = _dot(vt_ref[0, 0:va, key_rows(g)], jnp.concatenate(ps, axis=1))
        acc_ref[...] = acc_ref[...] * jnp.concatenate(alphas, axis=1) + pv
        return ties_before

    def attend_two(i, ties_before, near, ties):
        logits(2 * i + 1, sb_ref, near)
        ties_before = attend(2 * i, sa_ref, ties_before, ties)
        logits(2 * i + 2, sa_ref, True)
        return attend(2 * i + 1, sb_ref, ties_before, ties)

    nfar_iter = ((n - 1) // 2) // 2
    logits(0, sa_ref, True)

    def run(ties):
        def go():
            c = lax.fori_loop(0, nfar_iter, functools.partial(attend_two, near=False, ties=ties),
                              jnp.zeros((KP, CHUNK), jnp.float32))
            lax.fori_loop(nfar_iter, niter, functools.partial(attend_two, near=True, ties=ties), c)
        return go

    pl.when(has_ties)(run(True))
    pl.when(jnp.logical_not(has_ties))(run(False))

    lo = lax.broadcasted_iota(jnp.int32, (CHUNK, CHUNK), 1) < HALF
    for t in range(H // 2):
        tiles = []
        for h in (2 * t, 2 * t + 1):
            a = acc_ref[:, hs(h)] * (1.0 / acc_ref[HALF:HALF + 1, hs(h)])
            a = jnp.concatenate([a, jnp.zeros((CHUNK - va, CHUNK), jnp.float32)], axis=0)
            tiles.append(a.T)
        y_ref[0, :, hs(t)] = jnp.where(lo, tiles[0], pltpu.roll(tiles[1], HALF, 1)).astype(y_ref.dtype)


def _sparse_attention(qit, iwt, k1, k2, vt, bias_tbl, tri, topk, nb):
    B, nbp, _, hw = qit.shape
    tp = nbp * CHUNK
    nstep = nbp // 2
    nstep += nstep % 2
    kspec = pl.BlockSpec((1, tp, 128), lambda b, n: (b, 0, 0))
    acc_rows = 80
    return pl.pallas_call(
        functools.partial(_dsa_kernel, topk=topk),
        grid=(B, nb - 1),
        in_specs=[
            pl.BlockSpec((1, 1, 128, hw), lambda b, n: (b, n + 1, 0, 0)),
            pl.BlockSpec((1, 1, IDX_HEADS, CHUNK), lambda b, n: (b, n + 1, 0, 0)),
            kspec, kspec,
            pl.BlockSpec((1, 128, tp), lambda b, n: (b, 0, 0)),
            pl.BlockSpec(bias_tbl.shape, lambda b, n: (0, 0, 0)),
            pl.BlockSpec(tri.shape, lambda b, n: (0, 0)),
        ],
        out_specs=pl.BlockSpec((1, CHUNK, DSA_HEADS * DSA_DH), lambda b, n: (b, n, 0)),
        out_shape=jax.ShapeDtypeStruct((B, (nb - 1) * CHUNK, DSA_HEADS * DSA_DH), jnp.bfloat16),
        scratch_shapes=[
            pltpu.VMEM((nstep, 2 * CHUNK, CHUNK), jnp.int32),
            pltpu.VMEM((32, nstep, 8, CHUNK), jnp.int32),
            pltpu.VMEM((DSA_HEADS, CHUNK), jnp.float32),
            pltpu.VMEM((acc_rows, hw), jnp.float32),
            pltpu.VMEM((2 * CHUNK, hw), jnp.bfloat16),
            pltpu.VMEM((2 * CHUNK, hw), jnp.bfloat16),
        ],
        compiler_params=pltpu.CompilerParams(
            dimension_semantics=("arbitrary", "arbitrary"), vmem_limit_bytes=VMEM_LIMIT),
        name="sparse_attention",
    )(qit, iwt, k1, k2, vt, bias_tbl, tri)


def _mlp_kernel(x_ref, yr_ref, yd_ref, wor_ref, wod_ref, n2_ref, w1_ref, w2_ref, o_ref):
    h1 = x_ref[...] + _dot(yr_ref[...], wor_ref[...]) + _dot(yd_ref[...], wod_ref[...])
    ms = jnp.mean(h1 * h1, axis=-1, keepdims=True)
    u = (h1 * lax.rsqrt(ms + EPS) * n2_ref[...]).astype(jnp.bfloat16)
    o_ref[...] = h1
    d_ff = w1_ref.shape[1]
    for c in range(d_ff // FFN_CHUNK):
        sl = slice(c * FFN_CHUNK, (c + 1) * FFN_CHUNK)
        f = jnp.maximum(_dot(u, w1_ref[:, sl]), 0.0)
        o_ref[...] += _dot((f * f).astype(jnp.bfloat16), w2_ref[sl, :])


def _out_mlp(x2, yr, yd, wor, wod, n2, w1, w2):
    rows, D = x2.shape
    const = lambda a: pl.BlockSpec(a.shape, lambda i: (0, 0))
    tile = lambda w: pl.BlockSpec((ROW_TILE, w), lambda i: (i, 0))
    return pl.pallas_call(
        _mlp_kernel,
        grid=(rows // ROW_TILE,),
        in_specs=[tile(D), tile(yr.shape[1]), tile(yd.shape[1]),
                  const(wor), const(wod), const(n2), const(w1), const(w2)],
        out_specs=tile(D),
        out_shape=jax.ShapeDtypeStruct((rows, D), jnp.float32),
        compiler_params=pltpu.CompilerParams(
            dimension_semantics=("arbitrary",), vmem_limit_bytes=VMEM_LIMIT),
        name="out_mlp",
    )(x2, yr, yd, wor, wod, n2, w1, w2)


def kernel(x, meta_tokens, norm1_w, w_in, ret_norm_w, q_norm_w, k_norm_w, rel_bias,
           w_out, norm2_w, w_ff1, w_ff2):
    B, L, D = x.shape
    assert L % ROW_TILE == 0 and w_in.shape[0] == 1
    topk = min(TOPK_MAX, L // 4)
    nb = L // CHUNK + 1
    bf = jnp.bfloat16
    f32 = jnp.float32

    rw = RET_HEADS * RET_DK
    dw = DSA_HEADS * DSA_DH
    sizes = (rw, rw, rw, rw, dw, DSA_DH, DSA_DH, IDX_HEADS * DSA_DH, DSA_DH, IDX_HEADS)
    offs = np.concatenate([[0], np.cumsum(sizes)])
    col = lambda i: w_in[0][:, int(offs[i]):int(offs[i + 1])]
    wr = jnp.concatenate([col(0), col(1), col(2), col(3)], axis=1).astype(bf)
    wq = col(4).reshape(D, DSA_HEADS, DSA_DH)
    wiq = col(7).reshape(D, IDX_HEADS, DSA_DH)
    wqi = jnp.concatenate([wq, wiq], axis=2).reshape(D, DSA_HEADS * 128).astype(bf)
    wk = jnp.concatenate([col(5), col(8)], axis=1).astype(bf)
    wv = jnp.concatenate([col(6), jnp.zeros((D, 128 - DSA_DH), f32)], axis=1).astype(bf)
    ww = jnp.concatenate([col(9), jnp.zeros((D, 128 - IDX_HEADS), f32)], axis=1).astype(bf)

    idx_scale = jnp.full((DSA_DH,), DSA_DH ** -0.5, f32)
    qscale = jnp.concatenate([q_norm_w[0].astype(f32) * (DSA_DH ** -0.5 * LOG2E), idx_scale])[None]
    kscale = jnp.concatenate([k_norm_w[0].astype(f32), jnp.ones((DSA_DH,), f32)])[None]
    meta_pad = jnp.concatenate([jnp.zeros((PAD, D), x.dtype), meta_tokens.astype(x.dtype)], axis=0)

    cos2, sin2 = _rope_tables(nb)
    bias_tbl = _bias_tables(rel_bias)
    tri = jnp.asarray(np.tril(np.ones((2 * CHUNK, 2 * CHUNK), np.float32), k=-1), dtype=bf)

    r_all, qit, k1, k2, vt, iwt = _inproj(x, meta_pad, norm1_w[0][None].astype(f32),
                                          wr, wqi, wk, wv, ww, qscale, kscale)
    y_ret = _retention(r_all, nb, cos2, sin2, ret_norm_w[0][None].astype(f32))
    y_dsa = _sparse_attention(qit, iwt, k1, k2, vt, bias_tbl, tri, topk, nb)

    wo = w_out[0].astype(bf)
    out = _out_mlp(x.reshape(B * L, D), y_ret.reshape(B * L, rw), y_dsa.reshape(B * L, dw),
                   wo[:rw], wo[rw:], norm2_w[0][None].astype(f32),
                   w_ff1[0].astype(bf), w_ff2[0].astype(bf))
    return out.reshape(B, L, D)
```

```python
import functools
import math

import numpy as np
import jax
import jax.numpy as jnp
from jax import lax
from jax.experimental import pallas as pl
from jax.experimental.pallas import tpu as pltpu

N_META = 16
CHUNK = 128
RET_HEADS = 4
RET_DK = 128
DSA_HEADS = 8
DSA_DH = 64
IDX_HEADS = 8
TOPK_MAX = 256
N_BUCKETS = 32
MAX_DISTANCE = 128
ROPE_BASE = 10000.0
EPS = 1e-6
NEG = -1e30
_NEG_BITS = int(np.array(NEG, np.float32).view(np.uint32))
NEG_BF16 = float(np.array((_NEG_BITS + 0x7FFF + ((_NEG_BITS >> 16) & 1)) & 0xFFFF0000,
                          np.uint32).view(np.float32))
PAD = CHUNK - N_META
HALF = 64
INT_MIN = -(2 ** 31)
LOG2E = math.log2(math.e)

FFN_CHUNK = 1024
ROW_TILE = 256
VMEM_LIMIT = 56 * 1024 * 1024


def _dot(a, b):
    return jnp.dot(a, b, preferred_element_type=jnp.float32)


def _dot_nt(a, b):
    return lax.dot_general(a, b, (((1,), (1,)), ((), ())), preferred_element_type=jnp.float32)


def _dot_tn(a, b):
    return lax.dot_general(a, b, (((0,), (0,)), ((), ())), preferred_element_type=jnp.float32)


def _bucket_ranges():
    max_exact = N_BUCKETS // 2
    d = np.arange(0, 2 * CHUNK)
    large = max_exact + (np.log(np.maximum(d, 1) / max_exact) / math.log(MAX_DISTANCE / max_exact)
                         * (N_BUCKETS - max_exact)).astype(np.int64)
    bucket = np.where(d < max_exact, d, np.minimum(large, N_BUCKETS - 1))
    out = []
    for b in range(N_BUCKETS - 1):
        idx = np.nonzero(bucket == b)[0]
        out.append((int(idx.min()), int(idx.max())))
    return out


def _rope_kernel(cos_ref, sin_ref):
    n = pl.program_id(0)
    row = lax.broadcasted_iota(jnp.int32, (CHUNK, 128), 0)
    lane = lax.broadcasted_iota(jnp.int32, (CHUNK, 128), 1)
    pos = (n * CHUNK + row - PAD).astype(jnp.float32)
    frac = (lane % HALF).astype(jnp.float32) / HALF
    inv = jnp.exp(-frac * math.log(ROPE_BASE))
    ang = pos * inv
    cos_ref[...] = jnp.cos(ang)
    s = jnp.sin(ang)
    sin_ref[...] = jnp.where(lane < HALF, -s, s)


def _rope_tables(nb):
    return pl.pallas_call(
        _rope_kernel,
        grid=(nb,),
        out_specs=[pl.BlockSpec((CHUNK, 128), lambda n: (n, 0))] * 2,
        out_shape=[jax.ShapeDtypeStruct((nb * CHUNK, 128), jnp.float32)] * 2,
        name="rope_tables",
    )()


def _bias_kernel(rb_ref, out_ref):
    row = lax.broadcasted_iota(jnp.int32, (CHUNK, CHUNK), 0)
    col = lax.broadcasted_iota(jnp.int32, (CHUNK, CHUNK), 1)
    ranges = _bucket_ranges()
    out_ref[2] = jnp.zeros(out_ref.shape[1:], jnp.float32)
    for h in range(DSA_HEADS):
        far = rb_ref[N_BUCKETS - 1, h]
        for t in range(2):
            dist = col - row + t * CHUNK
            tile = jnp.zeros((CHUNK, CHUNK), jnp.float32)
            for b, (lo, hi) in enumerate(ranges):
                tile = jnp.where((dist >= lo) & (dist <= hi), (rb_ref[b, h] - far) * LOG2E, tile)
            out_ref[t, :, h * 128:(h + 1) * 128] = tile


def _bias_tables(rel_bias):
    return pl.pallas_call(
        _bias_kernel,
        in_specs=[pl.BlockSpec(memory_space=pltpu.SMEM)],
        out_specs=pl.BlockSpec(memory_space=pltpu.VMEM),
        out_shape=jax.ShapeDtypeStruct((3, CHUNK, DSA_HEADS * 128), jnp.float32),
        name="bias_tables",
    )(rel_bias.astype(jnp.float32))


def _inproj_kernel(xa_ref, xb_ref, meta_ref, n1_ref, wr_ref, wqi_ref, wk_ref, wv_ref, ww_ref,
                   qs_ref, ks_ref,
                   r_ref, qit_ref, k1_ref, k2_ref, vt_ref, iwt_ref, *, pad_last):
    s = pl.program_id(1)
    top = jnp.where(s == 0, meta_ref[...], xa_ref[0])
    bot = xb_ref[0]
    if pad_last:
        bot = jnp.where(s == pl.num_programs(1) - 1, 0.0, bot)
    src = jnp.concatenate([top, bot], axis=0)
    ms = jnp.mean(src * src, axis=-1, keepdims=True)
    u = (src * lax.rsqrt(ms + EPS) * n1_ref[...]).astype(jnp.bfloat16)

    r_ref[0] = _dot(u, wr_ref[...])

    lane = lax.broadcasted_iota(jnp.int32, (2 * CHUNK, 128), 1)
    lo = lane < HALF

    def head_rms(t):
        ssq = jnp.sum(jnp.where(lo, t * t, 0.0), axis=-1, keepdims=True)
        return lax.rsqrt(ssq / DSA_DH + EPS)

    pq = _dot(u, wqi_ref[...])
    for h in range(DSA_HEADS):
        t = pq[:, h * 128:(h + 1) * 128]
        t = t * (jnp.where(lo, head_rms(t), 1.0) * qs_ref[...])
        for j in range(2):
            qit_ref[0, j, :, h * 128:(h + 1) * 128] = (
                t[j * CHUNK:(j + 1) * CHUNK].T.astype(jnp.bfloat16))

    pk = _dot(u, wk_ref[...])
    kn = pk * head_rms(pk) * ks_ref[...]
    k1_ref[0] = jnp.where(lo, kn, 0.0).astype(jnp.bfloat16)
    k2_ref[0] = jnp.where(lo, 0.0, pk).astype(jnp.bfloat16)

    pv = _dot(u, wv_ref[...])
    vx = jnp.where(lo, pv, jnp.where(lane == HALF, 1.0, 0.0))
    iw = _dot(u, ww_ref[...]) * (IDX_HEADS ** -0.5)
    for j in range(2):
        rows = slice(j * CHUNK, (j + 1) * CHUNK)
        vt_ref[0, :, rows] = vx[rows].T.astype(jnp.bfloat16)
        iwt_ref[0, j] = iw[rows].T[0:IDX_HEADS, :]


def _inproj(x, meta_pad, n1, wr, wqi, wk, wv, ww, qscale, kscale):
    B, L, D = x.shape
    nx = L // CHUNK
    nbp = nx + 1 + ((nx + 1) % 2)
    tp = nbp * CHUNK
    const = lambda shape: pl.BlockSpec(shape, lambda b, s: (0,) * len(shape))
    blk = lambda w: pl.BlockSpec((1, 2 * CHUNK, w), lambda b, s: (b, s, 0))
    return pl.pallas_call(
        functools.partial(_inproj_kernel, pad_last=bool((nx + 1) % 2)),
        grid=(B, nbp // 2),
        in_specs=[
            pl.BlockSpec((1, CHUNK, D), lambda b, s: (b, jnp.maximum(2 * s - 1, 0), 0)),
            pl.BlockSpec((1, CHUNK, D), lambda b, s: (b, jnp.minimum(2 * s, nx - 1), 0)),
            const(meta_pad.shape), const(n1.shape), const(wr.shape), const(wqi.shape),
            const(wk.shape), const(wv.shape), const(ww.shape), const(qscale.shape),
            const(kscale.shape),
        ],
        out_specs=[
            blk(wr.shape[1]),
            pl.BlockSpec((1, 2, 128, DSA_HEADS * 128), lambda b, s: (b, s, 0, 0)),
            blk(128), blk(128),
            pl.BlockSpec((1, 128, 2 * CHUNK), lambda b, s: (b, 0, s)),
            pl.BlockSpec((1, 2, IDX_HEADS, CHUNK), lambda b, s: (b, s, 0, 0)),
        ],
        out_shape=[
            jax.ShapeDtypeStruct((B, tp, wr.shape[1]), jnp.float32),
            jax.ShapeDtypeStruct((B, nbp, 128, DSA_HEADS * 128), jnp.bfloat16),
            jax.ShapeDtypeStruct((B, tp, 128), jnp.bfloat16),
            jax.ShapeDtypeStruct((B, tp, 128), jnp.bfloat16),
            jax.ShapeDtypeStruct((B, 128, tp), jnp.bfloat16),
            jax.ShapeDtypeStruct((B, nbp, IDX_HEADS, CHUNK), jnp.float32),
        ],
        compiler_params=pltpu.CompilerParams(
            dimension_semantics=("arbitrary", "arbitrary"), vmem_limit_bytes=VMEM_LIMIT),
        name="in_projection",
    )(x, x, meta_pad, n1, wr, wqi, wk, wv, ww, qscale, kscale)


def _retention_kernel(ra_ref, rb_ref, cosa_ref, sina_ref, cosb_ref, sinb_ref, gain_ref, y_ref,
                      state_ref, tbl_ref):
    s = pl.program_id(1)
    w = RET_HEADS * RET_DK
    bf = jnp.bfloat16

    @pl.when(s == 0)
    def _():
        state_ref[...] = jnp.zeros_like(state_ref)
        row = lax.broadcasted_iota(jnp.int32, (CHUNK, CHUNK), 0)
        col = lax.broadcasted_iota(jnp.int32, (CHUNK, CHUNK), 1)
        rowf = row.astype(jnp.float32)
        diff = (row - col).astype(jnp.float32)
        for h in range(RET_HEADS):
            log_gamma = math.log(1.0 - 2.0 ** (-5.0 - h))
            tbl_ref[0, h] = jnp.where(diff >= 0, jnp.exp(log_gamma * jnp.maximum(diff, 0.0)), 0.0)
            tbl_ref[1, h] = jnp.exp(log_gamma * (rowf + 1.0))
            tbl_ref[2, h] = jnp.exp(log_gamma * (CHUNK - 1.0 - rowf))

    live = jnp.where(s == 0, 0.0, 1.0)
    chunks = ((ra_ref, cosa_ref, sina_ref, live), (rb_ref, cosb_ref, sinb_ref, None))
    for h in range(RET_HEADS):
        log_gamma = math.log(1.0 - 2.0 ** (-5.0 - h))
        sl = slice(h * 128, (h + 1) * 128)
        state = state_ref[h]
        for j, (r_ref, cos_ref, sin_ref, scale) in enumerate(chunks):
            cos2, sin2 = cos_ref[...], sin_ref[...]
            q = r_ref[0, :, sl]
            k = r_ref[0, :, w + h * 128: w + (h + 1) * 128]
            v = r_ref[0, :, 2 * w + h * 128: 2 * w + (h + 1) * 128]
            g = r_ref[0, :, 3 * w + h * 128: 3 * w + (h + 1) * 128]
            if scale is not None:
                k = k * scale
            v = v.astype(bf)
            qr = q * cos2 + pltpu.roll(q, HALF, 1) * sin2
            kr = (k * cos2 + pltpu.roll(k, HALF, 1) * sin2) * (RET_DK ** -0.5)

            scores = _dot_nt(qr.astype(bf), kr.astype(bf)) * tbl_ref[0, h]
            o = _dot(scores.astype(bf), v)
            o = o + _dot((qr * tbl_ref[1, h]).astype(bf), state.astype(bf))
            kv = _dot_tn((kr * tbl_ref[2, h]).astype(bf), v)
            state = state * math.exp(log_gamma * CHUNK) + kv

            ms = jnp.mean(o * o, axis=-1, keepdims=True)
            on = o * lax.rsqrt(ms + EPS) * gain_ref[:, sl]
            gate = g * (1.0 / (1.0 + jnp.exp(-g)))
            y_ref[0, j * CHUNK:(j + 1) * CHUNK, sl] = (gate * on).astype(y_ref.dtype)
        state_ref[h] = state


def _retention(r_all, nb, cos2, sin2, gain):
    B = r_all.shape[0]
    w = RET_HEADS * RET_DK
    nstep = (nb + 1) // 2
    rspec = lambda f: pl.BlockSpec((1, CHUNK, 4 * w), lambda b, s: (b, f(s), 0))
    tspec = lambda f: pl.BlockSpec((CHUNK, 128), lambda b, s: (f(s), 0))
    first = lambda s: jnp.maximum(2 * s - 1, 0)
    second = lambda s: 2 * s
    return pl.pallas_call(
        _retention_kernel,
        grid=(B, nstep),
        in_specs=[rspec(first), rspec(second), tspec(first), tspec(first), tspec(second),
                  tspec(second), pl.BlockSpec((1, w), lambda b, s: (0, 0))],
        out_specs=pl.BlockSpec((1, 2 * CHUNK, w), lambda b, s: (b, jnp.maximum(s - 1, 0), 0)),
        out_shape=jax.ShapeDtypeStruct((B, (nb - 1) * CHUNK, w), jnp.bfloat16),
        scratch_shapes=[pltpu.VMEM((RET_HEADS, RET_DK, RET_DK), jnp.float32),
                        pltpu.VMEM((3, RET_HEADS, CHUNK, CHUNK), jnp.float32)],
        compiler_params=pltpu.CompilerParams(
            dimension_semantics=("arbitrary", "arbitrary"), vmem_limit_bytes=VMEM_LIMIT),
        name="retention",
    )(r_all, r_all, cos2, sin2, cos2, sin2, gain)


def _bit_planes(tile):
    a = [tile[r * 8:(r + 1) * 8, :] for r in range(32)]
    j, m = 16, 0x0000FFFF
    while j:
        for k in range(32):
            if (k & j) == 0:
                t = (a[k] ^ lax.shift_right_logical(a[k + j], j)) & m
                a[k] = a[k] ^ t
                a[k + j] = a[k + j] ^ (t << j)
        j >>= 1
        if j:
            m = m ^ ((m << j) & 0xFFFFFFFF)
    return a


def _sublane_total(x):
    x = x + pltpu.roll(x, 4, 0)
    x = x + pltpu.roll(x, 2, 0)
    return x + pltpu.roll(x, 1, 0)


def _dsa_kernel(qit_ref, iwt_ref, k1_ref, k2_ref, vt_ref, bias_ref, tri_ref, y_ref,
                key_ref, plane_ref, m_ref, acc_ref, sa_ref, sb_ref, *, topk):
    n = pl.program_id(1) + 1
    npair = (n + 2) // 2
    bf = jnp.bfloat16
    H = DSA_HEADS
    KP = 2 * CHUNK
    G = plane_ref.shape[1]

    @pl.when((pl.program_id(0) == 0) & (pl.program_id(1) == 0))
    def _():
        plane_ref[...] = jnp.zeros_like(plane_ref)

    row = lax.broadcasted_iota(jnp.int32, (KP, CHUNK), 0)
    col = lax.broadcasted_iota(jnp.int32, (KP, CHUNK), 1)
    q_idx = n * CHUNK + col
    hs = lambda h: slice(h * 128, (h + 1) * 128)
    hp = lambda t: slice(t * 256, (t + 1) * 256)

    iw_rows = [jnp.broadcast_to(iwt_ref[0, 0, h:h + 1, :], (KP, CHUNK)) for h in range(IDX_HEADS)]

    def key_rows(g):
        last = k1_ref.shape[1] // KP - 1
        return pl.ds(pl.multiple_of(jnp.minimum(g, last) * KP, KP), KP)

    def score_pair(g):
        kblk = k2_ref[0, key_rows(g), :]
        sc = None
        for t in range(H // 2):
            s = _dot(kblk, qit_ref[0, 0, :, hp(t)])
            for hh in range(2):
                term = iw_rows[2 * t + hh] * jnp.maximum(s[:, hs(hh)], 0.0)
                sc = term if sc is None else sc + term
        k_idx = g * KP + row
        valid = (k_idx <= q_idx) & (k_idx >= PAD)
        sc = jnp.where(sc == 0.0, 0.0, sc)
        sc = jnp.where(valid, sc, NEG)
        bits = lax.bitcast_convert_type(sc, jnp.int32)
        key = jnp.where(bits < 0, bits ^ jnp.int32(0x7FFFFFFF), bits)
        key_ref[g] = key
        planes = _bit_planes(key ^ jnp.int32(INT_MIN))
        for p in range(32):
            plane_ref[p, g] = planes[p]

    def score_two(i, _):
        score_pair(2 * i)
        score_pair(2 * i + 1)
        return 0

    niter = (npair + 1) // 2
    lax.fori_loop(0, niter, score_two, 0)

    neg_bits = int(np.float32(NEG).view(np.int32))
    neg_key = neg_bits ^ 0x7FFFFFFF
    if neg_key >= 2 ** 31:
        neg_key -= 2 ** 32
    g_idx = lax.broadcasted_iota(jnp.int32, (G, 8, CHUNK), 0)
    alive0 = jnp.where(g_idx < npair, jnp.int32(-1), jnp.int32(0))

    def ones_in(words):
        part = lax.population_count(words).astype(jnp.float32).sum(axis=0)
        return _sublane_total(part)

    def radix(p, carry):
        alive, above, thr, cnt = carry
        take1 = (above + cnt) >= topk
        above = jnp.where(take1, above, above + cnt)
        thr = jnp.where(take1, thr | lax.shift_right_logical(jnp.int32(INT_MIN), p), thr)
        drop = jnp.where(take1, jnp.int32(0), jnp.int32(-1))
        alive = alive & (plane_ref[p] ^ drop[None])
        cnt = ones_in(alive & plane_ref[jnp.minimum(p + 1, 31)])
        return alive, above, thr, cnt

    zero8 = jnp.zeros((8, CHUNK), jnp.float32)
    alive, n_gt, thr_u, _ = lax.fori_loop(
        0, 32, radix,
        (alive0, zero8, jnp.zeros((8, CHUNK), jnp.int32), ones_in(alive0 & plane_ref[0])))
    thr = thr_u ^ jnp.int32(INT_MIN)
    n_eq = ones_in(alive)
    masked_thr = thr == neg_key
    n_tie = jnp.where(masked_thr, 0.0, topk - n_gt)
    has_ties = jnp.max(jnp.where((n_eq > n_tie) & ~masked_thr, 1.0, 0.0)) > 0.5
    rep = lambda x: jnp.tile(x, (KP // 8, 1))
    thr_b = rep(thr)
    thr_sel_b = rep(jnp.where(masked_thr, jnp.int32(neg_key + 1), thr))
    n_tie_b = rep(n_tie)

    m_ref[...] = jnp.full_like(m_ref, NEG_BF16)
    acc_ref[...] = jnp.zeros_like(acc_ref)
    va = acc_ref.shape[0]

    def logits(g, dst_ref, near):
        s = _dot(k1_ref[0, key_rows(g), :], qit_ref[0, 0])
        if near:
            ta = jnp.clip(n - 2 * g, 0, 2)
            tb = jnp.clip(n - 2 * g - 1, 0, 2)
            s = s + jnp.concatenate([bias_ref[ta], bias_ref[tb]], axis=0)
        dst_ref[...] = s.astype(bf)

    def attend(g, s_ref, ties_before, ties):
        key = key_ref[g]
        if ties:
            eq = key == thr_b
            eqf = jnp.where(eq, 1.0, 0.0)
            before = ties_before + _dot(tri_ref[...], eqf.astype(bf))
            sel = (key > thr_b) | (eq & (before < n_tie_b))
            ties_before = ties_before + jnp.sum(eqf, axis=0, keepdims=True)
        else:
            sel = key >= thr_sel_b
        mask = jnp.where(sel, 0.0, NEG).astype(bf)
        ps, alphas = [], []
        for h in range(H):
            s = s_ref[:, hs(h)] + mask
            m_prev = m_ref[h:h + 1, :]
            m_new = jnp.maximum(m_prev, jnp.max(s, axis=0, keepdims=True).astype(jnp.float32))
            m_ref[h:h + 1, :] = m_new
            alphas.append(jnp.exp2(m_prev - m_new))
            ps.append(jnp.exp2((s - m_new.astype(bf)).astype(jnp.float32)).astype(bf))
        pv = _dot(vt_ref[0, 0:va, key_rows(g)], jnp.concatenate(ps, axis=1))
        acc_ref[...] = acc_ref[...] * jnp.concatenate(alphas, axis=1) + pv
        return ties_before

    def attend_two(i, ties_before, near, ties):
        logits(2 * i + 1, sb_ref, near)
        ties_before = attend(2 * i, sa_ref, ties_before, ties)
        logits(2 * i + 2, sa_ref, True)
        return attend(2 * i + 1, sb_ref, ties_before, ties)

    nfar_iter = ((n - 1) // 2) // 2
    logits(0, sa_ref, True)

    def run(ties):
        def go():
            c = lax.fori_loop(0, nfar_iter, functools.partial(attend_two, near=False, ties=ties),
                              jnp.zeros((KP, CHUNK), jnp.float32))
            lax.fori_loop(nfar_iter, niter, functools.partial(attend_two, near=True, ties=ties), c)
        return go

    pl.when(has_ties)(run(True))
    pl.when(jnp.logical_not(has_ties))(run(False))

    lo = lax.broadcasted_iota(jnp.int32, (CHUNK, CHUNK), 1) < HALF
    for t in range(H // 2):
        tiles = []
        for h in (2 * t, 2 * t + 1):
            a = acc_ref[:, hs(h)] * (1.0 / acc_ref[HALF:HALF + 1, hs(h)])
            a = jnp.concatenate([a, jnp.zeros((CHUNK - va, CHUNK), jnp.float32)], axis=0)
            tiles.append(a.T)
        y_ref[0, :, hs(t)] = jnp.where(lo, tiles[0], pltpu.roll(tiles[1], HALF, 1)).astype(y_ref.dtype)


def _sparse_attention(qit, iwt, k1, k2, vt, bias_tbl, tri, topk, nb):
    B, nbp, _, hw = qit.shape
    tp = nbp * CHUNK
    nstep = nbp // 2
    nstep += nstep % 2
    kspec = pl.BlockSpec((1, tp, 128), lambda b, n: (b, 0, 0))
    acc_rows = 80
    return pl.pallas_call(
        functools.partial(_dsa_kernel, topk=topk),
        grid=(B, nb - 1),
        in_specs=[
            pl.BlockSpec((1, 1, 128, hw), lambda b, n: (b, n + 1, 0, 0)),
            pl.BlockSpec((1, 1, IDX_HEADS, CHUNK), lambda b, n: (b, n + 1, 0, 0)),
            kspec, kspec,
            pl.BlockSpec((1, 128, tp), lambda b, n: (b, 0, 0)),
            pl.BlockSpec(bias_tbl.shape, lambda b, n: (0, 0, 0)),
            pl.BlockSpec(tri.shape, lambda b, n: (0, 0)),
        ],
        out_specs=pl.BlockSpec((1, CHUNK, DSA_HEADS * DSA_DH), lambda b, n: (b, n, 0)),
        out_shape=jax.ShapeDtypeStruct((B, (nb - 1) * CHUNK, DSA_HEADS * DSA_DH), jnp.bfloat16),
        scratch_shapes=[
            pltpu.VMEM((nstep, 2 * CHUNK, CHUNK), jnp.int32),
            pltpu.VMEM((32, nstep, 8, CHUNK), jnp.int32),
            pltpu.VMEM((DSA_HEADS, CHUNK), jnp.float32),
            pltpu.VMEM((acc_rows, hw), jnp.float32),
            pltpu.VMEM((2 * CHUNK, hw), jnp.bfloat16),
            pltpu.VMEM((2 * CHUNK, hw), jnp.bfloat16),
        ],
        compiler_params=pltpu.CompilerParams(
            dimension_semantics=("arbitrary", "arbitrary"), vmem_limit_bytes=VMEM_LIMIT),
        name="sparse_attention",
    )(qit, iwt, k1, k2, vt, bias_tbl, tri)


def _mlp_kernel(x_ref, yr_ref, yd_ref, wor_ref, wod_ref, n2_ref, w1_ref, w2_ref, o_ref):
    h1 = x_ref[...] + _dot(yr_ref[...], wor_ref[...]) + _dot(yd_ref[...], wod_ref[...])
    ms = jnp.mean(h1 * h1, axis=-1, keepdims=True)
    u = (h1 * lax.rsqrt(ms + EPS) * n2_ref[...]).astype(jnp.bfloat16)
    o_ref[...] = h1
    d_ff = w1_ref.shape[1]
    for c in range(d_ff // FFN_CHUNK):
        sl = slice(c * FFN_CHUNK, (c + 1) * FFN_CHUNK)
        f = jnp.maximum(_dot(u, w1_ref[:, sl]), 0.0)
        o_ref[...] += _dot((f * f).astype(jnp.bfloat16), w2_ref[sl, :])


def _out_mlp(x2, yr, yd, wor, wod, n2, w1, w2):
    rows, D = x2.shape
    const = lambda a: pl.BlockSpec(a.shape, lambda i: (0, 0))
    tile = lambda w: pl.BlockSpec((ROW_TILE, w), lambda i: (i, 0))
    return pl.pallas_call(
        _mlp_kernel,
        grid=(rows // ROW_TILE,),
        in_specs=[tile(D), tile(yr.shape[1]), tile(yd.shape[1]),
                  const(wor), const(wod), const(n2), const(w1), const(w2)],
        out_specs=tile(D),
        out_shape=jax.ShapeDtypeStruct((rows, D), jnp.float32),
        compiler_params=pltpu.CompilerParams(
            dimension_semantics=("arbitrary",), vmem_limit_bytes=VMEM_LIMIT),
        name="out_mlp",
    )(x2, yr, yd, wor, wod, n2, w1, w2)


def kernel(x, meta_tokens, norm1_w, w_in, ret_norm_w, q_norm_w, k_norm_w, rel_bias,
           w_out, norm2_w, w_ff1, w_ff2):
    B, L, D = x.shape
    assert L % (2 * CHUNK) == 0 and L % ROW_TILE == 0 and w_in.shape[0] == 1
    topk = min(TOPK_MAX, L // 4)
    nb = L // CHUNK + 1
    bf = jnp.bfloat16
    f32 = jnp.float32

    rw = RET_HEADS * RET_DK
    dw = DSA_HEADS * DSA_DH
    sizes = (rw, rw, rw, rw, dw, DSA_DH, DSA_DH, IDX_HEADS * DSA_DH, DSA_DH, IDX_HEADS)
    offs = np.concatenate([[0], np.cumsum(sizes)])
    col = lambda i: w_in[0][:, int(offs[i]):int(offs[i + 1])]
    wr = jnp.concatenate([col(0), col(1), col(2), col(3)], axis=1).astype(bf)
    wq = col(4).reshape(D, DSA_HEADS, DSA_DH)
    wiq = col(7).reshape(D, IDX_HEADS, DSA_DH)
    wqi = jnp.concatenate([wq, wiq], axis=2).reshape(D, DSA_HEADS * 128).astype(bf)
    wk = jnp.concatenate([col(5), col(8)], axis=1).astype(bf)
    wv = jnp.concatenate([col(6), jnp.zeros((D, 128 - DSA_DH), f32)], axis=1).astype(bf)
    ww = jnp.concatenate([col(9), jnp.zeros((D, 128 - IDX_HEADS), f32)], axis=1).astype(bf)

    idx_scale = jnp.full((DSA_DH,), DSA_DH ** -0.5, f32)
    qscale = jnp.concatenate([q_norm_w[0].astype(f32) * (DSA_DH ** -0.5 * LOG2E), idx_scale])[None]
    kscale = jnp.concatenate([k_norm_w[0].astype(f32), jnp.ones((DSA_DH,), f32)])[None]
    meta_pad = jnp.concatenate([jnp.zeros((PAD, D), x.dtype), meta_tokens.astype(x.dtype)], axis=0)

    cos2, sin2 = _rope_tables(nb)
    bias_tbl = _bias_tables(rel_bias)
    tri = jnp.asarray(np.tril(np.ones((2 * CHUNK, 2 * CHUNK), np.float32), k=-1), dtype=bf)

    r_all, qit, k1, k2, vt, iwt = _inproj(x, meta_pad, norm1_w[0][None].astype(f32),
                                          wr, wqi, wk, wv, ww, qscale, kscale)
    y_ret = _retention(r_all, nb, cos2, sin2, ret_norm_w[0][None].astype(f32))
    y_dsa = _sparse_attention(qit, iwt, k1, k2, vt, bias_tbl, tri, topk, nb)

    wo = w_out[0].astype(bf)
    out = _out_mlp(x.reshape(B * L, D), y_ret.reshape(B * L, rw), y_dsa.reshape(B * L, dw),
                   wo[:rw], wo[rw:], norm2_w[0][None].astype(f32),
                   w_ff1[0].astype(bf), w_ff2[0].astype(bf))
    return out.reshape(B, L, D)
```

```python
import functools
import math

import numpy as np
import jax
import jax.numpy as jnp
from jax import lax
from jax.experimental import pallas as pl
from jax.experimental.pallas import tpu as pltpu

N_META = 16
CHUNK = 128
RET_HEADS = 4
RET_DK = 128
DSA_HEADS = 8
DSA_DH = 64
IDX_HEADS = 8
TOPK_MAX = 256
N_BUCKETS = 32
MAX_DISTANCE = 128
ROPE_BASE = 10000.0
EPS = 1e-6
NEG = -1e30
PAD = CHUNK - N_META
HALF = 64
INT_MIN = -(2 ** 31)
LOG2E = math.log2(math.e)

FFN_CHUNK = 1024
ROW_TILE = 256
VMEM_LIMIT = 56 * 1024 * 1024


def _dot(a, b):
    return jnp.dot(a, b, preferred_element_type=jnp.float32)


def _dot_nt(a, b):
    return lax.dot_general(a, b, (((1,), (1,)), ((), ())), preferred_element_type=jnp.float32)


def _dot_tn(a, b):
    return lax.dot_general(a, b, (((0,), (0,)), ((), ())), preferred_element_type=jnp.float32)


def _bucket_ranges():
    max_exact = N_BUCKETS // 2
    d = np.arange(0, 2 * CHUNK)
    large = max_exact + (np.log(np.maximum(d, 1) / max_exact) / math.log(MAX_DISTANCE / max_exact)
                         * (N_BUCKETS - max_exact)).astype(np.int64)
    bucket = np.where(d < max_exact, d, np.minimum(large, N_BUCKETS - 1))
    out = []
    for b in range(N_BUCKETS - 1):
        idx = np.nonzero(bucket == b)[0]
        out.append((int(idx.min()), int(idx.max())))
    return out


def _rope_kernel(cos_ref, sin_ref):
    n = pl.program_id(0)
    row = lax.broadcasted_iota(jnp.int32, (CHUNK, 128), 0)
    lane = lax.broadcasted_iota(jnp.int32, (CHUNK, 128), 1)
    pos = (n * CHUNK + row - PAD).astype(jnp.float32)
    frac = (lane % HALF).astype(jnp.float32) / HALF
    inv = jnp.exp(-frac * math.log(ROPE_BASE))
    ang = pos * inv
    cos_ref[...] = jnp.cos(ang)
    s = jnp.sin(ang)
    sin_ref[...] = jnp.where(lane < HALF, -s, s)


def _rope_tables(nb):
    return pl.pallas_call(
        _rope_kernel,
        grid=(nb,),
        out_specs=[pl.BlockSpec((CHUNK, 128), lambda n: (n, 0))] * 2,
        out_shape=[jax.ShapeDtypeStruct((nb * CHUNK, 128), jnp.float32)] * 2,
        name="rope_tables",
    )()


def _bias_kernel(rb_ref, out_ref):
    row = lax.broadcasted_iota(jnp.int32, (CHUNK, CHUNK), 0)
    col = lax.broadcasted_iota(jnp.int32, (CHUNK, CHUNK), 1)
    ranges = _bucket_ranges()
    out_ref[2] = jnp.zeros(out_ref.shape[1:], jnp.float32)
    for h in range(DSA_HEADS):
        far = rb_ref[N_BUCKETS - 1, h]
        for t in range(2):
            dist = col - row + t * CHUNK
            tile = jnp.zeros((CHUNK, CHUNK), jnp.float32)
            for b, (lo, hi) in enumerate(ranges):
                tile = jnp.where((dist >= lo) & (dist <= hi), (rb_ref[b, h] - far) * LOG2E, tile)
            out_ref[t, :, h * 128:(h + 1) * 128] = tile


def _bias_tables(rel_bias):
    return pl.pallas_call(
        _bias_kernel,
        in_specs=[pl.BlockSpec(memory_space=pltpu.SMEM)],
        out_specs=pl.BlockSpec(memory_space=pltpu.VMEM),
        out_shape=jax.ShapeDtypeStruct((3, CHUNK, DSA_HEADS * 128), jnp.float32),
        name="bias_tables",
    )(rel_bias.astype(jnp.float32))


def _inproj_kernel(xa_ref, xb_ref, meta_ref, n1_ref, wr_ref, wqi_ref, wk_ref, wv_ref, ww_ref,
                   qs_ref, ks_ref,
                   r_ref, qit_ref, k1_ref, k2_ref, vt_ref, iwt_ref, *, pad_last):
    s = pl.program_id(1)
    top = jnp.where(s == 0, meta_ref[...], xa_ref[0])
    bot = xb_ref[0]
    if pad_last:
        bot = jnp.where(s == pl.num_programs(1) - 1, 0.0, bot)
    src = jnp.concatenate([top, bot], axis=0)
    ms = jnp.mean(src * src, axis=-1, keepdims=True)
    u = (src * lax.rsqrt(ms + EPS) * n1_ref[...]).astype(jnp.bfloat16)

    r_ref[0] = _dot(u, wr_ref[...])

    lane = lax.broadcasted_iota(jnp.int32, (2 * CHUNK, 128), 1)
    lo = lane < HALF

    def head_rms(t):
        ssq = jnp.sum(jnp.where(lo, t * t, 0.0), axis=-1, keepdims=True)
        return lax.rsqrt(ssq / DSA_DH + EPS)

    pq = _dot(u, wqi_ref[...])
    for h in range(DSA_HEADS):
        t = pq[:, h * 128:(h + 1) * 128]
        t = t * (jnp.where(lo, head_rms(t), 1.0) * qs_ref[...])
        for j in range(2):
            qit_ref[0, j, :, h * 128:(h + 1) * 128] = (
                t[j * CHUNK:(j + 1) * CHUNK].T.astype(jnp.bfloat16))

    pk = _dot(u, wk_ref[...])
    kn = pk * head_rms(pk) * ks_ref[...]
    k1_ref[0] = jnp.where(lo, kn, 0.0).astype(jnp.bfloat16)
    k2_ref[0] = jnp.where(lo, 0.0, pk).astype(jnp.bfloat16)

    pv = _dot(u, wv_ref[...])
    vx = jnp.where(lo, pv, jnp.where(lane == HALF, 1.0, 0.0))
    iw = _dot(u, ww_ref[...]) * (IDX_HEADS ** -0.5)
    for j in range(2):
        rows = slice(j * CHUNK, (j + 1) * CHUNK)
        vt_ref[0, :, rows] = vx[rows].T.astype(jnp.bfloat16)
        iwt_ref[0, j] = iw[rows].T[0:IDX_HEADS, :]


def _inproj(x, meta_pad, n1, wr, wqi, wk, wv, ww, qscale, kscale):
    B, L, D = x.shape
    nx = L // CHUNK
    nbp = nx + 1 + ((nx + 1) % 2)
    tp = nbp * CHUNK
    const = lambda shape: pl.BlockSpec(shape, lambda b, s: (0,) * len(shape))
    blk = lambda w: pl.BlockSpec((1, 2 * CHUNK, w), lambda b, s: (b, s, 0))
    return pl.pallas_call(
        functools.partial(_inproj_kernel, pad_last=bool((nx + 1) % 2)),
        grid=(B, nbp // 2),
        in_specs=[
            pl.BlockSpec((1, CHUNK, D), lambda b, s: (b, jnp.maximum(2 * s - 1, 0), 0)),
            pl.BlockSpec((1, CHUNK, D), lambda b, s: (b, jnp.minimum(2 * s, nx - 1), 0)),
            const(meta_pad.shape), const(n1.shape), const(wr.shape), const(wqi.shape),
            const(wk.shape), const(wv.shape), const(ww.shape), const(qscale.shape),
            const(kscale.shape),
        ],
        out_specs=[
            blk(wr.shape[1]),
            pl.BlockSpec((1, 2, 128, DSA_HEADS * 128), lambda b, s: (b, s, 0, 0)),
            blk(128), blk(128),
            pl.BlockSpec((1, 128, 2 * CHUNK), lambda b, s: (b, 0, s)),
            pl.BlockSpec((1, 2, IDX_HEADS, CHUNK), lambda b, s: (b, s, 0, 0)),
        ],
        out_shape=[
            jax.ShapeDtypeStruct((B, tp, wr.shape[1]), jnp.float32),
            jax.ShapeDtypeStruct((B, nbp, 128, DSA_HEADS * 128), jnp.bfloat16),
            jax.ShapeDtypeStruct((B, tp, 128), jnp.bfloat16),
            jax.ShapeDtypeStruct((B, tp, 128), jnp.bfloat16),
            jax.ShapeDtypeStruct((B, 128, tp), jnp.bfloat16),
            jax.ShapeDtypeStruct((B, nbp, IDX_HEADS, CHUNK), jnp.float32),
        ],
        compiler_params=pltpu.CompilerParams(
            dimension_semantics=("arbitrary", "arbitrary"), vmem_limit_bytes=VMEM_LIMIT),
        name="in_projection",
    )(x, x, meta_pad, n1, wr, wqi, wk, wv, ww, qscale, kscale)


def _retention_kernel(ra_ref, rb_ref, cosa_ref, sina_ref, cosb_ref, sinb_ref, gain_ref, y_ref,
                      state_ref, tbl_ref):
    s = pl.program_id(1)
    w = RET_HEADS * RET_DK
    bf = jnp.bfloat16

    @pl.when(s == 0)
    def _():
        state_ref[...] = jnp.zeros_like(state_ref)
        row = lax.broadcasted_iota(jnp.int32, (CHUNK, CHUNK), 0)
        col = lax.broadcasted_iota(jnp.int32, (CHUNK, CHUNK), 1)
        rowf = row.astype(jnp.float32)
        diff = (row - col).astype(jnp.float32)
        for h in range(RET_HEADS):
            log_gamma = math.log(1.0 - 2.0 ** (-5.0 - h))
            tbl_ref[0, h] = jnp.where(diff >= 0, jnp.exp(log_gamma * jnp.maximum(diff, 0.0)), 0.0)
            tbl_ref[1, h] = jnp.exp(log_gamma * (rowf + 1.0))
            tbl_ref[2, h] = jnp.exp(log_gamma * (CHUNK - 1.0 - rowf))

    live = jnp.where(s == 0, 0.0, 1.0)
    chunks = ((ra_ref, cosa_ref, sina_ref, live), (rb_ref, cosb_ref, sinb_ref, None))
    for h in range(RET_HEADS):
        log_gamma = math.log(1.0 - 2.0 ** (-5.0 - h))
        sl = slice(h * 128, (h + 1) * 128)
        state = state_ref[h]
        for j, (r_ref, cos_ref, sin_ref, scale) in enumerate(chunks):
            cos2, sin2 = cos_ref[...], sin_ref[...]
            q = r_ref[0, :, sl]
            k = r_ref[0, :, w + h * 128: w + (h + 1) * 128]
            v = r_ref[0, :, 2 * w + h * 128: 2 * w + (h + 1) * 128]
            g = r_ref[0, :, 3 * w + h * 128: 3 * w + (h + 1) * 128]
            if scale is not None:
                k = k * scale
            v = v.astype(bf)
            qr = q * cos2 + pltpu.roll(q, HALF, 1) * sin2
            kr = (k * cos2 + pltpu.roll(k, HALF, 1) * sin2) * (RET_DK ** -0.5)

            scores = _dot_nt(qr.astype(bf), kr.astype(bf)) * tbl_ref[0, h]
            o = _dot(scores.astype(bf), v)
            o = o + _dot((qr * tbl_ref[1, h]).astype(bf), state.astype(bf))
            kv = _dot_tn((kr * tbl_ref[2, h]).astype(bf), v)
            state = state * math.exp(log_gamma * CHUNK) + kv

            ms = jnp.mean(o * o, axis=-1, keepdims=True)
            on = o * lax.rsqrt(ms + EPS) * gain_ref[:, sl]
            gate = g * (1.0 / (1.0 + jnp.exp(-g)))
            y_ref[0, j * CHUNK:(j + 1) * CHUNK, sl] = (gate * on).astype(y_ref.dtype)
        state_ref[h] = state


def _retention(r_all, nb, cos2, sin2, gain):
    B = r_all.shape[0]
    w = RET_HEADS * RET_DK
    nstep = (nb + 1) // 2
    rspec = lambda f: pl.BlockSpec((1, CHUNK, 4 * w), lambda b, s: (b, f(s), 0))
    tspec = lambda f: pl.BlockSpec((CHUNK, 128), lambda b, s: (f(s), 0))
    first = lambda s: jnp.maximum(2 * s - 1, 0)
    second = lambda s: 2 * s
    return pl.pallas_call(
        _retention_kernel,
        grid=(B, nstep),
        in_specs=[rspec(first), rspec(second), tspec(first), tspec(first), tspec(second),
                  tspec(second), pl.BlockSpec((1, w), lambda b, s: (0, 0))],
        out_specs=pl.BlockSpec((1, 2 * CHUNK, w), lambda b, s: (b, jnp.maximum(s - 1, 0), 0)),
        out_shape=jax.ShapeDtypeStruct((B, (nb - 1) * CHUNK, w), jnp.bfloat16),
        scratch_shapes=[pltpu.VMEM((RET_HEADS, RET_DK, RET_DK), jnp.float32),
                        pltpu.VMEM((3, RET_HEADS, CHUNK, CHUNK), jnp.float32)],
        compiler_params=pltpu.CompilerParams(
            dimension_semantics=("arbitrary", "arbitrary"), vmem_limit_bytes=VMEM_LIMIT),
        name="retention",
    )(r_all, r_all, cos2, sin2, cos2, sin2, gain)


def _bit_planes(tile):
    a = [tile[r * 8:(r + 1) * 8, :] for r in range(32)]
    j, m = 16, 0x0000FFFF
    while j:
        for k in range(32):
            if (k & j) == 0:
                t = (a[k] ^ lax.shift_right_logical(a[k + j], j)) & m
                a[k] = a[k] ^ t
                a[k + j] = a[k + j] ^ (t << j)
        j >>= 1
        if j:
            m = m ^ ((m << j) & 0xFFFFFFFF)
    return a


def _sublane_total(x):
    x = x + pltpu.roll(x, 4, 0)
    x = x + pltpu.roll(x, 2, 0)
    return x + pltpu.roll(x, 1, 0)


def _dsa_kernel(qit_ref, iwt_ref, k1_ref, k2_ref, vt_ref, bias_ref, tri_ref, y_ref,
                key_ref, plane_ref, m_ref, acc_ref, sa_ref, sb_ref, *, topk):
    n = pl.program_id(1) + 1
    npair = (n + 2) // 2
    bf = jnp.bfloat16
    H = DSA_HEADS
    KP = 2 * CHUNK
    G = plane_ref.shape[1]

    @pl.when((pl.program_id(0) == 0) & (pl.program_id(1) == 0))
    def _():
        plane_ref[...] = jnp.zeros_like(plane_ref)

    row = lax.broadcasted_iota(jnp.int32, (KP, CHUNK), 0)
    col = lax.broadcasted_iota(jnp.int32, (KP, CHUNK), 1)
    q_idx = n * CHUNK + col
    hs = lambda h: slice(h * 128, (h + 1) * 128)
    hp = lambda t: slice(t * 256, (t + 1) * 256)

    iw_rows = [jnp.broadcast_to(iwt_ref[0, 0, h:h + 1, :], (KP, CHUNK)) for h in range(IDX_HEADS)]

    def key_rows(g):
        last = k1_ref.shape[1] // KP - 1
        return pl.ds(pl.multiple_of(jnp.minimum(g, last) * KP, KP), KP)

    def score_pair(g):
        kblk = k2_ref[0, key_rows(g), :]
        sc = None
        for t in range(H // 2):
            s = _dot(kblk, qit_ref[0, 0, :, hp(t)])
            for hh in range(2):
                term = iw_rows[2 * t + hh] * jnp.maximum(s[:, hs(hh)], 0.0)
                sc = term if sc is None else sc + term
        k_idx = g * KP + row
        valid = (k_idx <= q_idx) & (k_idx >= PAD)
        sc = jnp.where(valid, sc, NEG)
        bits = lax.bitcast_convert_type(sc, jnp.int32)
        key = jnp.where(bits < 0, jnp.int32(INT_MIN) - bits, bits)
        key_ref[g] = key
        planes = _bit_planes(key ^ jnp.int32(INT_MIN))
        for p in range(32):
            plane_ref[p, g] = planes[p]

    def score_two(i, _):
        score_pair(2 * i)
        score_pair(2 * i + 1)
        return 0

    niter = (npair + 1) // 2
    lax.fori_loop(0, niter, score_two, 0)

    neg_key = -(int(np.float32(NEG).view(np.int32)) & 0x7FFFFFFF)
    g_idx = lax.broadcasted_iota(jnp.int32, (G, 8, CHUNK), 0)
    alive0 = jnp.where(g_idx < npair, jnp.int32(-1), jnp.int32(0))

    def ones_in(words):
        part = lax.population_count(words).astype(jnp.float32).sum(axis=0)
        return _sublane_total(part)

    def radix(p, carry):
        alive, above, thr, cnt = carry
        take1 = (above + cnt) >= topk
        above = jnp.where(take1, above, above + cnt)
        thr = jnp.where(take1, thr | lax.shift_right_logical(jnp.int32(INT_MIN), p), thr)
        drop = jnp.where(take1, jnp.int32(0), jnp.int32(-1))
        alive = alive & (plane_ref[p] ^ drop[None])
        cnt = ones_in(alive & plane_ref[jnp.minimum(p + 1, 31)])
        return alive, above, thr, cnt

    zero8 = jnp.zeros((8, CHUNK), jnp.float32)
    alive, n_gt, thr_u, _ = lax.fori_loop(
        0, 32, radix,
        (alive0, zero8, jnp.zeros((8, CHUNK), jnp.int32), ones_in(alive0 & plane_ref[0])))
    thr = thr_u ^ jnp.int32(INT_MIN)
    n_eq = ones_in(alive)
    masked_thr = thr == neg_key
    n_tie = jnp.where(masked_thr, 0.0, topk - n_gt)
    has_ties = jnp.max(jnp.where((n_eq > n_tie) & ~masked_thr, 1.0, 0.0)) > 0.5
    rep = lambda x: jnp.tile(x, (KP // 8, 1))
    thr_b = rep(thr)
    thr_sel_b = rep(jnp.where(masked_thr, jnp.int32(neg_key + 1), thr))
    n_tie_b = rep(n_tie)

    m_ref[...] = jnp.full_like(m_ref, NEG)
    acc_ref[...] = jnp.zeros_like(acc_ref)
    va = acc_ref.shape[0]

    def logits(g, dst_ref):
        dst_ref[...] = _dot(k1_ref[0, key_rows(g), :], qit_ref[0, 0])

    def attend(g, s_ref, ties_before, near, ties):
        key = key_ref[g]
        if ties:
            eq = key == thr_b
            eqf = jnp.where(eq, 1.0, 0.0)
            before = ties_before + _dot(tri_ref[...], eqf.astype(bf))
            sel = (key > thr_b) | (eq & (before < n_tie_b))
            ties_before = ties_before + jnp.sum(eqf, axis=0, keepdims=True)
        else:
            sel = key >= thr_sel_b
        ps, alphas = [], []
        for h in range(H):
            s = s_ref[:, hs(h)]
            if near:
                ta = jnp.clip(n - 2 * g, 0, 2)
                tb = jnp.clip(n - 2 * g - 1, 0, 2)
                s = s + jnp.concatenate([bias_ref[ta, :, hs(h)], bias_ref[tb, :, hs(h)]], axis=0)
            s = jnp.where(sel, s, NEG)
            m_prev = m_ref[h:h + 1, :]
            m_new = jnp.maximum(m_prev, jnp.max(s, axis=0, keepdims=True))
            m_ref[h:h + 1, :] = m_new
            alphas.append(jnp.exp2(m_prev - m_new))
            ps.append(jnp.exp2(s - m_new).astype(bf))
        pv = _dot(vt_ref[0, 0:va, key_rows(g)], jnp.concatenate(ps, axis=1))
        acc_ref[...] = acc_ref[...] * jnp.concatenate(alphas, axis=1) + pv
        return ties_before

    def attend_two(i, ties_before, near, ties):
        logits(2 * i + 1, sb_ref)
        ties_before = attend(2 * i, sa_ref, ties_before, near, ties)
        logits(2 * i + 2, sa_ref)
        return attend(2 * i + 1, sb_ref, ties_before, near, ties)

    nfar_iter = ((n - 1) // 2) // 2

    def run(ties):
        def go():
            logits(0, sa_ref)
            c = lax.fori_loop(0, nfar_iter, functools.partial(attend_two, near=False, ties=ties),
                              jnp.zeros((KP, CHUNK), jnp.float32))
            lax.fori_loop(nfar_iter, niter, functools.partial(attend_two, near=True, ties=ties), c)
        return go

    pl.when(has_ties)(run(True))
    pl.when(jnp.logical_not(has_ties))(run(False))

    lo = lax.broadcasted_iota(jnp.int32, (CHUNK, CHUNK), 1) < HALF
    for t in range(H // 2):
        tiles = []
        for h in (2 * t, 2 * t + 1):
            a = acc_ref[:, hs(h)] * (1.0 / acc_ref[HALF:HALF + 1, hs(h)])
            a = jnp.concatenate([a, jnp.zeros((CHUNK - va, CHUNK), jnp.float32)], axis=0)
            tiles.append(a.T)
        y_ref[0, :, hs(t)] = jnp.where(lo, tiles[0], pltpu.roll(tiles[1], HALF, 1)).astype(y_ref.dtype)


def _sparse_attention(qit, iwt, k1, k2, vt, bias_tbl, tri, topk, nb):
    B, nbp, _, hw = qit.shape
    tp = nbp * CHUNK
    nstep = nbp // 2
    nstep += nstep % 2
    kspec = pl.BlockSpec((1, tp, 128), lambda b, n: (b, 0, 0))
    acc_rows = 80
    return pl.pallas_call(
        functools.partial(_dsa_kernel, topk=topk),
        grid=(B, nb - 1),
        in_specs=[
            pl.BlockSpec((1, 1, 128, hw), lambda b, n: (b, n + 1, 0, 0)),
            pl.BlockSpec((1, 1, IDX_HEADS, CHUNK), lambda b, n: (b, n + 1, 0, 0)),
            kspec, kspec,
            pl.BlockSpec((1, 128, tp), lambda b, n: (b, 0, 0)),
            pl.BlockSpec(bias_tbl.shape, lambda b, n: (0, 0, 0)),
            pl.BlockSpec(tri.shape, lambda b, n: (0, 0)),
        ],
        out_specs=pl.BlockSpec((1, CHUNK, DSA_HEADS * DSA_DH), lambda b, n: (b, n, 0)),
        out_shape=jax.ShapeDtypeStruct((B, (nb - 1) * CHUNK, DSA_HEADS * DSA_DH), jnp.bfloat16),
        scratch_shapes=[
            pltpu.VMEM((nstep, 2 * CHUNK, CHUNK), jnp.int32),
            pltpu.VMEM((32, nstep, 8, CHUNK), jnp.int32),
            pltpu.VMEM((DSA_HEADS, CHUNK), jnp.float32),
            pltpu.VMEM((acc_rows, hw), jnp.float32),
            pltpu.VMEM((2 * CHUNK, hw), jnp.float32),
            pltpu.VMEM((2 * CHUNK, hw), jnp.float32),
        ],
        compiler_params=pltpu.CompilerParams(
            dimension_semantics=("arbitrary", "arbitrary"), vmem_limit_bytes=VMEM_LIMIT),
        name="sparse_attention",
    )(qit, iwt, k1, k2, vt, bias_tbl, tri)


def _mlp_kernel(x_ref, yr_ref, yd_ref, wor_ref, wod_ref, n2_ref, w1_ref, w2_ref, o_ref):
    h1 = x_ref[...] + _dot(yr_ref[...], wor_ref[...]) + _dot(yd_ref[...], wod_ref[...])
    ms = jnp.mean(h1 * h1, axis=-1, keepdims=True)
    u = (h1 * lax.rsqrt(ms + EPS) * n2_ref[...]).astype(jnp.bfloat16)
    o_ref[...] = h1
    d_ff = w1_ref.shape[1]
    for c in range(d_ff // FFN_CHUNK):
        sl = slice(c * FFN_CHUNK, (c + 1) * FFN_CHUNK)
        f = jnp.maximum(_dot(u, w1_ref[:, sl]), 0.0)
        o_ref[...] += _dot((f * f).astype(jnp.bfloat16), w2_ref[sl, :])


def _out_mlp(x2, yr, yd, wor, wod, n2, w1, w2):
    rows, D = x2.shape
    const = lambda a: pl.BlockSpec(a.shape, lambda i: (0, 0))
    tile = lambda w: pl.BlockSpec((ROW_TILE, w), lambda i: (i, 0))
    return pl.pallas_call(
        _mlp_kernel,
        grid=(rows // ROW_TILE,),
        in_specs=[tile(D), tile(yr.shape[1]), tile(yd.shape[1]),
                  const(wor), const(wod), const(n2), const(w1), const(w2)],
        out_specs=tile(D),
        out_shape=jax.ShapeDtypeStruct((rows, D), jnp.float32),
        compiler_params=pltpu.CompilerParams(
            dimension_semantics=("arbitrary",), vmem_limit_bytes=VMEM_LIMIT),
        name="out_mlp",
    )(x2, yr, yd, wor, wod, n2, w1, w2)


def kernel(x, meta_tokens, norm1_w, w_in, ret_norm_w, q_norm_w, k_norm_w, rel_bias,
           w_out, norm2_w, w_ff1, w_ff2):
    B, L, D = x.shape
    assert L % (2 * CHUNK) == 0 and L % ROW_TILE == 0 and w_in.shape[0] == 1
    topk = min(TOPK_MAX, L // 4)
    nb = L // CHUNK + 1
    bf = jnp.bfloat16
    f32 = jnp.float32

    rw = RET_HEADS * RET_DK
    dw = DSA_HEADS * DSA_DH
    sizes = (rw, rw, rw, rw, dw, DSA_DH, DSA_DH, IDX_HEADS * DSA_DH, DSA_DH, IDX_HEADS)
    offs = np.concatenate([[0], np.cumsum(sizes)])
    col = lambda i: w_in[0][:, int(offs[i]):int(offs[i + 1])]
    wr = jnp.concatenate([col(0), col(1), col(2), col(3)], axis=1).astype(bf)
    wq = col(4).reshape(D, DSA_HEADS, DSA_DH)
    wiq = col(7).reshape(D, IDX_HEADS, DSA_DH)
    wqi = jnp.concatenate([wq, wiq], axis=2).reshape(D, DSA_HEADS * 128).astype(bf)
    wk = jnp.concatenate([col(5), col(8)], axis=1).astype(bf)
    wv = jnp.concatenate([col(6), jnp.zeros((D, 128 - DSA_DH), f32)], axis=1).astype(bf)
    ww = jnp.concatenate([col(9), jnp.zeros((D, 128 - IDX_HEADS), f32)], axis=1).astype(bf)

    idx_scale = jnp.full((DSA_DH,), DSA_DH ** -0.5, f32)
    qscale = jnp.concatenate([q_norm_w[0].astype(f32) * (DSA_DH ** -0.5 * LOG2E), idx_scale])[None]
    kscale = jnp.concatenate([k_norm_w[0].astype(f32), jnp.ones((DSA_DH,), f32)])[None]
    meta_pad = jnp.concatenate([jnp.zeros((PAD, D), x.dtype), meta_tokens.astype(x.dtype)], axis=0)

    cos2, sin2 = _rope_tables(nb)
    bias_tbl = _bias_tables(rel_bias)
    tri = jnp.asarray(np.tril(np.ones((2 * CHUNK, 2 * CHUNK), np.float32), k=-1), dtype=bf)

    r_all, qit, k1, k2, vt, iwt = _inproj(x, meta_pad, norm1_w[0][None].astype(f32),
                                          wr, wqi, wk, wv, ww, qscale, kscale)
    y_ret = _retention(r_all, nb, cos2, sin2, ret_norm_w[0][None].astype(f32))
    y_dsa = _sparse_attention(qit, iwt, k1, k2, vt, bias_tbl, tri, topk, nb)

    wo = w_out[0].astype(bf)
    out = _out_mlp(x.reshape(B * L, D), y_ret.reshape(B * L, rw), y_dsa.reshape(B * L, dw),
                   wo[:rw], wo[rw:], norm2_w[0][None].astype(f32),
                   w_ff1[0].astype(bf), w_ff2[0].astype(bf))
    return out.reshape(B, L, D)
```

```python
import functools
import math

import numpy as np
import jax
import jax.numpy as jnp
from jax import lax
from jax.experimental import pallas as pl
from jax.experimental.pallas import tpu as pltpu

N_META = 16
CHUNK = 128
RET_HEADS = 4
RET_DK = 128
DSA_HEADS = 8
DSA_DH = 64
IDX_HEADS = 8
TOPK_MAX = 256
N_BUCKETS = 32
MAX_DISTANCE = 128
ROPE_BASE = 10000.0
EPS = 1e-6
NEG = -1e30
PAD = CHUNK - N_META
HALF = 64
INT_MIN = -(2 ** 31)
LOG2E = math.log2(math.e)

FFN_CHUNK = 1024
ROW_TILE = 256
VMEM_LIMIT = 56 * 1024 * 1024


def _dot(a, b):
    return jnp.dot(a, b, preferred_element_type=jnp.float32)


def _dot_nt(a, b):
    return lax.dot_general(a, b, (((1,), (1,)), ((), ())), preferred_element_type=jnp.float32)


def _dot_tn(a, b):
    return lax.dot_general(a, b, (((0,), (0,)), ((), ())), preferred_element_type=jnp.float32)


def _bucket_ranges():
    max_exact = N_BUCKETS // 2
    d = np.arange(0, 2 * CHUNK)
    large = max_exact + (np.log(np.maximum(d, 1) / max_exact) / math.log(MAX_DISTANCE / max_exact)
                         * (N_BUCKETS - max_exact)).astype(np.int64)
    bucket = np.where(d < max_exact, d, np.minimum(large, N_BUCKETS - 1))
    out = []
    for b in range(N_BUCKETS - 1):
        idx = np.nonzero(bucket == b)[0]
        out.append((int(idx.min()), int(idx.max())))
    return out


def _rope_kernel(cos_ref, sin_ref):
    n = pl.program_id(0)
    row = lax.broadcasted_iota(jnp.int32, (CHUNK, 128), 0)
    lane = lax.broadcasted_iota(jnp.int32, (CHUNK, 128), 1)
    pos = (n * CHUNK + row - PAD).astype(jnp.float32)
    frac = (lane % HALF).astype(jnp.float32) / HALF
    inv = jnp.exp(-frac * math.log(ROPE_BASE))
    ang = pos * inv
    cos_ref[...] = jnp.cos(ang)
    s = jnp.sin(ang)
    sin_ref[...] = jnp.where(lane < HALF, -s, s)


def _rope_tables(nb):
    return pl.pallas_call(
        _rope_kernel,
        grid=(nb,),
        out_specs=[pl.BlockSpec((CHUNK, 128), lambda n: (n, 0))] * 2,
        out_shape=[jax.ShapeDtypeStruct((nb * CHUNK, 128), jnp.float32)] * 2,
        name="rope_tables",
    )()


def _bias_kernel(rb_ref, out_ref):
    row = lax.broadcasted_iota(jnp.int32, (CHUNK, CHUNK), 0)
    col = lax.broadcasted_iota(jnp.int32, (CHUNK, CHUNK), 1)
    ranges = _bucket_ranges()
    out_ref[2] = jnp.zeros(out_ref.shape[1:], jnp.float32)
    for h in range(DSA_HEADS):
        far = rb_ref[N_BUCKETS - 1, h]
        for t in range(2):
            dist = col - row + t * CHUNK
            tile = jnp.zeros((CHUNK, CHUNK), jnp.float32)
            for b, (lo, hi) in enumerate(ranges):
                tile = jnp.where((dist >= lo) & (dist <= hi), (rb_ref[b, h] - far) * LOG2E, tile)
            out_ref[t, :, h * 128:(h + 1) * 128] = tile


def _bias_tables(rel_bias):
    return pl.pallas_call(
        _bias_kernel,
        in_specs=[pl.BlockSpec(memory_space=pltpu.SMEM)],
        out_specs=pl.BlockSpec(memory_space=pltpu.VMEM),
        out_shape=jax.ShapeDtypeStruct((3, CHUNK, DSA_HEADS * 128), jnp.float32),
        name="bias_tables",
    )(rel_bias.astype(jnp.float32))


def _inproj_kernel(xa_ref, xb_ref, meta_ref, n1_ref, wr_ref, wqi_ref, wk_ref, wv_ref, ww_ref,
                   qs_ref, ks_ref,
                   r_ref, qit_ref, k1_ref, k2_ref, vt_ref, iwt_ref, *, pad_last):
    s = pl.program_id(1)
    top = jnp.where(s == 0, meta_ref[...], xa_ref[0])
    bot = xb_ref[0]
    if pad_last:
        bot = jnp.where(s == pl.num_programs(1) - 1, 0.0, bot)
    src = jnp.concatenate([top, bot], axis=0)
    ms = jnp.mean(src * src, axis=-1, keepdims=True)
    u = (src * lax.rsqrt(ms + EPS) * n1_ref[...]).astype(jnp.bfloat16)

    r_ref[0] = _dot(u, wr_ref[...])

    lane = lax.broadcasted_iota(jnp.int32, (2 * CHUNK, 128), 1)
    lo = lane < HALF

    def head_rms(t):
        ssq = jnp.sum(jnp.where(lo, t * t, 0.0), axis=-1, keepdims=True)
        return lax.rsqrt(ssq / DSA_DH + EPS)

    pq = _dot(u, wqi_ref[...])
    for h in range(DSA_HEADS):
        t = pq[:, h * 128:(h + 1) * 128]
        t = t * (jnp.where(lo, head_rms(t), 1.0) * qs_ref[...])
        for j in range(2):
            qit_ref[0, j, :, h * 128:(h + 1) * 128] = (
                t[j * CHUNK:(j + 1) * CHUNK].T.astype(jnp.bfloat16))

    pk = _dot(u, wk_ref[...])
    kn = pk * head_rms(pk) * ks_ref[...]
    k1_ref[0] = jnp.where(lo, kn, 0.0).astype(jnp.bfloat16)
    k2_ref[0] = jnp.where(lo, 0.0, pk).astype(jnp.bfloat16)

    pv = _dot(u, wv_ref[...])
    vx = jnp.where(lo, pv, jnp.where(lane == HALF, 1.0, 0.0))
    iw = _dot(u, ww_ref[...]) * (IDX_HEADS ** -0.5)
    for j in range(2):
        rows = slice(j * CHUNK, (j + 1) * CHUNK)
        vt_ref[0, :, rows] = vx[rows].T.astype(jnp.bfloat16)
        iwt_ref[0, j] = iw[rows].T[0:IDX_HEADS, :]


def _inproj(x, meta_pad, n1, wr, wqi, wk, wv, ww, qscale, kscale):
    B, L, D = x.shape
    nx = L // CHUNK
    nbp = nx + 1 + ((nx + 1) % 2)
    tp = nbp * CHUNK
    const = lambda shape: pl.BlockSpec(shape, lambda b, s: (0,) * len(shape))
    blk = lambda w: pl.BlockSpec((1, 2 * CHUNK, w), lambda b, s: (b, s, 0))
    return pl.pallas_call(
        functools.partial(_inproj_kernel, pad_last=bool((nx + 1) % 2)),
        grid=(B, nbp // 2),
        in_specs=[
            pl.BlockSpec((1, CHUNK, D), lambda b, s: (b, jnp.maximum(2 * s - 1, 0), 0)),
            pl.BlockSpec((1, CHUNK, D), lambda b, s: (b, jnp.minimum(2 * s, nx - 1), 0)),
            const(meta_pad.shape), const(n1.shape), const(wr.shape), const(wqi.shape),
            const(wk.shape), const(wv.shape), const(ww.shape), const(qscale.shape),
            const(kscale.shape),
        ],
        out_specs=[
            blk(wr.shape[1]),
            pl.BlockSpec((1, 2, 128, DSA_HEADS * 128), lambda b, s: (b, s, 0, 0)),
            blk(128), blk(128),
            pl.BlockSpec((1, 128, 2 * CHUNK), lambda b, s: (b, 0, s)),
            pl.BlockSpec((1, 2, IDX_HEADS, CHUNK), lambda b, s: (b, s, 0, 0)),
        ],
        out_shape=[
            jax.ShapeDtypeStruct((B, tp, wr.shape[1]), jnp.float32),
            jax.ShapeDtypeStruct((B, nbp, 128, DSA_HEADS * 128), jnp.bfloat16),
            jax.ShapeDtypeStruct((B, tp, 128), jnp.bfloat16),
            jax.ShapeDtypeStruct((B, tp, 128), jnp.bfloat16),
            jax.ShapeDtypeStruct((B, 128, tp), jnp.bfloat16),
            jax.ShapeDtypeStruct((B, nbp, IDX_HEADS, CHUNK), jnp.float32),
        ],
        compiler_params=pltpu.CompilerParams(
            dimension_semantics=("arbitrary", "arbitrary"), vmem_limit_bytes=VMEM_LIMIT),
        name="in_projection",
    )(x, x, meta_pad, n1, wr, wqi, wk, wv, ww, qscale, kscale)


def _retention_kernel(ra_ref, rb_ref, cosa_ref, sina_ref, cosb_ref, sinb_ref, gain_ref, y_ref,
                      state_ref, tbl_ref):
    s = pl.program_id(1)
    w = RET_HEADS * RET_DK
    bf = jnp.bfloat16

    @pl.when(s == 0)
    def _():
        state_ref[...] = jnp.zeros_like(state_ref)
        row = lax.broadcasted_iota(jnp.int32, (CHUNK, CHUNK), 0)
        col = lax.broadcasted_iota(jnp.int32, (CHUNK, CHUNK), 1)
        rowf = row.astype(jnp.float32)
        diff = (row - col).astype(jnp.float32)
        for h in range(RET_HEADS):
            log_gamma = math.log(1.0 - 2.0 ** (-5.0 - h))
            tbl_ref[0, h] = jnp.where(diff >= 0, jnp.exp(log_gamma * jnp.maximum(diff, 0.0)), 0.0)
            tbl_ref[1, h] = jnp.exp(log_gamma * (rowf + 1.0))
            tbl_ref[2, h] = jnp.exp(log_gamma * (CHUNK - 1.0 - rowf))

    live = jnp.where(s == 0, 0.0, 1.0)
    chunks = ((ra_ref, cosa_ref, sina_ref, live), (rb_ref, cosb_ref, sinb_ref, None))
    for h in range(RET_HEADS):
        log_gamma = math.log(1.0 - 2.0 ** (-5.0 - h))
        sl = slice(h * 128, (h + 1) * 128)
        state = state_ref[h]
        for j, (r_ref, cos_ref, sin_ref, scale) in enumerate(chunks):
            cos2, sin2 = cos_ref[...], sin_ref[...]
            q = r_ref[0, :, sl]
            k = r_ref[0, :, w + h * 128: w + (h + 1) * 128]
            v = r_ref[0, :, 2 * w + h * 128: 2 * w + (h + 1) * 128]
            g = r_ref[0, :, 3 * w + h * 128: 3 * w + (h + 1) * 128]
            if scale is not None:
                k = k * scale
            v = v.astype(bf)
            qr = q * cos2 + pltpu.roll(q, HALF, 1) * sin2
            kr = (k * cos2 + pltpu.roll(k, HALF, 1) * sin2) * (RET_DK ** -0.5)

            scores = _dot_nt(qr.astype(bf), kr.astype(bf)) * tbl_ref[0, h]
            o = _dot(scores.astype(bf), v)
            o = o + _dot((qr * tbl_ref[1, h]).astype(bf), state.astype(bf))
            kv = _dot_tn((kr * tbl_ref[2, h]).astype(bf), v)
            state = state * math.exp(log_gamma * CHUNK) + kv

            ms = jnp.mean(o * o, axis=-1, keepdims=True)
            on = o * lax.rsqrt(ms + EPS) * gain_ref[:, sl]
            gate = g * (1.0 / (1.0 + jnp.exp(-g)))
            y_ref[0, j * CHUNK:(j + 1) * CHUNK, sl] = (gate * on).astype(y_ref.dtype)
        state_ref[h] = state


def _retention(r_all, nb, cos2, sin2, gain):
    B = r_all.shape[0]
    w = RET_HEADS * RET_DK
    nstep = (nb + 1) // 2
    rspec = lambda f: pl.BlockSpec((1, CHUNK, 4 * w), lambda b, s: (b, f(s), 0))
    tspec = lambda f: pl.BlockSpec((CHUNK, 128), lambda b, s: (f(s), 0))
    first = lambda s: jnp.maximum(2 * s - 1, 0)
    second = lambda s: 2 * s
    return pl.pallas_call(
        _retention_kernel,
        grid=(B, nstep),
        in_specs=[rspec(first), rspec(second), tspec(first), tspec(first), tspec(second),
                  tspec(second), pl.BlockSpec((1, w), lambda b, s: (0, 0))],
        out_specs=pl.BlockSpec((1, 2 * CHUNK, w), lambda b, s: (b, jnp.maximum(s - 1, 0), 0)),
        out_shape=jax.ShapeDtypeStruct((B, (nb - 1) * CHUNK, w), jnp.bfloat16),
        scratch_shapes=[pltpu.VMEM((RET_HEADS, RET_DK, RET_DK), jnp.float32),
                        pltpu.VMEM((3, RET_HEADS, CHUNK, CHUNK), jnp.float32)],
        compiler_params=pltpu.CompilerParams(
            dimension_semantics=("arbitrary", "arbitrary"), vmem_limit_bytes=VMEM_LIMIT),
        name="retention",
    )(r_all, r_all, cos2, sin2, cos2, sin2, gain)


def _bit_planes(tile):
    a = [tile[r * 8:(r + 1) * 8, :] for r in range(32)]
    j, m = 16, 0x0000FFFF
    while j:
        for k in range(32):
            if (k & j) == 0:
                t = (a[k] ^ lax.shift_right_logical(a[k + j], j)) & m
                a[k] = a[k] ^ t
                a[k + j] = a[k + j] ^ (t << j)
        j >>= 1
        if j:
            m = m ^ ((m << j) & 0xFFFFFFFF)
    return a


def _sublane_total(x):
    x = x + pltpu.roll(x, 4, 0)
    x = x + pltpu.roll(x, 2, 0)
    return x + pltpu.roll(x, 1, 0)


def _dsa_kernel(qa_ref, qb_ref, iwt_ref, k1_ref, k2_ref, vt_ref, bias_ref, tri_ref, y_ref,
                key_ref, plane_ref, thr_ref, ntie_ref, flag_ref, m_ref, acc_ref, sa_ref, sb_ref,
                *, topk):
    t = pl.program_id(1)
    last_q = pl.num_programs(1) - 1
    bf = jnp.bfloat16
    H = DSA_HEADS
    KP = 2 * CHUNK
    G = plane_ref.shape[1]
    cur, nxt = t % 2, (t + 1) % 2

    @pl.when((pl.program_id(0) == 0) & (t == 0))
    def _():
        plane_ref[...] = jnp.zeros_like(plane_ref)

    @pl.when(t == 0)
    def _():
        thr_ref[0] = jnp.zeros(thr_ref.shape[1:], jnp.int32)
        ntie_ref[0] = jnp.zeros(ntie_ref.shape[1:], jnp.float32)
        flag_ref[0] = 0

    row = lax.broadcasted_iota(jnp.int32, (KP, CHUNK), 0)
    col = lax.broadcasted_iota(jnp.int32, (KP, CHUNK), 1)
    hs = lambda h: slice(h * 128, (h + 1) * 128)
    hp = lambda c: slice(c * 256, (c + 1) * 256)
    steps = lambda n: ((n + 2) // 2 + 1) // 2

    def key_rows(g):
        last = k1_ref.shape[1] // KP - 1
        return pl.ds(pl.multiple_of(jnp.minimum(g, last) * KP, KP), KP)

    nb_ = t + 1
    iters_b = jnp.where(t < last_q, steps(nb_), 0)
    qb_idx = nb_ * CHUNK + col
    iw_rows = [jnp.broadcast_to(iwt_ref[0, 0, h:h + 1, :], (KP, CHUNK)) for h in range(IDX_HEADS)]

    def score_pair(g):
        kblk = k2_ref[0, key_rows(g), :]
        sc = None
        for c in range(H // 2):
            s = _dot(kblk, qb_ref[0, 0, :, hp(c)])
            for hh in range(2):
                term = iw_rows[2 * c + hh] * jnp.maximum(s[:, hs(hh)], 0.0)
                sc = term if sc is None else sc + term
        k_idx = g * KP + row
        valid = (k_idx <= qb_idx) & (k_idx >= PAD)
        sc = jnp.where(valid, sc, NEG)
        bits = lax.bitcast_convert_type(sc, jnp.int32)
        key = jnp.where(bits < 0, jnp.int32(INT_MIN) - bits, bits)
        key_ref[nxt, g] = key
        planes = _bit_planes(key ^ jnp.int32(INT_MIN))
        for p in range(32):
            plane_ref[p, g] = planes[p]

    def score_two(i):
        score_pair(2 * i)
        score_pair(2 * i + 1)

    na_ = t
    iters_a = jnp.where(t > 0, steps(na_), 0)
    far_a = jnp.where(t > 0, ((na_ - 1) // 2) // 2, 0)
    neg_key = -(int(np.float32(NEG).view(np.int32)) & 0x7FFFFFFF)
    rep = lambda x: jnp.tile(x, (KP // 8, 1))
    thr = thr_ref[cur]
    n_tie = ntie_ref[cur]
    masked_thr = thr == neg_key
    thr_b = rep(thr)
    thr_sel_b = rep(jnp.where(masked_thr, jnp.int32(neg_key + 1), thr))
    n_tie_b = rep(n_tie)
    has_ties = flag_ref[cur] > 0
    va = acc_ref.shape[0]

    @pl.when(t > 0)
    def _():
        m_ref[...] = jnp.full_like(m_ref, NEG)
        acc_ref[...] = jnp.zeros_like(acc_ref)

    def logits(g, dst_ref):
        dst_ref[...] = _dot(k1_ref[0, key_rows(g), :], qa_ref[0, 0])

    def attend(g, s_ref, ties_before, near, ties):
        key = key_ref[cur, g]
        if ties:
            eq = key == thr_b
            eqf = jnp.where(eq, 1.0, 0.0)
            before = ties_before + _dot(tri_ref[...], eqf.astype(bf))
            sel = (key > thr_b) | (eq & (before < n_tie_b))
            ties_before = ties_before + jnp.sum(eqf, axis=0, keepdims=True)
        else:
            sel = key >= thr_sel_b
        ps, alphas = [], []
        for h in range(H):
            s = s_ref[:, hs(h)]
            if near:
                ta = jnp.clip(na_ - 2 * g, 0, 2)
                tb = jnp.clip(na_ - 2 * g - 1, 0, 2)
                s = s + jnp.concatenate([bias_ref[ta, :, hs(h)], bias_ref[tb, :, hs(h)]], axis=0)
            s = jnp.where(sel, s, NEG)
            m_prev = m_ref[h:h + 1, :]
            m_new = jnp.maximum(m_prev, jnp.max(s, axis=0, keepdims=True))
            m_ref[h:h + 1, :] = m_new
            alphas.append(jnp.exp2(m_prev - m_new))
            ps.append(jnp.exp2(s - m_new).astype(bf))
        pv = _dot(vt_ref[0, 0:va, key_rows(g)], jnp.concatenate(ps, axis=1))
        acc_ref[...] = acc_ref[...] * jnp.concatenate(alphas, axis=1) + pv
        return ties_before

    def attend_two(i, ties_before, near, ties, fused):
        logits(2 * i + 1, sb_ref)
        ties_before = attend(2 * i, sa_ref, ties_before, near, ties)
        logits(2 * i + 2, sa_ref)
        ties_before = attend(2 * i + 1, sb_ref, ties_before, near, ties)
        if fused:
            score_two(i)
        return ties_before

    zero_ties = jnp.zeros((KP, CHUNK), jnp.float32)
    loop = lambda lo, hi, **kw: (lambda c: lax.fori_loop(lo, hi, functools.partial(attend_two, **kw), c))

    @pl.when(has_ties)
    def _():
        logits(0, sa_ref)
        c = loop(0, far_a, near=False, ties=True, fused=False)(zero_ties)
        loop(far_a, iters_a, near=True, ties=True, fused=False)(c)
        lax.fori_loop(0, iters_b, lambda i, _: (score_two(i), 0)[1], 0)

    @pl.when(jnp.logical_not(has_ties))
    def _():
        logits(0, sa_ref)
        fused_hi = jnp.minimum(iters_a, iters_b)
        loop(0, jnp.minimum(far_a, fused_hi), near=False, ties=False, fused=True)(zero_ties)
        loop(jnp.minimum(far_a, fused_hi), fused_hi, near=True, ties=False, fused=True)(zero_ties)
        loop(fused_hi, iters_a, near=True, ties=False, fused=False)(zero_ties)
        lax.fori_loop(fused_hi, iters_b, lambda i, _: (score_two(i), 0)[1], 0)

    @pl.when(t > 0)
    def _():
        lo = lax.broadcasted_iota(jnp.int32, (CHUNK, CHUNK), 1) < HALF
        for c in range(H // 2):
            tiles = []
            for h in (2 * c, 2 * c + 1):
                a = acc_ref[:, hs(h)] * (1.0 / acc_ref[HALF:HALF + 1, hs(h)])
                a = jnp.concatenate([a, jnp.zeros((CHUNK - va, CHUNK), jnp.float32)], axis=0)
                tiles.append(a.T)
            y_ref[0, :, hs(c)] = jnp.where(lo, tiles[0], pltpu.roll(tiles[1], HALF, 1)).astype(y_ref.dtype)

    @pl.when(t < last_q)
    def _():
        npair_b = (nb_ + 2) // 2
        g_idx = lax.broadcasted_iota(jnp.int32, (G, 8, CHUNK), 0)
        alive0 = jnp.where(g_idx < npair_b, jnp.int32(-1), jnp.int32(0))

        def ones_in(words):
            part = lax.population_count(words).astype(jnp.float32).sum(axis=0)
            return _sublane_total(part)

        def radix(p, carry):
            alive, above, thr_u, cnt = carry
            take1 = (above + cnt) >= topk
            above = jnp.where(take1, above, above + cnt)
            thr_u = jnp.where(take1, thr_u | lax.shift_right_logical(jnp.int32(INT_MIN), p), thr_u)
            drop = jnp.where(take1, jnp.int32(0), jnp.int32(-1))
            alive = alive & (plane_ref[p] ^ drop[None])
            cnt = ones_in(alive & plane_ref[jnp.minimum(p + 1, 31)])
            return alive, above, thr_u, cnt

        zero8 = jnp.zeros((8, CHUNK), jnp.float32)
        alive, n_gt, thr_u, _ = lax.fori_loop(
            0, 32, radix,
            (alive0, zero8, jnp.zeros((8, CHUNK), jnp.int32), ones_in(alive0 & plane_ref[0])))
        thr_n = thr_u ^ jnp.int32(INT_MIN)
        n_eq = ones_in(alive)
        masked_n = thr_n == neg_key
        n_tie_n = jnp.where(masked_n, 0.0, topk - n_gt)
        thr_ref[nxt] = thr_n
        ntie_ref[nxt] = n_tie_n
        more = jnp.max(jnp.where((n_eq > n_tie_n) & ~masked_n, 1.0, 0.0)) > 0.5
        flag_ref[nxt] = more.astype(jnp.int32)


def _sparse_attention(qit, iwt, k1, k2, vt, bias_tbl, tri, topk, nb):
    B, nbp, _, hw = qit.shape
    tp = nbp * CHUNK
    nstep = nbp // 2
    nstep += nstep % 2
    kspec = pl.BlockSpec((1, tp, 128), lambda b, t: (b, 0, 0))
    acc_rows = 80
    nxt = lambda t: jnp.minimum(t + 1, nb - 1)
    return pl.pallas_call(
        functools.partial(_dsa_kernel, topk=topk),
        grid=(B, nb),
        in_specs=[
            pl.BlockSpec((1, 1, 128, hw), lambda b, t: (b, t, 0, 0)),
            pl.BlockSpec((1, 1, 128, hw), lambda b, t: (b, nxt(t), 0, 0)),
            pl.BlockSpec((1, 1, IDX_HEADS, CHUNK), lambda b, t: (b, nxt(t), 0, 0)),
            kspec, kspec,
            pl.BlockSpec((1, 128, tp), lambda b, t: (b, 0, 0)),
            pl.BlockSpec(bias_tbl.shape, lambda b, t: (0, 0, 0)),
            pl.BlockSpec(tri.shape, lambda b, t: (0, 0)),
        ],
        out_specs=pl.BlockSpec((1, CHUNK, DSA_HEADS * DSA_DH),
                               lambda b, t: (b, jnp.maximum(t - 1, 0), 0)),
        out_shape=jax.ShapeDtypeStruct((B, (nb - 1) * CHUNK, DSA_HEADS * DSA_DH), jnp.bfloat16),
        scratch_shapes=[
            pltpu.VMEM((2, nstep, 2 * CHUNK, CHUNK), jnp.int32),
            pltpu.VMEM((32, nstep, 8, CHUNK), jnp.int32),
            pltpu.VMEM((2, 8, CHUNK), jnp.int32),
            pltpu.VMEM((2, 8, CHUNK), jnp.float32),
            pltpu.SMEM((2,), jnp.int32),
            pltpu.VMEM((DSA_HEADS, CHUNK), jnp.float32),
            pltpu.VMEM((acc_rows, hw), jnp.float32),
            pltpu.VMEM((2 * CHUNK, hw), jnp.float32),
            pltpu.VMEM((2 * CHUNK, hw), jnp.float32),
        ],
        compiler_params=pltpu.CompilerParams(
            dimension_semantics=("arbitrary", "arbitrary"), vmem_limit_bytes=VMEM_LIMIT),
        name="sparse_attention",
    )(qit, qit, iwt, k1, k2, vt, bias_tbl, tri)


def _mlp_kernel(x_ref, yr_ref, yd_ref, wor_ref, wod_ref, n2_ref, w1_ref, w2_ref, o_ref):
    h1 = x_ref[...] + _dot(yr_ref[...], wor_ref[...]) + _dot(yd_ref[...], wod_ref[...])
    ms = jnp.mean(h1 * h1, axis=-1, keepdims=True)
    u = (h1 * lax.rsqrt(ms + EPS) * n2_ref[...]).astype(jnp.bfloat16)
    o_ref[...] = h1
    d_ff = w1_ref.shape[1]
    for c in range(d_ff // FFN_CHUNK):
        sl = slice(c * FFN_CHUNK, (c + 1) * FFN_CHUNK)
        f = jnp.maximum(_dot(u, w1_ref[:, sl]), 0.0)
        o_ref[...] += _dot((f * f).astype(jnp.bfloat16), w2_ref[sl, :])


def _out_mlp(x2, yr, yd, wor, wod, n2, w1, w2):
    rows, D = x2.shape
    const = lambda a: pl.BlockSpec(a.shape, lambda i: (0, 0))
    tile = lambda w: pl.BlockSpec((ROW_TILE, w), lambda i: (i, 0))
    return pl.pallas_call(
        _mlp_kernel,
        grid=(rows // ROW_TILE,),
        in_specs=[tile(D), tile(yr.shape[1]), tile(yd.shape[1]),
                  const(wor), const(wod), const(n2), const(w1), const(w2)],
        out_specs=tile(D),
        out_shape=jax.ShapeDtypeStruct((rows, D), jnp.float32),
        compiler_params=pltpu.CompilerParams(
            dimension_semantics=("arbitrary",), vmem_limit_bytes=VMEM_LIMIT),
        name="out_mlp",
    )(x2, yr, yd, wor, wod, n2, w1, w2)


def kernel(x, meta_tokens, norm1_w, w_in, ret_norm_w, q_norm_w, k_norm_w, rel_bias,
           w_out, norm2_w, w_ff1, w_ff2):
    B, L, D = x.shape
    assert L % (2 * CHUNK) == 0 and L % ROW_TILE == 0 and w_in.shape[0] == 1
    topk = min(TOPK_MAX, L // 4)
    nb = L // CHUNK + 1
    bf = jnp.bfloat16
    f32 = jnp.float32

    rw = RET_HEADS * RET_DK
    dw = DSA_HEADS * DSA_DH
    sizes = (rw, rw, rw, rw, dw, DSA_DH, DSA_DH, IDX_HEADS * DSA_DH, DSA_DH, IDX_HEADS)
    offs = np.concatenate([[0], np.cumsum(sizes)])
    col = lambda i: w_in[0][:, int(offs[i]):int(offs[i + 1])]
    wr = jnp.concatenate([col(0), col(1), col(2), col(3)], axis=1).astype(bf)
    wq = col(4).reshape(D, DSA_HEADS, DSA_DH)
    wiq = col(7).reshape(D, IDX_HEADS, DSA_DH)
    wqi = jnp.concatenate([wq, wiq], axis=2).reshape(D, DSA_HEADS * 128).astype(bf)
    wk = jnp.concatenate([col(5), col(8)], axis=1).astype(bf)
    wv = jnp.concatenate([col(6), jnp.zeros((D, 128 - DSA_DH), f32)], axis=1).astype(bf)
    ww = jnp.concatenate([col(9), jnp.zeros((D, 128 - IDX_HEADS), f32)], axis=1).astype(bf)

    idx_scale = jnp.full((DSA_DH,), DSA_DH ** -0.5, f32)
    qscale = jnp.concatenate([q_norm_w[0].astype(f32) * (DSA_DH ** -0.5 * LOG2E), idx_scale])[None]
    kscale = jnp.concatenate([k_norm_w[0].astype(f32), jnp.ones((DSA_DH,), f32)])[None]
    meta_pad = jnp.concatenate([jnp.zeros((PAD, D), x.dtype), meta_tokens.astype(x.dtype)], axis=0)

    cos2, sin2 = _rope_tables(nb)
    bias_tbl = _bias_tables(rel_bias)
    tri = jnp.asarray(np.tril(np.ones((2 * CHUNK, 2 * CHUNK), np.float32), k=-1), dtype=bf)

    r_all, qit, k1, k2, vt, iwt = _inproj(x, meta_pad, norm1_w[0][None].astype(f32),
                                          wr, wqi, wk, wv, ww, qscale, kscale)
    y_ret = _retention(r_all, nb, cos2, sin2, ret_norm_w[0][None].astype(f32))
    y_dsa = _sparse_attention(qit, iwt, k1, k2, vt, bias_tbl, tri, topk, nb)

    wo = w_out[0].astype(bf)
    out = _out_mlp(x.reshape(B * L, D), y_ret.reshape(B * L, rw), y_dsa.reshape(B * L, dw),
                   wo[:rw], wo[rw:], norm2_w[0][None].astype(f32),
                   w_ff1[0].astype(bf), w_ff2[0].astype(bf))
    return out.reshape(B, L, D)
```

```python
import functools
import math

import numpy as np
import jax
import jax.numpy as jnp
from jax import lax
from jax.experimental import pallas as pl
from jax.experimental.pallas import tpu as pltpu

N_META = 16
CHUNK = 128
RET_HEADS = 4
RET_DK = 128
DSA_HEADS = 8
DSA_DH = 64
IDX_HEADS = 8
TOPK_MAX = 256
N_BUCKETS = 32
MAX_DISTANCE = 128
ROPE_BASE = 10000.0
EPS = 1e-6
NEG = -1e30
PAD = CHUNK - N_META
HALF = 64
INT_MIN = -(2 ** 31)
LOG2E = math.log2(math.e)

FFN_CHUNK = 1024
ROW_TILE = 512
VMEM_LIMIT = 56 * 1024 * 1024


def _dot(a, b):
    return jnp.dot(a, b, preferred_element_type=jnp.float32)


def _dot_nt(a, b):
    return lax.dot_general(a, b, (((1,), (1,)), ((), ())), preferred_element_type=jnp.float32)


def _dot_tn(a, b):
    return lax.dot_general(a, b, (((0,), (0,)), ((), ())), preferred_element_type=jnp.float32)


def _bucket_ranges():
    max_exact = N_BUCKETS // 2
    d = np.arange(0, 2 * CHUNK)
    large = max_exact + (np.log(np.maximum(d, 1) / max_exact) / math.log(MAX_DISTANCE / max_exact)
                         * (N_BUCKETS - max_exact)).astype(np.int64)
    bucket = np.where(d < max_exact, d, np.minimum(large, N_BUCKETS - 1))
    out = []
    for b in range(N_BUCKETS - 1):
        idx = np.nonzero(bucket == b)[0]
        out.append((int(idx.min()), int(idx.max())))
    return out


def _rope_kernel(cos_ref, sin_ref):
    n = pl.program_id(0)
    row = lax.broadcasted_iota(jnp.int32, (CHUNK, 128), 0)
    lane = lax.broadcasted_iota(jnp.int32, (CHUNK, 128), 1)
    pos = (n * CHUNK + row - PAD).astype(jnp.float32)
    frac = (lane % HALF).astype(jnp.float32) / HALF
    inv = jnp.exp(-frac * math.log(ROPE_BASE))
    ang = pos * inv
    cos_ref[...] = jnp.cos(ang)
    s = jnp.sin(ang)
    sin_ref[...] = jnp.where(lane < HALF, -s, s)


def _rope_tables(nb):
    return pl.pallas_call(
        _rope_kernel,
        grid=(nb,),
        out_specs=[pl.BlockSpec((CHUNK, 128), lambda n: (n, 0))] * 2,
        out_shape=[jax.ShapeDtypeStruct((nb * CHUNK, 128), jnp.float32)] * 2,
        name="rope_tables",
    )()


def _bias_kernel(rb_ref, out_ref):
    row = lax.broadcasted_iota(jnp.int32, (CHUNK, CHUNK), 0)
    col = lax.broadcasted_iota(jnp.int32, (CHUNK, CHUNK), 1)
    ranges = _bucket_ranges()
    out_ref[2] = jnp.zeros(out_ref.shape[1:], jnp.float32)
    for h in range(DSA_HEADS):
        far = rb_ref[N_BUCKETS - 1, h]
        for t in range(2):
            dist = col - row + t * CHUNK
            tile = jnp.zeros((CHUNK, CHUNK), jnp.float32)
            for b, (lo, hi) in enumerate(ranges):
                tile = jnp.where((dist >= lo) & (dist <= hi), (rb_ref[b, h] - far) * LOG2E, tile)
            out_ref[t, :, h * 128:(h + 1) * 128] = tile


def _bias_tables(rel_bias):
    return pl.pallas_call(
        _bias_kernel,
        in_specs=[pl.BlockSpec(memory_space=pltpu.SMEM)],
        out_specs=pl.BlockSpec(memory_space=pltpu.VMEM),
        out_shape=jax.ShapeDtypeStruct((3, CHUNK, DSA_HEADS * 128), jnp.float32),
        name="bias_tables",
    )(rel_bias.astype(jnp.float32))


def _inproj_kernel(xa_ref, xb_ref, meta_ref, n1_ref, wr_ref, wqi_ref, wk_ref, wv_ref, ww_ref,
                   qs_ref, ks_ref,
                   r_ref, qit_ref, k1_ref, k2_ref, vt_ref, iwt_ref, *, pad_last):
    s = pl.program_id(1)
    top = jnp.where(s == 0, meta_ref[...], xa_ref[0])
    bot = xb_ref[0]
    if pad_last:
        bot = jnp.where(s == pl.num_programs(1) - 1, 0.0, bot)
    src = jnp.concatenate([top, bot], axis=0)
    ms = jnp.mean(src * src, axis=-1, keepdims=True)
    u = (src * lax.rsqrt(ms + EPS) * n1_ref[...]).astype(jnp.bfloat16)

    r_ref[0] = _dot(u, wr_ref[...])

    lane = lax.broadcasted_iota(jnp.int32, (2 * CHUNK, 128), 1)
    lo = lane < HALF

    def head_rms(t):
        ssq = jnp.sum(jnp.where(lo, t * t, 0.0), axis=-1, keepdims=True)
        return lax.rsqrt(ssq / DSA_DH + EPS)

    pq = _dot(u, wqi_ref[...])
    for h in range(DSA_HEADS):
        t = pq[:, h * 128:(h + 1) * 128]
        t = t * (jnp.where(lo, head_rms(t), 1.0) * qs_ref[...])
        for j in range(2):
            qit_ref[0, j, :, h * 128:(h + 1) * 128] = (
                t[j * CHUNK:(j + 1) * CHUNK].T.astype(jnp.bfloat16))

    pk = _dot(u, wk_ref[...])
    kn = pk * head_rms(pk) * ks_ref[...]
    k1_ref[0] = jnp.where(lo, kn, 0.0).astype(jnp.bfloat16)
    k2_ref[0] = jnp.where(lo, 0.0, pk).astype(jnp.bfloat16)

    pv = _dot(u, wv_ref[...])
    vx = jnp.where(lo, pv, jnp.where(lane == HALF, 1.0, 0.0))
    iw = _dot(u, ww_ref[...]) * (IDX_HEADS ** -0.5)
    for j in range(2):
        rows = slice(j * CHUNK, (j + 1) * CHUNK)
        vt_ref[0, :, rows] = vx[rows].T.astype(jnp.bfloat16)
        iwt_ref[0, j] = iw[rows].T[0:IDX_HEADS, :]


def _inproj(x, meta_pad, n1, wr, wqi, wk, wv, ww, qscale, kscale):
    B, L, D = x.shape
    nx = L // CHUNK
    nbp = nx + 1 + ((nx + 1) % 2)
    tp = nbp * CHUNK
    const = lambda shape: pl.BlockSpec(shape, lambda b, s: (0,) * len(shape))
    blk = lambda w: pl.BlockSpec((1, 2 * CHUNK, w), lambda b, s: (b, s, 0))
    return pl.pallas_call(
        functools.partial(_inproj_kernel, pad_last=bool((nx + 1) % 2)),
        grid=(B, nbp // 2),
        in_specs=[
            pl.BlockSpec((1, CHUNK, D), lambda b, s: (b, jnp.maximum(2 * s - 1, 0), 0)),
            pl.BlockSpec((1, CHUNK, D), lambda b, s: (b, jnp.minimum(2 * s, nx - 1), 0)),
            const(meta_pad.shape), const(n1.shape), const(wr.shape), const(wqi.shape),
            const(wk.shape), const(wv.shape), const(ww.shape), const(qscale.shape),
            const(kscale.shape),
        ],
        out_specs=[
            blk(wr.shape[1]),
            pl.BlockSpec((1, 2, 128, DSA_HEADS * 128), lambda b, s: (b, s, 0, 0)),
            blk(128), blk(128),
            pl.BlockSpec((1, 128, 2 * CHUNK), lambda b, s: (b, 0, s)),
            pl.BlockSpec((1, 2, IDX_HEADS, CHUNK), lambda b, s: (b, s, 0, 0)),
        ],
        out_shape=[
            jax.ShapeDtypeStruct((B, tp, wr.shape[1]), jnp.float32),
            jax.ShapeDtypeStruct((B, nbp, 128, DSA_HEADS * 128), jnp.bfloat16),
            jax.ShapeDtypeStruct((B, tp, 128), jnp.bfloat16),
            jax.ShapeDtypeStruct((B, tp, 128), jnp.bfloat16),
            jax.ShapeDtypeStruct((B, 128, tp), jnp.bfloat16),
            jax.ShapeDtypeStruct((B, nbp, IDX_HEADS, CHUNK), jnp.float32),
        ],
        compiler_params=pltpu.CompilerParams(
            dimension_semantics=("arbitrary", "arbitrary"), vmem_limit_bytes=VMEM_LIMIT),
        name="in_projection",
    )(x, x, meta_pad, n1, wr, wqi, wk, wv, ww, qscale, kscale)


def _retention_kernel(ra_ref, rb_ref, cosa_ref, sina_ref, cosb_ref, sinb_ref, gain_ref, y_ref,
                      state_ref, tbl_ref):
    s = pl.program_id(1)
    w = RET_HEADS * RET_DK
    bf = jnp.bfloat16

    @pl.when(s == 0)
    def _():
        state_ref[...] = jnp.zeros_like(state_ref)
        row = lax.broadcasted_iota(jnp.int32, (CHUNK, CHUNK), 0)
        col = lax.broadcasted_iota(jnp.int32, (CHUNK, CHUNK), 1)
        rowf = row.astype(jnp.float32)
        diff = (row - col).astype(jnp.float32)
        for h in range(RET_HEADS):
            log_gamma = math.log(1.0 - 2.0 ** (-5.0 - h))
            tbl_ref[0, h] = jnp.where(diff >= 0, jnp.exp(log_gamma * jnp.maximum(diff, 0.0)), 0.0)
            tbl_ref[1, h] = jnp.exp(log_gamma * (rowf + 1.0))
            tbl_ref[2, h] = jnp.exp(log_gamma * (CHUNK - 1.0 - rowf))

    live = jnp.where(s == 0, 0.0, 1.0)
    chunks = ((ra_ref, cosa_ref, sina_ref, live), (rb_ref, cosb_ref, sinb_ref, None))
    for h in range(RET_HEADS):
        log_gamma = math.log(1.0 - 2.0 ** (-5.0 - h))
        sl = slice(h * 128, (h + 1) * 128)
        state = state_ref[h]
        for j, (r_ref, cos_ref, sin_ref, scale) in enumerate(chunks):
            cos2, sin2 = cos_ref[...], sin_ref[...]
            q = r_ref[0, :, sl]
            k = r_ref[0, :, w + h * 128: w + (h + 1) * 128]
            v = r_ref[0, :, 2 * w + h * 128: 2 * w + (h + 1) * 128]
            g = r_ref[0, :, 3 * w + h * 128: 3 * w + (h + 1) * 128]
            if scale is not None:
                k = k * scale
            v = v.astype(bf)
            qr = q * cos2 + pltpu.roll(q, HALF, 1) * sin2
            kr = (k * cos2 + pltpu.roll(k, HALF, 1) * sin2) * (RET_DK ** -0.5)

            scores = _dot_nt(qr.astype(bf), kr.astype(bf)) * tbl_ref[0, h]
            o = _dot(scores.astype(bf), v)
            o = o + _dot((qr * tbl_ref[1, h]).astype(bf), state.astype(bf))
            kv = _dot_tn((kr * tbl_ref[2, h]).astype(bf), v)
            state = state * math.exp(log_gamma * CHUNK) + kv

            ms = jnp.mean(o * o, axis=-1, keepdims=True)
            on = o * lax.rsqrt(ms + EPS) * gain_ref[:, sl]
            gate = g * (1.0 / (1.0 + jnp.exp(-g)))
            y_ref[0, j * CHUNK:(j + 1) * CHUNK, sl] = (gate * on).astype(y_ref.dtype)
        state_ref[h] = state


def _retention(r_all, nb, cos2, sin2, gain):
    B = r_all.shape[0]
    w = RET_HEADS * RET_DK
    nstep = (nb + 1) // 2
    rspec = lambda f: pl.BlockSpec((1, CHUNK, 4 * w), lambda b, s: (b, f(s), 0))
    tspec = lambda f: pl.BlockSpec((CHUNK, 128), lambda b, s: (f(s), 0))
    first = lambda s: jnp.maximum(2 * s - 1, 0)
    second = lambda s: 2 * s
    return pl.pallas_call(
        _retention_kernel,
        grid=(B, nstep),
        in_specs=[rspec(first), rspec(second), tspec(first), tspec(first), tspec(second),
                  tspec(second), pl.BlockSpec((1, w), lambda b, s: (0, 0))],
        out_specs=pl.BlockSpec((1, 2 * CHUNK, w), lambda b, s: (b, jnp.maximum(s - 1, 0), 0)),
        out_shape=jax.ShapeDtypeStruct((B, (nb - 1) * CHUNK, w), jnp.bfloat16),
        scratch_shapes=[pltpu.VMEM((RET_HEADS, RET_DK, RET_DK), jnp.float32),
                        pltpu.VMEM((3, RET_HEADS, CHUNK, CHUNK), jnp.float32)],
        compiler_params=pltpu.CompilerParams(
            dimension_semantics=("arbitrary", "arbitrary"), vmem_limit_bytes=VMEM_LIMIT),
        name="retention",
    )(r_all, r_all, cos2, sin2, cos2, sin2, gain)


def _bit_planes(tile):
    a = [tile[r * 8:(r + 1) * 8, :] for r in range(32)]
    j, m = 16, 0x0000FFFF
    while j:
        for k in range(32):
            if (k & j) == 0:
                t = (a[k] ^ lax.shift_right_logical(a[k + j], j)) & m
                a[k] = a[k] ^ t
                a[k + j] = a[k + j] ^ (t << j)
        j >>= 1
        if j:
            m = m ^ ((m << j) & 0xFFFFFFFF)
    return a


def _sublane_total(x):
    x = x + pltpu.roll(x, 4, 0)
    x = x + pltpu.roll(x, 2, 0)
    return x + pltpu.roll(x, 1, 0)


def _dsa_kernel(qa_ref, qb_ref, iwt_ref, k1_ref, k2_ref, vt_ref, bias_ref, tri_ref, y_ref,
                key_ref, plane_ref, thr_ref, ntie_ref, flag_ref, m_ref, acc_ref, sa_ref, sb_ref,
                *, topk):
    t = pl.program_id(1)
    last_q = pl.num_programs(1) - 1
    bf = jnp.bfloat16
    H = DSA_HEADS
    KP = 2 * CHUNK
    G = plane_ref.shape[1]
    cur, nxt = t % 2, (t + 1) % 2

    @pl.when((pl.program_id(0) == 0) & (t == 0))
    def _():
        plane_ref[...] = jnp.zeros_like(plane_ref)

    @pl.when(t == 0)
    def _():
        thr_ref[0] = jnp.zeros(thr_ref.shape[1:], jnp.int32)
        ntie_ref[0] = jnp.zeros(ntie_ref.shape[1:], jnp.float32)
        flag_ref[0] = 0

    row = lax.broadcasted_iota(jnp.int32, (KP, CHUNK), 0)
    col = lax.broadcasted_iota(jnp.int32, (KP, CHUNK), 1)
    hs = lambda h: slice(h * 128, (h + 1) * 128)
    hp = lambda c: slice(c * 256, (c + 1) * 256)
    nsteps = lambda n: (n + 2) // 2

    def key_rows(g):
        last = k1_ref.shape[1] // KP - 1
        return pl.ds(pl.multiple_of(jnp.minimum(g, last) * KP, KP), KP)

    nb_ = t + 1
    steps_b = jnp.where(t < last_q, nsteps(nb_), 0)
    iters_b, tail_b = steps_b // 2, steps_b % 2
    qb_idx = nb_ * CHUNK + col
    iw_rows = [jnp.broadcast_to(iwt_ref[0, 0, h:h + 1, :], (KP, CHUNK)) for h in range(IDX_HEADS)]

    def score_pair(g):
        kblk = k2_ref[0, key_rows(g), :]
        sc = None
        for c in range(H // 2):
            s = _dot(kblk, qb_ref[0, 0, :, hp(c)])
            for hh in range(2):
                term = iw_rows[2 * c + hh] * jnp.maximum(s[:, hs(hh)], 0.0)
                sc = term if sc is None else sc + term
        k_idx = g * KP + row
        valid = (k_idx <= qb_idx) & (k_idx >= PAD)
        sc = jnp.where(valid, sc, NEG)
        bits = lax.bitcast_convert_type(sc, jnp.int32)
        key = jnp.where(bits < 0, jnp.int32(INT_MIN) - bits, bits)
        key_ref[nxt, g] = key
        planes = _bit_planes(key ^ jnp.int32(INT_MIN))
        for p in range(32):
            plane_ref[p, g] = planes[p]

    def score_two(i):
        score_pair(2 * i)
        score_pair(2 * i + 1)

    na_ = t
    steps_a = jnp.where(t > 0, nsteps(na_), 0)
    iters_a, tail_a = steps_a // 2, steps_a % 2
    far_a = jnp.where(t > 0, ((na_ - 1) // 2) // 2, 0)
    neg_key = -(int(np.float32(NEG).view(np.int32)) & 0x7FFFFFFF)
    rep = lambda x: jnp.tile(x, (KP // 8, 1))
    thr = thr_ref[cur]
    n_tie = ntie_ref[cur]
    masked_thr = thr == neg_key
    thr_b = rep(thr)
    thr_sel_b = rep(jnp.where(masked_thr, jnp.int32(neg_key + 1), thr))
    n_tie_b = rep(n_tie)
    has_ties = flag_ref[cur] > 0
    va = acc_ref.shape[0]

    @pl.when(t > 0)
    def _():
        m_ref[...] = jnp.full_like(m_ref, NEG)
        acc_ref[...] = jnp.zeros_like(acc_ref)

    def logits(g, dst_ref):
        dst_ref[...] = _dot(k1_ref[0, key_rows(g), :], qa_ref[0, 0])

    def attend(g, s_ref, ties_before, near, ties):
        key = key_ref[cur, g]
        if ties:
            eq = key == thr_b
            eqf = jnp.where(eq, 1.0, 0.0)
            before = ties_before + _dot(tri_ref[...], eqf.astype(bf))
            sel = (key > thr_b) | (eq & (before < n_tie_b))
            ties_before = ties_before + jnp.sum(eqf, axis=0, keepdims=True)
        else:
            sel = key >= thr_sel_b
        ps, alphas = [], []
        for h in range(H):
            s = s_ref[:, hs(h)]
            if near:
                ta = jnp.clip(na_ - 2 * g, 0, 2)
                tb = jnp.clip(na_ - 2 * g - 1, 0, 2)
                s = s + jnp.concatenate([bias_ref[ta, :, hs(h)], bias_ref[tb, :, hs(h)]], axis=0)
            s = jnp.where(sel, s, NEG)
            m_prev = m_ref[h:h + 1, :]
            m_new = jnp.maximum(m_prev, jnp.max(s, axis=0, keepdims=True))
            m_ref[h:h + 1, :] = m_new
            alphas.append(jnp.exp2(m_prev - m_new))
            ps.append(jnp.exp2(s - m_new).astype(bf))
        pv = _dot(vt_ref[0, 0:va, key_rows(g)], jnp.concatenate(ps, axis=1))
        acc_ref[...] = acc_ref[...] * jnp.concatenate(alphas, axis=1) + pv
        return ties_before

    def attend_two(i, ties_before, near, ties, fused):
        logits(2 * i + 1, sb_ref)
        ties_before = attend(2 * i, sa_ref, ties_before, near, ties)
        logits(2 * i + 2, sa_ref)
        ties_before = attend(2 * i + 1, sb_ref, ties_before, near, ties)
        if fused:
            score_two(i)
        return ties_before

    zero_ties = jnp.zeros((KP, CHUNK), jnp.float32)
    loop = lambda lo, hi, **kw: (lambda c: lax.fori_loop(lo, hi, functools.partial(attend_two, **kw), c))
    fused_hi = jnp.minimum(iters_a, iters_b)

    @pl.when(has_ties)
    def _():
        logits(0, sa_ref)
        c = loop(0, far_a, near=False, ties=True, fused=False)(zero_ties)
        c = loop(far_a, iters_a, near=True, ties=True, fused=False)(c)

        @pl.when(tail_a == 1)
        def _():
            attend(2 * iters_a, sa_ref, c, True, True)

    @pl.when(jnp.logical_not(has_ties))
    def _():
        logits(0, sa_ref)
        far_hi = jnp.minimum(far_a, fused_hi)
        loop(0, far_hi, near=False, ties=False, fused=True)(zero_ties)
        loop(far_hi, fused_hi, near=True, ties=False, fused=True)(zero_ties)
        loop(fused_hi, iters_a, near=True, ties=False, fused=False)(zero_ties)

        @pl.when(tail_a == 1)
        def _():
            attend(2 * iters_a, sa_ref, zero_ties, True, False)

    lax.fori_loop(jnp.where(has_ties, 0, fused_hi), iters_b, lambda i, _: (score_two(i), 0)[1], 0)

    @pl.when(tail_b == 1)
    def _():
        score_pair(2 * iters_b)

    @pl.when(t > 0)
    def _():
        lo = lax.broadcasted_iota(jnp.int32, (CHUNK, CHUNK), 1) < HALF
        for c in range(H // 2):
            tiles = []
            for h in (2 * c, 2 * c + 1):
                a = acc_ref[:, hs(h)] * (1.0 / acc_ref[HALF:HALF + 1, hs(h)])
                a = jnp.concatenate([a, jnp.zeros((CHUNK - va, CHUNK), jnp.float32)], axis=0)
                tiles.append(a.T)
            y_ref[0, :, hs(c)] = jnp.where(lo, tiles[0], pltpu.roll(tiles[1], HALF, 1)).astype(y_ref.dtype)

    @pl.when(t < last_q)
    def _():
        npair_b = (nb_ + 2) // 2
        g_idx = lax.broadcasted_iota(jnp.int32, (G, 8, CHUNK), 0)
        alive0 = jnp.where(g_idx < npair_b, jnp.int32(-1), jnp.int32(0))

        def ones_in(words):
            part = lax.population_count(words).astype(jnp.float32).sum(axis=0)
            return _sublane_total(part)

        def radix(p, carry):
            alive, above, thr_u, cnt = carry
            take1 = (above + cnt) >= topk
            above = jnp.where(take1, above, above + cnt)
            thr_u = jnp.where(take1, thr_u | lax.shift_right_logical(jnp.int32(INT_MIN), p), thr_u)
            drop = jnp.where(take1, jnp.int32(0), jnp.int32(-1))
            alive = alive & (plane_ref[p] ^ drop[None])
            cnt = ones_in(alive & plane_ref[jnp.minimum(p + 1, 31)])
            return alive, above, thr_u, cnt

        zero8 = jnp.zeros((8, CHUNK), jnp.float32)
        alive, n_gt, thr_u, _ = lax.fori_loop(
            0, 32, radix,
            (alive0, zero8, jnp.zeros((8, CHUNK), jnp.int32), ones_in(alive0 & plane_ref[0])))
        thr_n = thr_u ^ jnp.int32(INT_MIN)
        n_eq = ones_in(alive)
        masked_n = thr_n == neg_key
        n_tie_n = jnp.where(masked_n, 0.0, topk - n_gt)
        thr_ref[nxt] = thr_n
        ntie_ref[nxt] = n_tie_n
        more = jnp.max(jnp.where((n_eq > n_tie_n) & ~masked_n, 1.0, 0.0)) > 0.5
        flag_ref[nxt] = more.astype(jnp.int32)


def _sparse_attention(qit, iwt, k1, k2, vt, bias_tbl, tri, topk, nb):
    B, nbp, _, hw = qit.shape
    tp = nbp * CHUNK
    nstep = nbp // 2
    nstep += nstep % 2
    kspec = pl.BlockSpec((1, tp, 128), lambda b, t: (b, 0, 0))
    acc_rows = 80
    nxt = lambda t: jnp.minimum(t + 1, nb - 1)
    return pl.pallas_call(
        functools.partial(_dsa_kernel, topk=topk),
        grid=(B, nb),
        in_specs=[
            pl.BlockSpec((1, 1, 128, hw), lambda b, t: (b, t, 0, 0)),
            pl.BlockSpec((1, 1, 128, hw), lambda b, t: (b, nxt(t), 0, 0)),
            pl.BlockSpec((1, 1, IDX_HEADS, CHUNK), lambda b, t: (b, nxt(t), 0, 0)),
            kspec, kspec,
            pl.BlockSpec((1, 128, tp), lambda b, t: (b, 0, 0)),
            pl.BlockSpec(bias_tbl.shape, lambda b, t: (0, 0, 0)),
            pl.BlockSpec(tri.shape, lambda b, t: (0, 0)),
        ],
        out_specs=pl.BlockSpec((1, CHUNK, DSA_HEADS * DSA_DH),
                               lambda b, t: (b, jnp.maximum(t - 1, 0), 0)),
        out_shape=jax.ShapeDtypeStruct((B, (nb - 1) * CHUNK, DSA_HEADS * DSA_DH), jnp.bfloat16),
        scratch_shapes=[
            pltpu.VMEM((2, nstep, 2 * CHUNK, CHUNK), jnp.int32),
            pltpu.VMEM((32, nstep, 8, CHUNK), jnp.int32),
            pltpu.VMEM((2, 8, CHUNK), jnp.int32),
            pltpu.VMEM((2, 8, CHUNK), jnp.float32),
            pltpu.SMEM((2,), jnp.int32),
            pltpu.VMEM((DSA_HEADS, CHUNK), jnp.float32),
            pltpu.VMEM((acc_rows, hw), jnp.float32),
            pltpu.VMEM((2 * CHUNK, hw), jnp.float32),
            pltpu.VMEM((2 * CHUNK, hw), jnp.float32),
        ],
        compiler_params=pltpu.CompilerParams(
            dimension_semantics=("arbitrary", "arbitrary"), vmem_limit_bytes=VMEM_LIMIT),
        name="sparse_attention",
    )(qit, qit, iwt, k1, k2, vt, bias_tbl, tri)


def _mlp_kernel(x_ref, yr_ref, yd_ref, wor_ref, wod_ref, n2_ref, w1_ref, w2_ref, o_ref):
    h1 = x_ref[...] + _dot(yr_ref[...], wor_ref[...]) + _dot(yd_ref[...], wod_ref[...])
    ms = jnp.mean(h1 * h1, axis=-1, keepdims=True)
    u = (h1 * lax.rsqrt(ms + EPS) * n2_ref[...]).astype(jnp.bfloat16)
    o_ref[...] = h1
    d_ff = w1_ref.shape[1]
    for c in range(d_ff // FFN_CHUNK):
        sl = slice(c * FFN_CHUNK, (c + 1) * FFN_CHUNK)
        f = jnp.maximum(_dot(u, w1_ref[:, sl]), 0.0)
        o_ref[...] += _dot((f * f).astype(jnp.bfloat16), w2_ref[sl, :])


def _out_mlp(x2, yr, yd, wor, wod, n2, w1, w2):
    rows, D = x2.shape
    const = lambda a: pl.BlockSpec(a.shape, lambda i: (0, 0), pipeline_mode=pl.Buffered(1))
    tile = lambda w: pl.BlockSpec((ROW_TILE, w), lambda i: (i, 0))
    return pl.pallas_call(
        _mlp_kernel,
        grid=(rows // ROW_TILE,),
        in_specs=[tile(D), tile(yr.shape[1]), tile(yd.shape[1]),
                  const(wor), const(wod), const(n2), const(w1), const(w2)],
        out_specs=tile(D),
        out_shape=jax.ShapeDtypeStruct((rows, D), jnp.float32),
        compiler_params=pltpu.CompilerParams(
            dimension_semantics=("arbitrary",), vmem_limit_bytes=VMEM_LIMIT),
        name="out_mlp",
    )(x2, yr, yd, wor, wod, n2, w1, w2)


def kernel(x, meta_tokens, norm1_w, w_in, ret_norm_w, q_norm_w, k_norm_w, rel_bias,
           w_out, norm2_w, w_ff1, w_ff2):
    B, L, D = x.shape
    assert L % (2 * CHUNK) == 0 and L % ROW_TILE == 0 and w_in.shape[0] == 1
    topk = min(TOPK_MAX, L // 4)
    nb = L // CHUNK + 1
    bf = jnp.bfloat16
    f32 = jnp.float32

    rw = RET_HEADS * RET_DK
    dw = DSA_HEADS * DSA_DH
    sizes = (rw, rw, rw, rw, dw, DSA_DH, DSA_DH, IDX_HEADS * DSA_DH, DSA_DH, IDX_HEADS)
    offs = np.concatenate([[0], np.cumsum(sizes)])
    col = lambda i: w_in[0][:, int(offs[i]):int(offs[i + 1])]
    wr = jnp.concatenate([col(0), col(1), col(2), col(3)], axis=1).astype(bf)
    wq = col(4).reshape(D, DSA_HEADS, DSA_DH)
    wiq = col(7).reshape(D, IDX_HEADS, DSA_DH)
    wqi = jnp.concatenate([wq, wiq], axis=2).reshape(D, DSA_HEADS * 128).astype(bf)
    wk = jnp.concatenate([col(5), col(8)], axis=1).astype(bf)
    wv = jnp.concatenate([col(6), jnp.zeros((D, 128 - DSA_DH), f32)], axis=1).astype(bf)
    ww = jnp.concatenate([col(9), jnp.zeros((D, 128 - IDX_HEADS), f32)], axis=1).astype(bf)

    idx_scale = jnp.full((DSA_DH,), DSA_DH ** -0.5, f32)
    qscale = jnp.concatenate([q_norm_w[0].astype(f32) * (DSA_DH ** -0.5 * LOG2E), idx_scale])[None]
    kscale = jnp.concatenate([k_norm_w[0].astype(f32), jnp.ones((DSA_DH,), f32)])[None]
    meta_pad = jnp.concatenate([jnp.zeros((PAD, D), x.dtype), meta_tokens.astype(x.dtype)], axis=0)

    cos2, sin2 = _rope_tables(nb)
    bias_tbl = _bias_tables(rel_bias)
    tri = jnp.asarray(np.tril(np.ones((2 * CHUNK, 2 * CHUNK), np.float32), k=-1), dtype=bf)

    r_all, qit, k1, k2, vt, iwt = _inproj(x, meta_pad, norm1_w[0][None].astype(f32),
                                          wr, wqi, wk, wv, ww, qscale, kscale)
    y_ret = _retention(r_all, nb, cos2, sin2, ret_norm_w[0][None].astype(f32))
    y_dsa = _sparse_attention(qit, iwt, k1, k2, vt, bias_tbl, tri, topk, nb)

    wo = w_out[0].astype(bf)
    out = _out_mlp(x.reshape(B * L, D), y_ret.reshape(B * L, rw), y_dsa.reshape(B * L, dw),
                   wo[:rw], wo[rw:], norm2_w[0][None].astype(f32),
                   w_ff1[0].astype(bf), w_ff2[0].astype(bf))
    return out.reshape(B, L, D)
```

```python
import functools
import math

import numpy as np
import jax
import jax.numpy as jnp
from jax import lax
from jax.experimental import pallas as pl
from jax.experimental.pallas import tpu as pltpu

N_META = 16
CHUNK = 128
RET_HEADS = 4
RET_DK = 128
DSA_HEADS = 8
DSA_DH = 64
IDX_HEADS = 8
TOPK_MAX = 256
N_BUCKETS = 32
MAX_DISTANCE = 128
ROPE_BASE = 10000.0
EPS = 1e-6
NEG = -1e30
PAD = CHUNK - N_META
HALF = 64
INT_MIN = -(2 ** 31)
LOG2E = math.log2(math.e)

FFN_CHUNK = 1024
ROW_TILE = 512
VMEM_LIMIT = 56 * 1024 * 1024


def _dot(a, b):
    return jnp.dot(a, b, preferred_element_type=jnp.float32)


def _dot_nt(a, b):
    return lax.dot_general(a, b, (((1,), (1,)), ((), ())), preferred_element_type=jnp.float32)


def _dot_tn(a, b):
    return lax.dot_general(a, b, (((0,), (0,)), ((), ())), preferred_element_type=jnp.float32)


def _bucket_ranges():
    max_exact = N_BUCKETS // 2
    d = np.arange(0, 2 * CHUNK)
    large = max_exact + (np.log(np.maximum(d, 1) / max_exact) / math.log(MAX_DISTANCE / max_exact)
                         * (N_BUCKETS - max_exact)).astype(np.int64)
    bucket = np.where(d < max_exact, d, np.minimum(large, N_BUCKETS - 1))
    out = []
    for b in range(N_BUCKETS - 1):
        idx = np.nonzero(bucket == b)[0]
        out.append((int(idx.min()), int(idx.max())))
    return out


def _rope_kernel(cos_ref, sin_ref):
    n = pl.program_id(0)
    row = lax.broadcasted_iota(jnp.int32, (CHUNK, 128), 0)
    lane = lax.broadcasted_iota(jnp.int32, (CHUNK, 128), 1)
    pos = (n * CHUNK + row - PAD).astype(jnp.float32)
    frac = (lane % HALF).astype(jnp.float32) / HALF
    inv = jnp.exp(-frac * math.log(ROPE_BASE))
    ang = pos * inv
    cos_ref[...] = jnp.cos(ang)
    s = jnp.sin(ang)
    sin_ref[...] = jnp.where(lane < HALF, -s, s)


def _rope_tables(nb):
    return pl.pallas_call(
        _rope_kernel,
        grid=(nb,),
        out_specs=[pl.BlockSpec((CHUNK, 128), lambda n: (n, 0))] * 2,
        out_shape=[jax.ShapeDtypeStruct((nb * CHUNK, 128), jnp.float32)] * 2,
        name="rope_tables",
    )()


def _bias_kernel(rb_ref, out_ref):
    row = lax.broadcasted_iota(jnp.int32, (CHUNK, CHUNK), 0)
    col = lax.broadcasted_iota(jnp.int32, (CHUNK, CHUNK), 1)
    ranges = _bucket_ranges()
    out_ref[2] = jnp.zeros(out_ref.shape[1:], jnp.float32)
    for h in range(DSA_HEADS):
        far = rb_ref[N_BUCKETS - 1, h]
        for t in range(2):
            dist = col - row + t * CHUNK
            tile = jnp.zeros((CHUNK, CHUNK), jnp.float32)
            for b, (lo, hi) in enumerate(ranges):
                tile = jnp.where((dist >= lo) & (dist <= hi), (rb_ref[b, h] - far) * LOG2E, tile)
            out_ref[t, :, h * 128:(h + 1) * 128] = tile


def _bias_tables(rel_bias):
    return pl.pallas_call(
        _bias_kernel,
        in_specs=[pl.BlockSpec(memory_space=pltpu.SMEM)],
        out_specs=pl.BlockSpec(memory_space=pltpu.VMEM),
        out_shape=jax.ShapeDtypeStruct((3, CHUNK, DSA_HEADS * 128), jnp.float32),
        name="bias_tables",
    )(rel_bias.astype(jnp.float32))


def _inproj_kernel(xa_ref, xb_ref, meta_ref, n1_ref, wr_ref, wqi_ref, wk_ref, wv_ref, ww_ref,
                   qs_ref, ks_ref,
                   r_ref, qit_ref, k1_ref, k2_ref, vt_ref, iwt_ref, *, pad_last):
    s = pl.program_id(1)
    top = jnp.where(s == 0, meta_ref[...], xa_ref[0])
    bot = xb_ref[0]
    if pad_last:
        bot = jnp.where(s == pl.num_programs(1) - 1, 0.0, bot)
    src = jnp.concatenate([top, bot], axis=0)
    ms = jnp.mean(src * src, axis=-1, keepdims=True)
    u = (src * lax.rsqrt(ms + EPS) * n1_ref[...]).astype(jnp.bfloat16)

    r_ref[0] = _dot(u, wr_ref[...])

    lane = lax.broadcasted_iota(jnp.int32, (2 * CHUNK, 128), 1)
    lo = lane < HALF

    def head_rms(t):
        ssq = jnp.sum(jnp.where(lo, t * t, 0.0), axis=-1, keepdims=True)
        return lax.rsqrt(ssq / DSA_DH + EPS)

    pq = _dot(u, wqi_ref[...])
    for h in range(DSA_HEADS):
        t = pq[:, h * 128:(h + 1) * 128]
        t = t * (jnp.where(lo, head_rms(t), 1.0) * qs_ref[...])
        for j in range(2):
            qit_ref[0, j, :, h * 128:(h + 1) * 128] = (
                t[j * CHUNK:(j + 1) * CHUNK].T.astype(jnp.bfloat16))

    pk = _dot(u, wk_ref[...])
    kn = pk * head_rms(pk) * ks_ref[...]
    k1_ref[0] = jnp.where(lo, kn, 0.0).astype(jnp.bfloat16)
    k2_ref[0] = jnp.where(lo, 0.0, pk).astype(jnp.bfloat16)

    pv = _dot(u, wv_ref[...])
    vx = jnp.where(lo, pv, jnp.where(lane == HALF, 1.0, 0.0))
    iw = _dot(u, ww_ref[...]) * (IDX_HEADS ** -0.5)
    for j in range(2):
        rows = slice(j * CHUNK, (j + 1) * CHUNK)
        vt_ref[0, :, rows] = vx[rows].T.astype(jnp.bfloat16)
        iwt_ref[0, j] = iw[rows].T[0:IDX_HEADS, :]


def _inproj(x, meta_pad, n1, wr, wqi, wk, wv, ww, qscale, kscale):
    B, L, D = x.shape
    nx = L // CHUNK
    nbp = nx + 1 + ((nx + 1) % 2)
    tp = nbp * CHUNK
    const = lambda shape: pl.BlockSpec(shape, lambda b, s: (0,) * len(shape))
    blk = lambda w: pl.BlockSpec((1, 2 * CHUNK, w), lambda b, s: (b, s, 0))
    return pl.pallas_call(
        functools.partial(_inproj_kernel, pad_last=bool((nx + 1) % 2)),
        grid=(B, nbp // 2),
        in_specs=[
            pl.BlockSpec((1, CHUNK, D), lambda b, s: (b, jnp.maximum(2 * s - 1, 0), 0)),
            pl.BlockSpec((1, CHUNK, D), lambda b, s: (b, jnp.minimum(2 * s, nx - 1), 0)),
            const(meta_pad.shape), const(n1.shape), const(wr.shape), const(wqi.shape),
            const(wk.shape), const(wv.shape), const(ww.shape), const(qscale.shape),
            const(kscale.shape),
        ],
        out_specs=[
            blk(wr.shape[1]),
            pl.BlockSpec((1, 2, 128, DSA_HEADS * 128), lambda b, s: (b, s, 0, 0)),
            blk(128), blk(128),
            pl.BlockSpec((1, 128, 2 * CHUNK), lambda b, s: (b, 0, s)),
            pl.BlockSpec((1, 2, IDX_HEADS, CHUNK), lambda b, s: (b, s, 0, 0)),
        ],
        out_shape=[
            jax.ShapeDtypeStruct((B, tp, wr.shape[1]), jnp.float32),
            jax.ShapeDtypeStruct((B, nbp, 128, DSA_HEADS * 128), jnp.bfloat16),
            jax.ShapeDtypeStruct((B, tp, 128), jnp.bfloat16),
            jax.ShapeDtypeStruct((B, tp, 128), jnp.bfloat16),
            jax.ShapeDtypeStruct((B, 128, tp), jnp.bfloat16),
            jax.ShapeDtypeStruct((B, nbp, IDX_HEADS, CHUNK), jnp.float32),
        ],
        compiler_params=pltpu.CompilerParams(
            dimension_semantics=("arbitrary", "arbitrary"), vmem_limit_bytes=VMEM_LIMIT),
        name="in_projection",
    )(x, x, meta_pad, n1, wr, wqi, wk, wv, ww, qscale, kscale)


def _retention_kernel(ra_ref, rb_ref, cosa_ref, sina_ref, cosb_ref, sinb_ref, gain_ref, y_ref,
                      state_ref, tbl_ref):
    s = pl.program_id(1)
    w = RET_HEADS * RET_DK
    bf = jnp.bfloat16

    @pl.when(s == 0)
    def _():
        state_ref[...] = jnp.zeros_like(state_ref)
        row = lax.broadcasted_iota(jnp.int32, (CHUNK, CHUNK), 0)
        col = lax.broadcasted_iota(jnp.int32, (CHUNK, CHUNK), 1)
        rowf = row.astype(jnp.float32)
        diff = (row - col).astype(jnp.float32)
        for h in range(RET_HEADS):
            log_gamma = math.log(1.0 - 2.0 ** (-5.0 - h))
            tbl_ref[0, h] = jnp.where(diff >= 0, jnp.exp(log_gamma * jnp.maximum(diff, 0.0)), 0.0)
            tbl_ref[1, h] = jnp.exp(log_gamma * (rowf + 1.0))
            tbl_ref[2, h] = jnp.exp(log_gamma * (CHUNK - 1.0 - rowf))

    live = jnp.where(s == 0, 0.0, 1.0)
    chunks = ((ra_ref, cosa_ref, sina_ref, live), (rb_ref, cosb_ref, sinb_ref, None))
    for h in range(RET_HEADS):
        log_gamma = math.log(1.0 - 2.0 ** (-5.0 - h))
        sl = slice(h * 128, (h + 1) * 128)
        state = state_ref[h]
        for j, (r_ref, cos_ref, sin_ref, scale) in enumerate(chunks):
            cos2, sin2 = cos_ref[...], sin_ref[...]
            q = r_ref[0, :, sl]
            k = r_ref[0, :, w + h * 128: w + (h + 1) * 128]
            v = r_ref[0, :, 2 * w + h * 128: 2 * w + (h + 1) * 128]
            g = r_ref[0, :, 3 * w + h * 128: 3 * w + (h + 1) * 128]
            if scale is not None:
                k = k * scale
            v = v.astype(bf)
            qr = q * cos2 + pltpu.roll(q, HALF, 1) * sin2
            kr = (k * cos2 + pltpu.roll(k, HALF, 1) * sin2) * (RET_DK ** -0.5)

            scores = _dot_nt(qr.astype(bf), kr.astype(bf)) * tbl_ref[0, h]
            o = _dot(scores.astype(bf), v)
            o = o + _dot((qr * tbl_ref[1, h]).astype(bf), state.astype(bf))
            kv = _dot_tn((kr * tbl_ref[2, h]).astype(bf), v)
            state = state * math.exp(log_gamma * CHUNK) + kv

            ms = jnp.mean(o * o, axis=-1, keepdims=True)
            on = o * lax.rsqrt(ms + EPS) * gain_ref[:, sl]
            gate = g * (1.0 / (1.0 + jnp.exp(-g)))
            y_ref[0, j * CHUNK:(j + 1) * CHUNK, sl] = (gate * on).astype(y_ref.dtype)
        state_ref[h] = state


def _retention(r_all, nb, cos2, sin2, gain):
    B = r_all.shape[0]
    w = RET_HEADS * RET_DK
    nstep = (nb + 1) // 2
    rspec = lambda f: pl.BlockSpec((1, CHUNK, 4 * w), lambda b, s: (b, f(s), 0))
    tspec = lambda f: pl.BlockSpec((CHUNK, 128), lambda b, s: (f(s), 0))
    first = lambda s: jnp.maximum(2 * s - 1, 0)
    second = lambda s: 2 * s
    return pl.pallas_call(
        _retention_kernel,
        grid=(B, nstep),
        in_specs=[rspec(first), rspec(second), tspec(first), tspec(first), tspec(second),
                  tspec(second), pl.BlockSpec((1, w), lambda b, s: (0, 0))],
        out_specs=pl.BlockSpec((1, 2 * CHUNK, w), lambda b, s: (b, jnp.maximum(s - 1, 0), 0)),
        out_shape=jax.ShapeDtypeStruct((B, (nb - 1) * CHUNK, w), jnp.bfloat16),
        scratch_shapes=[pltpu.VMEM((RET_HEADS, RET_DK, RET_DK), jnp.float32),
                        pltpu.VMEM((3, RET_HEADS, CHUNK, CHUNK), jnp.float32)],
        compiler_params=pltpu.CompilerParams(
            dimension_semantics=("arbitrary", "arbitrary"), vmem_limit_bytes=VMEM_LIMIT),
        name="retention",
    )(r_all, r_all, cos2, sin2, cos2, sin2, gain)


def _bit_planes(tile):
    a = [tile[r * 8:(r + 1) * 8, :] for r in range(32)]
    j, m = 16, 0x0000FFFF
    while j:
        for k in range(32):
            if (k & j) == 0:
                t = (a[k] ^ lax.shift_right_logical(a[k + j], j)) & m
                a[k] = a[k] ^ t
                a[k + j] = a[k + j] ^ (t << j)
        j >>= 1
        if j:
            m = m ^ ((m << j) & 0xFFFFFFFF)
    return a


def _sublane_total(x):
    x = x + pltpu.roll(x, 4, 0)
    x = x + pltpu.roll(x, 2, 0)
    return x + pltpu.roll(x, 1, 0)


def _dsa_kernel(qa_ref, qb_ref, iwt_ref, k1_ref, k2_ref, vt_ref, bias_ref, tri_ref, y_ref,
                key_ref, plane_ref, thr_ref, ntie_ref, flag_ref, m_ref, acc_ref, sa_ref, sb_ref,
                rhs_ref, *, topk):
    t = pl.program_id(1)
    last_q = pl.num_programs(1) - 1
    bf = jnp.bfloat16
    H = DSA_HEADS
    KP = 2 * CHUNK
    G = plane_ref.shape[1]
    cur, nxt = t % 2, (t + 1) % 2

    @pl.when((pl.program_id(0) == 0) & (t == 0))
    def _():
        plane_ref[...] = jnp.zeros_like(plane_ref)
        key_ref[...] = jnp.zeros_like(key_ref)
        eye = (lax.broadcasted_iota(jnp.int32, (CHUNK, CHUNK), 0)
               == lax.broadcasted_iota(jnp.int32, (CHUNK, CHUNK), 1))
        for h in range(DSA_HEADS):
            rhs_ref[CHUNK:2 * CHUNK, h * 128:(h + 1) * 128] = jnp.where(eye, 1.0, 0.0).astype(bf)

    @pl.when(t == 0)
    def _():
        thr_ref[0] = jnp.zeros(thr_ref.shape[1:], jnp.int32)
        ntie_ref[0] = jnp.zeros(ntie_ref.shape[1:], jnp.float32)
        flag_ref[0] = 0

    row = lax.broadcasted_iota(jnp.int32, (KP, CHUNK), 0)
    col = lax.broadcasted_iota(jnp.int32, (KP, CHUNK), 1)
    hs = lambda h: slice(h * 128, (h + 1) * 128)
    hp = lambda c: slice(c * 256, (c + 1) * 256)
    nsteps = lambda n: (n + 2) // 2

    def key_rows(g):
        last = k1_ref.shape[1] // KP - 1
        return pl.ds(pl.multiple_of(jnp.minimum(g, last) * KP, KP), KP)

    nb_ = t + 1
    steps_b = jnp.where(t < last_q, nsteps(nb_), 0)
    iters_b, tail_b = steps_b // 2, steps_b % 2
    qb_idx = nb_ * CHUNK + col
    iw_rows = [jnp.broadcast_to(iwt_ref[0, 0, h:h + 1, :], (KP, CHUNK)) for h in range(IDX_HEADS)]

    def score_pair(g):
        kblk = k2_ref[0, key_rows(g), :]
        sc = None
        for c in range(H // 2):
            s = _dot(kblk, qb_ref[0, 0, :, hp(c)])
            for hh in range(2):
                term = iw_rows[2 * c + hh] * jnp.maximum(s[:, hs(hh)], 0.0)
                sc = term if sc is None else sc + term
        k_idx = g * KP + row
        valid = (k_idx <= qb_idx) & (k_idx >= PAD)
        sc = jnp.where(valid, sc, NEG)
        bits = lax.bitcast_convert_type(sc, jnp.int32)
        key = jnp.where(bits < 0, jnp.int32(INT_MIN) - bits, bits)
        key_ref[nxt, g] = key
        planes = _bit_planes(key ^ jnp.int32(INT_MIN))
        for p in range(32):
            plane_ref[p, g] = planes[p]

    def score_two(i):
        score_pair(2 * i)
        score_pair(2 * i + 1)

    na_ = t
    steps_a = jnp.where(t > 0, nsteps(na_), 0)
    iters_a, tail_a = steps_a // 2, steps_a % 2
    far_a = jnp.where(t > 0, ((na_ - 1) // 2) // 2, 0)
    neg_key = -(int(np.float32(NEG).view(np.int32)) & 0x7FFFFFFF)
    rep = lambda x: jnp.tile(x, (KP // 8, 1))
    thr = thr_ref[cur]
    n_tie = ntie_ref[cur]
    masked_thr = thr == neg_key
    thr_b = rep(thr)
    thr_sel_b = rep(jnp.where(masked_thr, jnp.int32(neg_key + 1), thr))
    n_tie_b = rep(n_tie)
    has_ties = flag_ref[cur] > 0
    va = acc_ref.shape[0]

    @pl.when(t > 0)
    def _():
        m_ref[...] = jnp.full_like(m_ref, NEG)
        acc_ref[...] = jnp.zeros_like(acc_ref)
        rhs_ref[0:CHUNK, :] = qa_ref[0, 0]

    def logits(g, dst_ref, ties):
        kblk = k1_ref[0, key_rows(g), :]
        if ties:
            dst_ref[...] = _dot(kblk, qa_ref[0, 0])
        else:
            off = jnp.where(key_ref[cur, g] >= thr_sel_b, 0.0, NEG).astype(bf)
            dst_ref[...] = _dot(jnp.concatenate([kblk, off], axis=1), rhs_ref[...])

    def attend(g, s_ref, ties_before, near, ties):
        if ties:
            key = key_ref[cur, g]
            eq = key == thr_b
            eqf = jnp.where(eq, 1.0, 0.0)
            before = ties_before + _dot(tri_ref[...], eqf.astype(bf))
            sel = (key > thr_b) | (eq & (before < n_tie_b))
            ties_before = ties_before + jnp.sum(eqf, axis=0, keepdims=True)
        ps, alphas = [], []
        for h in range(H):
            s = s_ref[:, hs(h)]
            if near:
                ta = jnp.clip(na_ - 2 * g, 0, 2)
                tb = jnp.clip(na_ - 2 * g - 1, 0, 2)
                s = s + jnp.concatenate([bias_ref[ta, :, hs(h)], bias_ref[tb, :, hs(h)]], axis=0)
            if ties:
                s = jnp.where(sel, s, NEG)
            m_prev = m_ref[h:h + 1, :]
            m_new = jnp.maximum(m_prev, jnp.max(s, axis=0, keepdims=True))
            m_ref[h:h + 1, :] = m_new
            alphas.append(jnp.exp2(m_prev - m_new))
            ps.append(jnp.exp2(s - m_new).astype(bf))
        pv = _dot(vt_ref[0, 0:va, key_rows(g)], jnp.concatenate(ps, axis=1))
        acc_ref[...] = acc_ref[...] * jnp.concatenate(alphas, axis=1) + pv
        return ties_before

    def attend_two(i, ties_before, near, ties, fused):
        logits(2 * i + 1, sb_ref, ties)
        ties_before = attend(2 * i, sa_ref, ties_before, near, ties)
        logits(2 * i + 2, sa_ref, ties)
        ties_before = attend(2 * i + 1, sb_ref, ties_before, near, ties)
        if fused:
            score_two(i)
        return ties_before

    zero_ties = jnp.zeros((KP, CHUNK), jnp.float32)
    loop = lambda lo, hi, **kw: (lambda c: lax.fori_loop(lo, hi, functools.partial(attend_two, **kw), c))
    fused_hi = jnp.minimum(iters_a, iters_b)

    @pl.when(has_ties)
    def _():
        logits(0, sa_ref, True)
        c = loop(0, far_a, near=False, ties=True, fused=False)(zero_ties)
        c = loop(far_a, iters_a, near=True, ties=True, fused=False)(c)

        @pl.when(tail_a == 1)
        def _():
            attend(2 * iters_a, sa_ref, c, True, True)

    @pl.when(jnp.logical_not(has_ties))
    def _():
        logits(0, sa_ref, False)
        far_hi = jnp.minimum(far_a, fused_hi)
        loop(0, far_hi, near=False, ties=False, fused=True)(zero_ties)
        loop(far_hi, fused_hi, near=True, ties=False, fused=True)(zero_ties)
        loop(fused_hi, iters_a, near=True, ties=False, fused=False)(zero_ties)

        @pl.when(tail_a == 1)
        def _():
            attend(2 * iters_a, sa_ref, zero_ties, True, False)

    lax.fori_loop(jnp.where(has_ties, 0, fused_hi), iters_b, lambda i, _: (score_two(i), 0)[1], 0)

    @pl.when(tail_b == 1)
    def _():
        score_pair(2 * iters_b)

    @pl.when(t > 0)
    def _():
        lo = lax.broadcasted_iota(jnp.int32, (CHUNK, CHUNK), 1) < HALF
        for c in range(H // 2):
            tiles = []
            for h in (2 * c, 2 * c + 1):
                a = acc_ref[:, hs(h)] * (1.0 / acc_ref[HALF:HALF + 1, hs(h)])
                a = jnp.concatenate([a, jnp.zeros((CHUNK - va, CHUNK), jnp.float32)], axis=0)
                tiles.append(a.T)
            y_ref[0, :, hs(c)] = jnp.where(lo, tiles[0], pltpu.roll(tiles[1], HALF, 1)).astype(y_ref.dtype)

    @pl.when(t < last_q)
    def _():
        npair_b = (nb_ + 2) // 2
        g_idx = lax.broadcasted_iota(jnp.int32, (G, 8, CHUNK), 0)
        alive0 = jnp.where(g_idx < npair_b, jnp.int32(-1), jnp.int32(0))

        def ones_in(words):
            part = lax.population_count(words).astype(jnp.float32).sum(axis=0)
            return _sublane_total(part)

        def radix(p, carry):
            alive, above, thr_u, cnt = carry
            take1 = (above + cnt) >= topk
            above = jnp.where(take1, above, above + cnt)
            thr_u = jnp.where(take1, thr_u | lax.shift_right_logical(jnp.int32(INT_MIN), p), thr_u)
            drop = jnp.where(take1, jnp.int32(0), jnp.int32(-1))
            alive = alive & (plane_ref[p] ^ drop[None])
            cnt = ones_in(alive & plane_ref[jnp.minimum(p + 1, 31)])
            return alive, above, thr_u, cnt

        zero8 = jnp.zeros((8, CHUNK), jnp.float32)
        alive, n_gt, thr_u, _ = lax.fori_loop(
            0, 32, radix,
            (alive0, zero8, jnp.zeros((8, CHUNK), jnp.int32), ones_in(alive0 & plane_ref[0])))
        thr_n = thr_u ^ jnp.int32(INT_MIN)
        n_eq = ones_in(alive)
        masked_n = thr_n == neg_key
        n_tie_n = jnp.where(masked_n, 0.0, topk - n_gt)
        thr_ref[nxt] = thr_n
        ntie_ref[nxt] = n_tie_n
        more = jnp.max(jnp.where((n_eq > n_tie_n) & ~masked_n, 1.0, 0.0)) > 0.5
        flag_ref[nxt] = more.astype(jnp.int32)


def _sparse_attention(qit, iwt, k1, k2, vt, bias_tbl, tri, topk, nb):
    B, nbp, _, hw = qit.shape
    tp = nbp * CHUNK
    nstep = nbp // 2
    nstep += nstep % 2
    kspec = pl.BlockSpec((1, tp, 128), lambda b, t: (b, 0, 0))
    acc_rows = 80
    nxt = lambda t: jnp.minimum(t + 1, nb - 1)
    return pl.pallas_call(
        functools.partial(_dsa_kernel, topk=topk),
        grid=(B, nb),
        in_specs=[
            pl.BlockSpec((1, 1, 128, hw), lambda b, t: (b, t, 0, 0)),
            pl.BlockSpec((1, 1, 128, hw), lambda b, t: (b, nxt(t), 0, 0)),
            pl.BlockSpec((1, 1, IDX_HEADS, CHUNK), lambda b, t: (b, nxt(t), 0, 0)),
            kspec, kspec,
            pl.BlockSpec((1, 128, tp), lambda b, t: (b, 0, 0)),
            pl.BlockSpec(bias_tbl.shape, lambda b, t: (0, 0, 0)),
            pl.BlockSpec(tri.shape, lambda b, t: (0, 0)),
        ],
        out_specs=pl.BlockSpec((1, CHUNK, DSA_HEADS * DSA_DH),
                               lambda b, t: (b, jnp.maximum(t - 1, 0), 0)),
        out_shape=jax.ShapeDtypeStruct((B, (nb - 1) * CHUNK, DSA_HEADS * DSA_DH), jnp.bfloat16),
        scratch_shapes=[
            pltpu.VMEM((2, nstep, 2 * CHUNK, CHUNK), jnp.int32),
            pltpu.VMEM((32, nstep, 8, CHUNK), jnp.int32),
            pltpu.VMEM((2, 8, CHUNK), jnp.int32),
            pltpu.VMEM((2, 8, CHUNK), jnp.float32),
            pltpu.SMEM((2,), jnp.int32),
            pltpu.VMEM((DSA_HEADS, CHUNK), jnp.float32),
            pltpu.VMEM((acc_rows, hw), jnp.float32),
            pltpu.VMEM((2 * CHUNK, hw), jnp.float32),
            pltpu.VMEM((2 * CHUNK, hw), jnp.float32),
            pltpu.VMEM((2 * CHUNK, hw), jnp.bfloat16),
        ],
        compiler_params=pltpu.CompilerParams(
            dimension_semantics=("arbitrary", "arbitrary"), vmem_limit_bytes=VMEM_LIMIT),
        name="sparse_attention",
    )(qit, qit, iwt, k1, k2, vt, bias_tbl, tri)


def _mlp_kernel(x_ref, yr_ref, yd_ref, wor_ref, wod_ref, n2_ref, w1_ref, w2_ref, o_ref):
    h1 = x_ref[...] + _dot(yr_ref[...], wor_ref[...]) + _dot(yd_ref[...], wod_ref[...])
    ms = jnp.mean(h1 * h1, axis=-1, keepdims=True)
    u = (h1 * lax.rsqrt(ms + EPS) * n2_ref[...]).astype(jnp.bfloat16)
    o_ref[...] = h1
    d_ff = w1_ref.shape[1]
    for c in range(d_ff // FFN_CHUNK):
        sl = slice(c * FFN_CHUNK, (c + 1) * FFN_CHUNK)
        f = jnp.maximum(_dot(u, w1_ref[:, sl]), 0.0)
        o_ref[...] += _dot((f * f).astype(jnp.bfloat16), w2_ref[sl, :])


def _out_mlp(x2, yr, yd, wor, wod, n2, w1, w2):
    rows, D = x2.shape
    const = lambda a: pl.BlockSpec(a.shape, lambda i: (0, 0), pipeline_mode=pl.Buffered(1))
    tile = lambda w: pl.BlockSpec((ROW_TILE, w), lambda i: (i, 0))
    return pl.pallas_call(
        _mlp_kernel,
        grid=(rows // ROW_TILE,),
        in_specs=[tile(D), tile(yr.shape[1]), tile(yd.shape[1]),
                  const(wor), const(wod), const(n2), const(w1), const(w2)],
        out_specs=tile(D),
        out_shape=jax.ShapeDtypeStruct((rows, D), jnp.float32),
        compiler_params=pltpu.CompilerParams(
            dimension_semantics=("arbitrary",), vmem_limit_bytes=VMEM_LIMIT),
        name="out_mlp",
    )(x2, yr, yd, wor, wod, n2, w1, w2)


def kernel(x, meta_tokens, norm1_w, w_in, ret_norm_w, q_norm_w, k_norm_w, rel_bias,
           w_out, norm2_w, w_ff1, w_ff2):
    B, L, D = x.shape
    assert L % (2 * CHUNK) == 0 and L % ROW_TILE == 0 and w_in.shape[0] == 1
    topk = min(TOPK_MAX, L // 4)
    nb = L // CHUNK + 1
    bf = jnp.bfloat16
    f32 = jnp.float32

    rw = RET_HEADS * RET_DK
    dw = DSA_HEADS * DSA_DH
    sizes = (rw, rw, rw, rw, dw, DSA_DH, DSA_DH, IDX_HEADS * DSA_DH, DSA_DH, IDX_HEADS)
    offs = np.concatenate([[0], np.cumsum(sizes)])
    col = lambda i: w_in[0][:, int(offs[i]):int(offs[i + 1])]
    wr = jnp.concatenate([col(0), col(1), col(2), col(3)], axis=1).astype(bf)
    wq = col(4).reshape(D, DSA_HEADS, DSA_DH)
    wiq = col(7).reshape(D, IDX_HEADS, DSA_DH)
    wqi = jnp.concatenate([wq, wiq], axis=2).reshape(D, DSA_HEADS * 128).astype(bf)
    wk = jnp.concatenate([col(5), col(8)], axis=1).astype(bf)
    wv = jnp.concatenate([col(6), jnp.zeros((D, 128 - DSA_DH), f32)], axis=1).astype(bf)
    ww = jnp.concatenate([col(9), jnp.zeros((D, 128 - IDX_HEADS), f32)], axis=1).astype(bf)

    idx_scale = jnp.full((DSA_DH,), DSA_DH ** -0.5, f32)
    qscale = jnp.concatenate([q_norm_w[0].astype(f32) * (DSA_DH ** -0.5 * LOG2E), idx_scale])[None]
    kscale = jnp.concatenate([k_norm_w[0].astype(f32), jnp.ones((DSA_DH,), f32)])[None]
    meta_pad = jnp.concatenate([jnp.zeros((PAD, D), x.dtype), meta_tokens.astype(x.dtype)], axis=0)

    cos2, sin2 = _rope_tables(nb)
    bias_tbl = _bias_tables(rel_bias)
    tri = jnp.asarray(np.tril(np.ones((2 * CHUNK, 2 * CHUNK), np.float32), k=-1), dtype=bf)

    r_all, qit, k1, k2, vt, iwt = _inproj(x, meta_pad, norm1_w[0][None].astype(f32),
                                          wr, wqi, wk, wv, ww, qscale, kscale)
    y_ret = _retention(r_all, nb, cos2, sin2, ret_norm_w[0][None].astype(f32))
    y_dsa = _sparse_attention(qit, iwt, k1, k2, vt, bias_tbl, tri, topk, nb)

    wo = w_out[0].astype(bf)
    out = _out_mlp(x.reshape(B * L, D), y_ret.reshape(B * L, rw), y_dsa.reshape(B * L, dw),
                   wo[:rw], wo[rw:], norm2_w[0][None].astype(f32),
                   w_ff1[0].astype(bf), w_ff2[0].astype(bf))
    return out.reshape(B, L, D)
```

```python
import functools
import math

import numpy as np
import jax
import jax.numpy as jnp
from jax import lax
from jax.experimental import pallas as pl
from jax.experimental.pallas import tpu as pltpu

N_META = 16
CHUNK = 128
RET_HEADS = 4
RET_DK = 128
DSA_HEADS = 8
DSA_DH = 64
IDX_HEADS = 8
TOPK_MAX = 256
N_BUCKETS = 32
MAX_DISTANCE = 128
ROPE_BASE = 10000.0
EPS = 1e-6
NEG = -1e30
PAD = CHUNK - N_META
HALF = 64
INT_MIN = -(2 ** 31)
LOG2E = math.log2(math.e)

FFN_CHUNK = 1024
ROW_TILE = 512
VMEM_LIMIT = 56 * 1024 * 1024


def _dot(a, b):
    return jnp.dot(a, b, preferred_element_type=jnp.float32)


def _dot_nt(a, b):
    return lax.dot_general(a, b, (((1,), (1,)), ((), ())), preferred_element_type=jnp.float32)


def _dot_tn(a, b):
    return lax.dot_general(a, b, (((0,), (0,)), ((), ())), preferred_element_type=jnp.float32)


def _bucket_ranges():
    max_exact = N_BUCKETS // 2
    d = np.arange(0, 2 * CHUNK)
    large = max_exact + (np.log(np.maximum(d, 1) / max_exact) / math.log(MAX_DISTANCE / max_exact)
                         * (N_BUCKETS - max_exact)).astype(np.int64)
    bucket = np.where(d < max_exact, d, np.minimum(large, N_BUCKETS - 1))
    out = []
    for b in range(N_BUCKETS - 1):
        idx = np.nonzero(bucket == b)[0]
        out.append((int(idx.min()), int(idx.max())))
    return out


def _rope_kernel(cos_ref, sin_ref):
    n = pl.program_id(0)
    row = lax.broadcasted_iota(jnp.int32, (CHUNK, 128), 0)
    lane = lax.broadcasted_iota(jnp.int32, (CHUNK, 128), 1)
    pos = (n * CHUNK + row - PAD).astype(jnp.float32)
    frac = (lane % HALF).astype(jnp.float32) / HALF
    inv = jnp.exp(-frac * math.log(ROPE_BASE))
    ang = pos * inv
    cos_ref[...] = jnp.cos(ang)
    s = jnp.sin(ang)
    sin_ref[...] = jnp.where(lane < HALF, -s, s)


def _rope_tables(nb):
    return pl.pallas_call(
        _rope_kernel,
        grid=(nb,),
        out_specs=[pl.BlockSpec((CHUNK, 128), lambda n: (n, 0))] * 2,
        out_shape=[jax.ShapeDtypeStruct((nb * CHUNK, 128), jnp.float32)] * 2,
        name="rope_tables",
    )()


def _bias_kernel(rb_ref, out_ref):
    row = lax.broadcasted_iota(jnp.int32, (CHUNK, CHUNK), 0)
    col = lax.broadcasted_iota(jnp.int32, (CHUNK, CHUNK), 1)
    ranges = _bucket_ranges()
    out_ref[2] = jnp.zeros(out_ref.shape[1:], jnp.float32)
    for h in range(DSA_HEADS):
        far = rb_ref[N_BUCKETS - 1, h]
        for t in range(2):
            dist = col - row + t * CHUNK
            tile = jnp.zeros((CHUNK, CHUNK), jnp.float32)
            for b, (lo, hi) in enumerate(ranges):
                tile = jnp.where((dist >= lo) & (dist <= hi), (rb_ref[b, h] - far) * LOG2E, tile)
            out_ref[t, :, h * 128:(h + 1) * 128] = tile


def _bias_tables(rel_bias):
    return pl.pallas_call(
        _bias_kernel,
        in_specs=[pl.BlockSpec(memory_space=pltpu.SMEM)],
        out_specs=pl.BlockSpec(memory_space=pltpu.VMEM),
        out_shape=jax.ShapeDtypeStruct((3, CHUNK, DSA_HEADS * 128), jnp.float32),
        name="bias_tables",
    )(rel_bias.astype(jnp.float32))


def _inproj_kernel(xa_ref, xb_ref, meta_ref, n1_ref, wr_ref, wqi_ref, wk_ref, wv_ref, ww_ref,
                   qs_ref, ks_ref,
                   r_ref, qit_ref, k1_ref, k2_ref, vt_ref, iwt_ref, *, pad_last):
    s = pl.program_id(1)
    top = jnp.where(s == 0, meta_ref[...], xa_ref[0])
    bot = xb_ref[0]
    if pad_last:
        bot = jnp.where(s == pl.num_programs(1) - 1, 0.0, bot)
    src = jnp.concatenate([top, bot], axis=0)
    ms = jnp.mean(src * src, axis=-1, keepdims=True)
    u = (src * lax.rsqrt(ms + EPS) * n1_ref[...]).astype(jnp.bfloat16)

    r_ref[0] = _dot(u, wr_ref[...])

    lane = lax.broadcasted_iota(jnp.int32, (2 * CHUNK, 128), 1)
    lo = lane < HALF

    def head_rms(t):
        ssq = jnp.sum(jnp.where(lo, t * t, 0.0), axis=-1, keepdims=True)
        return lax.rsqrt(ssq / DSA_DH + EPS)

    pq = _dot(u, wqi_ref[...])
    for h in range(DSA_HEADS):
        t = pq[:, h * 128:(h + 1) * 128]
        t = t * (jnp.where(lo, head_rms(t), 1.0) * qs_ref[...])
        for j in range(2):
            qit_ref[0, j, :, h * 128:(h + 1) * 128] = (
                t[j * CHUNK:(j + 1) * CHUNK].T.astype(jnp.bfloat16))

    pk = _dot(u, wk_ref[...])
    kn = pk * head_rms(pk) * ks_ref[...]
    k1_ref[0] = jnp.where(lo, kn, 0.0).astype(jnp.bfloat16)
    k2_ref[0] = jnp.where(lo, 0.0, pk).astype(jnp.bfloat16)

    pv = _dot(u, wv_ref[...])
    vx = jnp.where(lo, pv, jnp.where(lane == HALF, 1.0, 0.0))
    iw = _dot(u, ww_ref[...]) * (IDX_HEADS ** -0.5)
    for j in range(2):
        rows = slice(j * CHUNK, (j + 1) * CHUNK)
        vt_ref[0, :, rows] = vx[rows].T.astype(jnp.bfloat16)
        iwt_ref[0, j] = iw[rows].T[0:IDX_HEADS, :]


def _inproj(x, meta_pad, n1, wr, wqi, wk, wv, ww, qscale, kscale):
    B, L, D = x.shape
    nx = L // CHUNK
    nbp = nx + 1 + ((nx + 1) % 2)
    tp = nbp * CHUNK
    const = lambda shape: pl.BlockSpec(shape, lambda b, s: (0,) * len(shape))
    blk = lambda w: pl.BlockSpec((1, 2 * CHUNK, w), lambda b, s: (b, s, 0))
    return pl.pallas_call(
        functools.partial(_inproj_kernel, pad_last=bool((nx + 1) % 2)),
        grid=(B, nbp // 2),
        in_specs=[
            pl.BlockSpec((1, CHUNK, D), lambda b, s: (b, jnp.maximum(2 * s - 1, 0), 0)),
            pl.BlockSpec((1, CHUNK, D), lambda b, s: (b, jnp.minimum(2 * s, nx - 1), 0)),
            const(meta_pad.shape), const(n1.shape), const(wr.shape), const(wqi.shape),
            const(wk.shape), const(wv.shape), const(ww.shape), const(qscale.shape),
            const(kscale.shape),
        ],
        out_specs=[
            blk(wr.shape[1]),
            pl.BlockSpec((1, 2, 128, DSA_HEADS * 128), lambda b, s: (b, s, 0, 0)),
            blk(128), blk(128),
            pl.BlockSpec((1, 128, 2 * CHUNK), lambda b, s: (b, 0, s)),
            pl.BlockSpec((1, 2, IDX_HEADS, CHUNK), lambda b, s: (b, s, 0, 0)),
        ],
        out_shape=[
            jax.ShapeDtypeStruct((B, tp, wr.shape[1]), jnp.float32),
            jax.ShapeDtypeStruct((B, nbp, 128, DSA_HEADS * 128), jnp.bfloat16),
            jax.ShapeDtypeStruct((B, tp, 128), jnp.bfloat16),
            jax.ShapeDtypeStruct((B, tp, 128), jnp.bfloat16),
            jax.ShapeDtypeStruct((B, 128, tp), jnp.bfloat16),
            jax.ShapeDtypeStruct((B, nbp, IDX_HEADS, CHUNK), jnp.float32),
        ],
        compiler_params=pltpu.CompilerParams(
            dimension_semantics=("arbitrary", "arbitrary"), vmem_limit_bytes=VMEM_LIMIT),
        name="in_projection",
    )(x, x, meta_pad, n1, wr, wqi, wk, wv, ww, qscale, kscale)


def _retention_kernel(ra_ref, rb_ref, cosa_ref, sina_ref, cosb_ref, sinb_ref, gain_ref, y_ref,
                      state_ref, tbl_ref):
    s = pl.program_id(1)
    w = RET_HEADS * RET_DK
    bf = jnp.bfloat16

    @pl.when(s == 0)
    def _():
        state_ref[...] = jnp.zeros_like(state_ref)
        row = lax.broadcasted_iota(jnp.int32, (CHUNK, CHUNK), 0)
        col = lax.broadcasted_iota(jnp.int32, (CHUNK, CHUNK), 1)
        rowf = row.astype(jnp.float32)
        diff = (row - col).astype(jnp.float32)
        for h in range(RET_HEADS):
            log_gamma = math.log(1.0 - 2.0 ** (-5.0 - h))
            tbl_ref[0, h] = jnp.where(diff >= 0, jnp.exp(log_gamma * jnp.maximum(diff, 0.0)), 0.0)
            tbl_ref[1, h] = jnp.exp(log_gamma * (rowf + 1.0))
            tbl_ref[2, h] = jnp.exp(log_gamma * (CHUNK - 1.0 - rowf))

    live = jnp.where(s == 0, 0.0, 1.0)
    chunks = ((ra_ref, cosa_ref, sina_ref, live), (rb_ref, cosb_ref, sinb_ref, None))
    for h in range(RET_HEADS):
        log_gamma = math.log(1.0 - 2.0 ** (-5.0 - h))
        sl = slice(h * 128, (h + 1) * 128)
        state = state_ref[h]
        for j, (r_ref, cos_ref, sin_ref, scale) in enumerate(chunks):
            cos2, sin2 = cos_ref[...], sin_ref[...]
            q = r_ref[0, :, sl]
            k = r_ref[0, :, w + h * 128: w + (h + 1) * 128]
            v = r_ref[0, :, 2 * w + h * 128: 2 * w + (h + 1) * 128]
            g = r_ref[0, :, 3 * w + h * 128: 3 * w + (h + 1) * 128]
            if scale is not None:
                k = k * scale
            v = v.astype(bf)
            qr = q * cos2 + pltpu.roll(q, HALF, 1) * sin2
            kr = (k * cos2 + pltpu.roll(k, HALF, 1) * sin2) * (RET_DK ** -0.5)

            scores = _dot_nt(qr.astype(bf), kr.astype(bf)) * tbl_ref[0, h]
            o = _dot(scores.astype(bf), v)
            o = o + _dot((qr * tbl_ref[1, h]).astype(bf), state.astype(bf))
            kv = _dot_tn((kr * tbl_ref[2, h]).astype(bf), v)
            state = state * math.exp(log_gamma * CHUNK) + kv

            ms = jnp.mean(o * o, axis=-1, keepdims=True)
            on = o * lax.rsqrt(ms + EPS) * gain_ref[:, sl]
            gate = g * (1.0 / (1.0 + jnp.exp(-g)))
            y_ref[0, j * CHUNK:(j + 1) * CHUNK, sl] = (gate * on).astype(y_ref.dtype)
        state_ref[h] = state


def _retention(r_all, nb, cos2, sin2, gain):
    B = r_all.shape[0]
    w = RET_HEADS * RET_DK
    nstep = (nb + 1) // 2
    rspec = lambda f: pl.BlockSpec((1, CHUNK, 4 * w), lambda b, s: (b, f(s), 0))
    tspec = lambda f: pl.BlockSpec((CHUNK, 128), lambda b, s: (f(s), 0))
    first = lambda s: jnp.maximum(2 * s - 1, 0)
    second = lambda s: 2 * s
    return pl.pallas_call(
        _retention_kernel,
        grid=(B, nstep),
        in_specs=[rspec(first), rspec(second), tspec(first), tspec(first), tspec(second),
                  tspec(second), pl.BlockSpec((1, w), lambda b, s: (0, 0))],
        out_specs=pl.BlockSpec((1, 2 * CHUNK, w), lambda b, s: (b, jnp.maximum(s - 1, 0), 0)),
        out_shape=jax.ShapeDtypeStruct((B, (nb - 1) * CHUNK, w), jnp.bfloat16),
        scratch_shapes=[pltpu.VMEM((RET_HEADS, RET_DK, RET_DK), jnp.float32),
                        pltpu.VMEM((3, RET_HEADS, CHUNK, CHUNK), jnp.float32)],
        compiler_params=pltpu.CompilerParams(
            dimension_semantics=("arbitrary", "arbitrary"), vmem_limit_bytes=VMEM_LIMIT),
        name="retention",
    )(r_all, r_all, cos2, sin2, cos2, sin2, gain)


def _bit_planes(tile):
    a = [tile[r * 8:(r + 1) * 8, :] for r in range(32)]
    j, m = 16, 0x0000FFFF
    while j:
        for k in range(32):
            if (k & j) == 0:
                t = (a[k] ^ lax.shift_right_logical(a[k + j], j)) & m
                a[k] = a[k] ^ t
                a[k + j] = a[k + j] ^ (t << j)
        j >>= 1
        if j:
            m = m ^ ((m << j) & 0xFFFFFFFF)
    return a


def _sublane_total(x):
    x = x + pltpu.roll(x, 4, 0)
    x = x + pltpu.roll(x, 2, 0)
    return x + pltpu.roll(x, 1, 0)


def _dsa_kernel(qa_ref, qb_ref, iwt_ref, k1_ref, k2_ref, vt_ref, bias_ref, tri_ref, y_ref,
                key_ref, plane_ref, thr_ref, ntie_ref, flag_ref, m_ref, acc_ref, sa_ref, sb_ref,
                rhs_ref, *, topk):
    t = pl.program_id(1)
    last_q = pl.num_programs(1) - 1
    bf = jnp.bfloat16
    H = DSA_HEADS
    KP = 2 * CHUNK
    G = plane_ref.shape[1]
    cur, nxt = t % 2, (t + 1) % 2

    @pl.when((pl.program_id(0) == 0) & (t == 0))
    def _():
        plane_ref[...] = jnp.zeros_like(plane_ref)
        key_ref[...] = jnp.zeros_like(key_ref)
        eye = (lax.broadcasted_iota(jnp.int32, (CHUNK, CHUNK), 0)
               == lax.broadcasted_iota(jnp.int32, (CHUNK, CHUNK), 1))
        for h in range(DSA_HEADS):
            rhs_ref[CHUNK:2 * CHUNK, h * 128:(h + 1) * 128] = jnp.where(eye, 1.0, 0.0).astype(bf)

    @pl.when(t == 0)
    def _():
        thr_ref[0] = jnp.zeros(thr_ref.shape[1:], jnp.int32)
        ntie_ref[0] = jnp.zeros(ntie_ref.shape[1:], jnp.float32)
        flag_ref[0] = 0

    row = lax.broadcasted_iota(jnp.int32, (KP, CHUNK), 0)
    col = lax.broadcasted_iota(jnp.int32, (KP, CHUNK), 1)
    hs = lambda h: slice(h * 128, (h + 1) * 128)
    hp = lambda c: slice(c * 256, (c + 1) * 256)
    nsteps = lambda n: (n + 2) // 2

    def key_rows(g):
        last = k1_ref.shape[1] // KP - 1
        return pl.ds(pl.multiple_of(jnp.minimum(g, last) * KP, KP), KP)

    nb_ = t + 1
    steps_b = jnp.where(t < last_q, nsteps(nb_), 0)
    iters_b, tail_b = steps_b // 2, steps_b % 2
    qb_idx = nb_ * CHUNK + col
    iw_rows = [jnp.broadcast_to(iwt_ref[0, 0, h:h + 1, :], (KP, CHUNK)) for h in range(IDX_HEADS)]

    def score_pair(g):
        kblk = k2_ref[0, key_rows(g), :]
        sc = None
        for c in range(H // 2):
            s = _dot(kblk, qb_ref[0, 0, :, hp(c)])
            for hh in range(2):
                term = iw_rows[2 * c + hh] * jnp.maximum(s[:, hs(hh)], 0.0)
                sc = term if sc is None else sc + term
        k_idx = g * KP + row
        valid = (k_idx <= qb_idx) & (k_idx >= PAD)
        sc = jnp.where(valid, sc, NEG)
        bits = lax.bitcast_convert_type(sc, jnp.int32)
        key = jnp.where(bits < 0, jnp.int32(INT_MIN) - bits, bits)
        key_ref[nxt, g] = key
        planes = _bit_planes(key ^ jnp.int32(INT_MIN))
        for p in range(32):
            plane_ref[p, g] = planes[p]

    def score_two(i):
        score_pair(2 * i)
        score_pair(2 * i + 1)

    na_ = t
    steps_a = jnp.where(t > 0, nsteps(na_), 0)
    iters_a, tail_a = steps_a // 2, steps_a % 2
    far_a = jnp.where(t > 0, ((na_ - 1) // 2) // 2, 0)
    neg_key = -(int(np.float32(NEG).view(np.int32)) & 0x7FFFFFFF)
    rep = lambda x: jnp.tile(x, (KP // 8, 1))
    thr = thr_ref[cur]
    n_tie = ntie_ref[cur]
    masked_thr = thr == neg_key
    thr_b = rep(thr)
    thr_sel_b = rep(jnp.where(masked_thr, jnp.int32(neg_key + 1), thr))
    n_tie_b = rep(n_tie)
    has_ties = flag_ref[cur] > 0
    va = acc_ref.shape[0]

    @pl.when(t > 0)
    def _():
        m_ref[...] = jnp.full_like(m_ref, NEG)
        acc_ref[...] = jnp.zeros_like(acc_ref)
        rhs_ref[0:CHUNK, :] = qa_ref[0, 0]

    def logits(g, dst_ref, ties):
        kblk = k1_ref[0, key_rows(g), :]
        if ties:
            dst_ref[...] = _dot(kblk, qa_ref[0, 0])
        else:
            off = jnp.where(key_ref[cur, g] >= thr_sel_b, 0.0, NEG).astype(bf)
            dst_ref[...] = _dot(jnp.concatenate([kblk, off], axis=1), rhs_ref[...])

    def attend(g, s_ref, ties_before, near, ties):
        if ties:
            key = key_ref[cur, g]
            eq = key == thr_b
            eqf = jnp.where(eq, 1.0, 0.0)
            before = ties_before + _dot(tri_ref[...], eqf.astype(bf))
            sel = (key > thr_b) | (eq & (before < n_tie_b))
            ties_before = ties_before + jnp.sum(eqf, axis=0, keepdims=True)
        ps, alphas = [], []
        for h in range(H):
            s = s_ref[:, hs(h)]
            if near:
                ta = jnp.clip(na_ - 2 * g, 0, 2)
                tb = jnp.clip(na_ - 2 * g - 1, 0, 2)
                s = s + jnp.concatenate([bias_ref[ta, :, hs(h)], bias_ref[tb, :, hs(h)]], axis=0)
            if ties:
                s = jnp.where(sel, s, NEG)
            m_prev = m_ref[h:h + 1, :]
            m_new = jnp.maximum(m_prev, jnp.max(s, axis=0, keepdims=True))
            m_ref[h:h + 1, :] = m_new
            alphas.append(jnp.exp2(m_prev - m_new))
            ps.append(jnp.exp2(s - m_new).astype(bf))
        pv = _dot(vt_ref[0, 0:va, key_rows(g)], jnp.concatenate(ps, axis=1))
        acc_ref[...] = acc_ref[...] * jnp.concatenate(alphas, axis=1) + pv
        return ties_before

    def attend_two(i, ties_before, near, ties, fused):
        logits(2 * i + 1, sb_ref, ties)
        ties_before = attend(2 * i, sa_ref, ties_before, near, ties)
        logits(2 * i + 2, sa_ref, ties)
        ties_before = attend(2 * i + 1, sb_ref, ties_before, near, ties)
        if fused:
            score_two(i)
        return ties_before

    zero_ties = jnp.zeros((KP, CHUNK), jnp.float32)
    loop = lambda lo, hi, **kw: (lambda c: lax.fori_loop(lo, hi, functools.partial(attend_two, **kw), c))
    fused_hi = jnp.minimum(iters_a, iters_b)

    @pl.when(has_ties)
    def _():
        logits(0, sa_ref, True)
        c = loop(0, far_a, near=False, ties=True, fused=False)(zero_ties)
        c = loop(far_a, iters_a, near=True, ties=True, fused=False)(c)

        @pl.when(tail_a == 1)
        def _():
            attend(2 * iters_a, sa_ref, c, True, True)

    @pl.when(jnp.logical_not(has_ties))
    def _():
        logits(0, sa_ref, False)
        far_hi = jnp.minimum(far_a, fused_hi)
        loop(0, far_hi, near=False, ties=False, fused=True)(zero_ties)
        loop(far_hi, fused_hi, near=True, ties=False, fused=True)(zero_ties)
        loop(fused_hi, iters_a, near=True, ties=False, fused=False)(zero_ties)

        @pl.when(tail_a == 1)
        def _():
            attend(2 * iters_a, sa_ref, zero_ties, True, False)

    lax.fori_loop(jnp.where(has_ties, 0, fused_hi), iters_b, lambda i, _: (score_two(i), 0)[1], 0)

    @pl.when(tail_b == 1)
    def _():
        score_pair(2 * iters_b)

    @pl.when(t > 0)
    def _():
        lo = lax.broadcasted_iota(jnp.int32, (CHUNK, CHUNK), 1) < HALF
        for c in range(H // 2):
            tiles = []
            for h in (2 * c, 2 * c + 1):
                a = acc_ref[:, hs(h)] * (1.0 / acc_ref[HALF:HALF + 1, hs(h)])
                a = jnp.concatenate([a, jnp.zeros((CHUNK - va, CHUNK), jnp.float32)], axis=0)
                tiles.append(a.T)
            y_ref[0, :, hs(c)] = jnp.where(lo, tiles[0], pltpu.roll(tiles[1], HALF, 1)).astype(y_ref.dtype)

    @pl.when(t < last_q)
    def _():
        npair_b = (nb_ + 2) // 2
        g_idx = lax.broadcasted_iota(jnp.int32, (G, 8, CHUNK), 0)
        alive0 = jnp.where(g_idx < npair_b, jnp.int32(-1), jnp.int32(0))

        def ones_in(words):
            return _sublane_total(lax.population_count(words).sum(axis=0))

        def radix(i, carry):
            alive, above, thr_u = carry
            hi_plane, lo_plane = plane_ref[2 * i], plane_ref[2 * i + 1]
            a1 = alive & hi_plane
            a0 = alive ^ a1
            a11, a01 = a1 & lo_plane, a0 & lo_plane
            c1, c11, c01 = ones_in(a1), ones_in(a11), ones_in(a01)
            hi = (above + c1) >= topk
            above = jnp.where(hi, above, above + c1)
            c_lo = jnp.where(hi, c11, c01)
            lo = (above + c_lo) >= topk
            above = jnp.where(lo, above, above + c_lo)
            bit = lax.shift_right_logical(jnp.int32(INT_MIN), 2 * i)
            thr_u = (thr_u | jnp.where(hi, bit, 0)
                     | jnp.where(lo, lax.shift_right_logical(bit, 1), 0))
            group = jnp.where(hi[None], a1, a0)
            with_lo = jnp.where(hi[None], a11, a01)
            alive = jnp.where(lo[None], with_lo, group ^ with_lo)
            return alive, above, thr_u

        zero8 = jnp.zeros((8, CHUNK), jnp.int32)
        alive, n_gt, thr_u = lax.fori_loop(0, 16, radix, (alive0, zero8, zero8))
        n_gt = n_gt.astype(jnp.float32)
        thr_n = thr_u ^ jnp.int32(INT_MIN)
        n_eq = ones_in(alive).astype(jnp.float32)
        masked_n = thr_n == neg_key
        n_tie_n = jnp.where(masked_n, 0.0, topk - n_gt)
        thr_ref[nxt] = thr_n
        ntie_ref[nxt] = n_tie_n
        more = jnp.max(jnp.where((n_eq > n_tie_n) & ~masked_n, 1.0, 0.0)) > 0.5
        flag_ref[nxt] = more.astype(jnp.int32)


def _sparse_attention(qit, iwt, k1, k2, vt, bias_tbl, tri, topk, nb):
    B, nbp, _, hw = qit.shape
    tp = nbp * CHUNK
    nstep = nbp // 2
    nstep += nstep % 2
    kspec = pl.BlockSpec((1, tp, 128), lambda b, t: (b, 0, 0))
    acc_rows = 80
    nxt = lambda t: jnp.minimum(t + 1, nb - 1)
    return pl.pallas_call(
        functools.partial(_dsa_kernel, topk=topk),
        grid=(B, nb),
        in_specs=[
            pl.BlockSpec((1, 1, 128, hw), lambda b, t: (b, t, 0, 0)),
            pl.BlockSpec((1, 1, 128, hw), lambda b, t: (b, nxt(t), 0, 0)),
            pl.BlockSpec((1, 1, IDX_HEADS, CHUNK), lambda b, t: (b, nxt(t), 0, 0)),
            kspec, kspec,
            pl.BlockSpec((1, 128, tp), lambda b, t: (b, 0, 0)),
            pl.BlockSpec(bias_tbl.shape, lambda b, t: (0, 0, 0)),
            pl.BlockSpec(tri.shape, lambda b, t: (0, 0)),
        ],
        out_specs=pl.BlockSpec((1, CHUNK, DSA_HEADS * DSA_DH),
                               lambda b, t: (b, jnp.maximum(t - 1, 0), 0)),
        out_shape=jax.ShapeDtypeStruct((B, (nb - 1) * CHUNK, DSA_HEADS * DSA_DH), jnp.bfloat16),
        scratch_shapes=[
            pltpu.VMEM((2, nstep, 2 * CHUNK, CHUNK), jnp.int32),
            pltpu.VMEM((32, nstep, 8, CHUNK), jnp.int32),
            pltpu.VMEM((2, 8, CHUNK), jnp.int32),
            pltpu.VMEM((2, 8, CHUNK), jnp.float32),
            pltpu.SMEM((2,), jnp.int32),
            pltpu.VMEM((DSA_HEADS, CHUNK), jnp.float32),
            pltpu.VMEM((acc_rows, hw), jnp.float32),
            pltpu.VMEM((2 * CHUNK, hw), jnp.float32),
            pltpu.VMEM((2 * CHUNK, hw), jnp.float32),
            pltpu.VMEM((2 * CHUNK, hw), jnp.bfloat16),
        ],
        compiler_params=pltpu.CompilerParams(
            dimension_semantics=("arbitrary", "arbitrary"), vmem_limit_bytes=VMEM_LIMIT),
        name="sparse_attention",
    )(qit, qit, iwt, k1, k2, vt, bias_tbl, tri)


def _mlp_kernel(x_ref, yr_ref, yd_ref, wor_ref, wod_ref, n2_ref, w1_ref, w2_ref, o_ref):
    h1 = x_ref[...] + _dot(yr_ref[...], wor_ref[...]) + _dot(yd_ref[...], wod_ref[...])
    ms = jnp.mean(h1 * h1, axis=-1, keepdims=True)
    u = (h1 * lax.rsqrt(ms + EPS) * n2_ref[...]).astype(jnp.bfloat16)
    o_ref[...] = h1
    d_ff = w1_ref.shape[1]
    for c in range(d_ff // FFN_CHUNK):
        sl = slice(c * FFN_CHUNK, (c + 1) * FFN_CHUNK)
        f = jnp.maximum(_dot(u, w1_ref[:, sl]), 0.0)
        o_ref[...] += _dot((f * f).astype(jnp.bfloat16), w2_ref[sl, :])


def _out_mlp(x2, yr, yd, wor, wod, n2, w1, w2):
    rows, D = x2.shape
    const = lambda a: pl.BlockSpec(a.shape, lambda i: (0, 0), pipeline_mode=pl.Buffered(1))
    tile = lambda w: pl.BlockSpec((ROW_TILE, w), lambda i: (i, 0))
    return pl.pallas_call(
        _mlp_kernel,
        grid=(rows // ROW_TILE,),
        in_specs=[tile(D), tile(yr.shape[1]), tile(yd.shape[1]),
                  const(wor), const(wod), const(n2), const(w1), const(w2)],
        out_specs=tile(D),
        out_shape=jax.ShapeDtypeStruct((rows, D), jnp.float32),
        compiler_params=pltpu.CompilerParams(
            dimension_semantics=("arbitrary",), vmem_limit_bytes=VMEM_LIMIT),
        name="out_mlp",
    )(x2, yr, yd, wor, wod, n2, w1, w2)


def kernel(x, meta_tokens, norm1_w, w_in, ret_norm_w, q_norm_w, k_norm_w, rel_bias,
           w_out, norm2_w, w_ff1, w_ff2):
    B, L, D = x.shape
    assert L % (2 * CHUNK) == 0 and L % ROW_TILE == 0 and w_in.shape[0] == 1
    topk = min(TOPK_MAX, L // 4)
    nb = L // CHUNK + 1
    bf = jnp.bfloat16
    f32 = jnp.float32

    rw = RET_HEADS * RET_DK
    dw = DSA_HEADS * DSA_DH
    sizes = (rw, rw, rw, rw, dw, DSA_DH, DSA_DH, IDX_HEADS * DSA_DH, DSA_DH, IDX_HEADS)
    offs = np.concatenate([[0], np.cumsum(sizes)])
    col = lambda i: w_in[0][:, int(offs[i]):int(offs[i + 1])]
    wr = jnp.concatenate([col(0), col(1), col(2), col(3)], axis=1).astype(bf)
    wq = col(4).reshape(D, DSA_HEADS, DSA_DH)
    wiq = col(7).reshape(D, IDX_HEADS, DSA_DH)
    wqi = jnp.concatenate([wq, wiq], axis=2).reshape(D, DSA_HEADS * 128).astype(bf)
    wk = jnp.concatenate([col(5), col(8)], axis=1).astype(bf)
    wv = jnp.concatenate([col(6), jnp.zeros((D, 128 - DSA_DH), f32)], axis=1).astype(bf)
    ww = jnp.concatenate([col(9), jnp.zeros((D, 128 - IDX_HEADS), f32)], axis=1).astype(bf)

    idx_scale = jnp.full((DSA_DH,), DSA_DH ** -0.5, f32)
    qscale = jnp.concatenate([q_norm_w[0].astype(f32) * (DSA_DH ** -0.5 * LOG2E), idx_scale])[None]
    kscale = jnp.concatenate([k_norm_w[0].astype(f32), jnp.ones((DSA_DH,), f32)])[None]
    meta_pad = jnp.concatenate([jnp.zeros((PAD, D), x.dtype), meta_tokens.astype(x.dtype)], axis=0)

    cos2, sin2 = _rope_tables(nb)
    bias_tbl = _bias_tables(rel_bias)
    tri = jnp.asarray(np.tril(np.ones((2 * CHUNK, 2 * CHUNK), np.float32), k=-1), dtype=bf)

    r_all, qit, k1, k2, vt, iwt = _inproj(x, meta_pad, norm1_w[0][None].astype(f32),
                                          wr, wqi, wk, wv, ww, qscale, kscale)
    y_ret = _retention(r_all, nb, cos2, sin2, ret_norm_w[0][None].astype(f32))
    y_dsa = _sparse_attention(qit, iwt, k1, k2, vt, bias_tbl, tri, topk, nb)

    wo = w_out[0].astype(bf)
    out = _out_mlp(x.reshape(B * L, D), y_ret.reshape(B * L, rw), y_dsa.reshape(B * L, dw),
                   wo[:rw], wo[rw:], norm2_w[0][None].astype(f32),
                   w_ff1[0].astype(bf), w_ff2[0].astype(bf))
    return out.reshape(B, L, D)
```

```python
import functools
import math

import numpy as np
import jax
import jax.numpy as jnp
from jax import lax
from jax.experimental import pallas as pl
from jax.experimental.pallas import tpu as pltpu

N_META = 16
CHUNK = 128
RET_HEADS = 4
RET_DK = 128
DSA_HEADS = 8
DSA_DH = 64
IDX_HEADS = 8
TOPK_MAX = 256
N_BUCKETS = 32
MAX_DISTANCE = 128
ROPE_BASE = 10000.0
EPS = 1e-6
NEG = -1e30
PAD = CHUNK - N_META
HALF = 64
INT_MIN = -(2 ** 31)
LOG2E = math.log2(math.e)

FFN_CHUNK = 1024
ROW_TILE = 512
VMEM_LIMIT = 56 * 1024 * 1024


def _dot(a, b):
    return jnp.dot(a, b, preferred_element_type=jnp.float32)


def _dot_nt(a, b):
    return lax.dot_general(a, b, (((1,), (1,)), ((), ())), preferred_element_type=jnp.float32)


def _dot_tn(a, b):
    return lax.dot_general(a, b, (((0,), (0,)), ((), ())), preferred_element_type=jnp.float32)


def _bucket_ranges():
    max_exact = N_BUCKETS // 2
    d = np.arange(0, 2 * CHUNK)
    large = max_exact + (np.log(np.maximum(d, 1) / max_exact) / math.log(MAX_DISTANCE / max_exact)
                         * (N_BUCKETS - max_exact)).astype(np.int64)
    bucket = np.where(d < max_exact, d, np.minimum(large, N_BUCKETS - 1))
    out = []
    for b in range(N_BUCKETS - 1):
        idx = np.nonzero(bucket == b)[0]
        out.append((int(idx.min()), int(idx.max())))
    return out


def _rope_kernel(cos_ref, sin_ref):
    n = pl.program_id(0)
    row = lax.broadcasted_iota(jnp.int32, (CHUNK, 128), 0)
    lane = lax.broadcasted_iota(jnp.int32, (CHUNK, 128), 1)
    pos = (n * CHUNK + row - PAD).astype(jnp.float32)
    frac = (lane % HALF).astype(jnp.float32) / HALF
    inv = jnp.exp(-frac * math.log(ROPE_BASE))
    ang = pos * inv
    cos_ref[...] = jnp.cos(ang)
    s = jnp.sin(ang)
    sin_ref[...] = jnp.where(lane < HALF, -s, s)


def _rope_tables(nb):
    return pl.pallas_call(
        _rope_kernel,
        grid=(nb,),
        out_specs=[pl.BlockSpec((CHUNK, 128), lambda n: (n, 0))] * 2,
        out_shape=[jax.ShapeDtypeStruct((nb * CHUNK, 128), jnp.float32)] * 2,
        name="rope_tables",
    )()


def _bias_kernel(rb_ref, out_ref):
    row = lax.broadcasted_iota(jnp.int32, (CHUNK, CHUNK), 0)
    col = lax.broadcasted_iota(jnp.int32, (CHUNK, CHUNK), 1)
    ranges = _bucket_ranges()
    out_ref[2] = jnp.zeros(out_ref.shape[1:], jnp.float32)
    for h in range(DSA_HEADS):
        far = rb_ref[N_BUCKETS - 1, h]
        for t in range(2):
            dist = col - row + t * CHUNK
            tile = jnp.zeros((CHUNK, CHUNK), jnp.float32)
            for b, (lo, hi) in enumerate(ranges):
                tile = jnp.where((dist >= lo) & (dist <= hi), (rb_ref[b, h] - far) * LOG2E, tile)
            out_ref[t, :, h * 128:(h + 1) * 128] = tile


def _bias_tables(rel_bias):
    return pl.pallas_call(
        _bias_kernel,
        in_specs=[pl.BlockSpec(memory_space=pltpu.SMEM)],
        out_specs=pl.BlockSpec(memory_space=pltpu.VMEM),
        out_shape=jax.ShapeDtypeStruct((3, CHUNK, DSA_HEADS * 128), jnp.float32),
        name="bias_tables",
    )(rel_bias.astype(jnp.float32))


def _inproj_kernel(xa_ref, xb_ref, meta_ref, n1_ref, wr_ref, wqi_ref, wk_ref, wv_ref, ww_ref,
                   qs_ref, ks_ref,
                   r_ref, qit_ref, k1_ref, k2_ref, vt_ref, iwt_ref, *, pad_last):
    s = pl.program_id(1)
    top = jnp.where(s == 0, meta_ref[...], xa_ref[0])
    bot = xb_ref[0]
    if pad_last:
        bot = jnp.where(s == pl.num_programs(1) - 1, 0.0, bot)
    src = jnp.concatenate([top, bot], axis=0)
    ms = jnp.mean(src * src, axis=-1, keepdims=True)
    u = (src * lax.rsqrt(ms + EPS) * n1_ref[...]).astype(jnp.bfloat16)

    r_ref[0] = _dot(u, wr_ref[...])

    lane = lax.broadcasted_iota(jnp.int32, (2 * CHUNK, 128), 1)
    lo = lane < HALF

    def head_rms(t):
        ssq = jnp.sum(jnp.where(lo, t * t, 0.0), axis=-1, keepdims=True)
        return lax.rsqrt(ssq / DSA_DH + EPS)

    pq = _dot(u, wqi_ref[...])
    for h in range(DSA_HEADS):
        t = pq[:, h * 128:(h + 1) * 128]
        t = t * (jnp.where(lo, head_rms(t), 1.0) * qs_ref[...])
        for j in range(2):
            qit_ref[0, j, :, h * 128:(h + 1) * 128] = (
                t[j * CHUNK:(j + 1) * CHUNK].T.astype(jnp.bfloat16))

    pk = _dot(u, wk_ref[...])
    kn = pk * head_rms(pk) * ks_ref[...]
    k1_ref[0] = jnp.where(lo, kn, 0.0).astype(jnp.bfloat16)
    k2_ref[0] = jnp.where(lo, 0.0, pk).astype(jnp.bfloat16)

    pv = _dot(u, wv_ref[...])
    vx = jnp.where(lo, pv, jnp.where(lane == HALF, 1.0, 0.0))
    iw = _dot(u, ww_ref[...]) * (IDX_HEADS ** -0.5)
    for j in range(2):
        rows = slice(j * CHUNK, (j + 1) * CHUNK)
        vt_ref[0, :, rows] = vx[rows].T.astype(jnp.bfloat16)
        iwt_ref[0, j] = iw[rows].T[0:IDX_HEADS, :]


def _inproj(x, meta_pad, n1, wr, wqi, wk, wv, ww, qscale, kscale):
    B, L, D = x.shape
    nx = L // CHUNK
    nbp = nx + 1 + ((nx + 1) % 2)
    tp = nbp * CHUNK
    const = lambda shape: pl.BlockSpec(shape, lambda b, s: (0,) * len(shape))
    blk = lambda w: pl.BlockSpec((1, 2 * CHUNK, w), lambda b, s: (b, s, 0))
    return pl.pallas_call(
        functools.partial(_inproj_kernel, pad_last=bool((nx + 1) % 2)),
        grid=(B, nbp // 2),
        in_specs=[
            pl.BlockSpec((1, CHUNK, D), lambda b, s: (b, jnp.maximum(2 * s - 1, 0), 0)),
            pl.BlockSpec((1, CHUNK, D), lambda b, s: (b, jnp.minimum(2 * s, nx - 1), 0)),
            const(meta_pad.shape), const(n1.shape), const(wr.shape), const(wqi.shape),
            const(wk.shape), const(wv.shape), const(ww.shape), const(qscale.shape),
            const(kscale.shape),
        ],
        out_specs=[
            blk(wr.shape[1]),
            pl.BlockSpec((1, 2, 128, DSA_HEADS * 128), lambda b, s: (b, s, 0, 0)),
            blk(128), blk(128),
            pl.BlockSpec((1, 128, 2 * CHUNK), lambda b, s: (b, 0, s)),
            pl.BlockSpec((1, 2, IDX_HEADS, CHUNK), lambda b, s: (b, s, 0, 0)),
        ],
        out_shape=[
            jax.ShapeDtypeStruct((B, tp, wr.shape[1]), jnp.float32),
            jax.ShapeDtypeStruct((B, nbp, 128, DSA_HEADS * 128), jnp.bfloat16),
            jax.ShapeDtypeStruct((B, tp, 128), jnp.bfloat16),
            jax.ShapeDtypeStruct((B, tp, 128), jnp.bfloat16),
            jax.ShapeDtypeStruct((B, 128, tp), jnp.bfloat16),
            jax.ShapeDtypeStruct((B, nbp, IDX_HEADS, CHUNK), jnp.float32),
        ],
        compiler_params=pltpu.CompilerParams(
            dimension_semantics=("arbitrary", "arbitrary"), vmem_limit_bytes=VMEM_LIMIT),
        name="in_projection",
    )(x, x, meta_pad, n1, wr, wqi, wk, wv, ww, qscale, kscale)


def _retention_kernel(ra_ref, rb_ref, cosa_ref, sina_ref, cosb_ref, sinb_ref, gain_ref, y_ref,
                      state_ref, tbl_ref):
    s = pl.program_id(1)
    w = RET_HEADS * RET_DK
    bf = jnp.bfloat16

    @pl.when(s == 0)
    def _():
        state_ref[...] = jnp.zeros_like(state_ref)
        row = lax.broadcasted_iota(jnp.int32, (CHUNK, CHUNK), 0)
        col = lax.broadcasted_iota(jnp.int32, (CHUNK, CHUNK), 1)
        rowf = row.astype(jnp.float32)
        diff = (row - col).astype(jnp.float32)
        for h in range(RET_HEADS):
            log_gamma = math.log(1.0 - 2.0 ** (-5.0 - h))
            tbl_ref[0, h] = jnp.where(diff >= 0, jnp.exp(log_gamma * jnp.maximum(diff, 0.0)), 0.0)
            tbl_ref[1, h] = jnp.exp(log_gamma * (rowf + 1.0))
            tbl_ref[2, h] = jnp.exp(log_gamma * (CHUNK - 1.0 - rowf))

    live = jnp.where(s == 0, 0.0, 1.0)
    chunks = ((ra_ref, cosa_ref, sina_ref, live), (rb_ref, cosb_ref, sinb_ref, None))
    for h in range(RET_HEADS):
        log_gamma = math.log(1.0 - 2.0 ** (-5.0 - h))
        sl = slice(h * 128, (h + 1) * 128)
        state = state_ref[h]
        for j, (r_ref, cos_ref, sin_ref, scale) in enumerate(chunks):
            cos2, sin2 = cos_ref[...], sin_ref[...]
            q = r_ref[0, :, sl]
            k = r_ref[0, :, w + h * 128: w + (h + 1) * 128]
            v = r_ref[0, :, 2 * w + h * 128: 2 * w + (h + 1) * 128]
            g = r_ref[0, :, 3 * w + h * 128: 3 * w + (h + 1) * 128]
            if scale is not None:
                k = k * scale
            v = v.astype(bf)
            qr = q * cos2 + pltpu.roll(q, HALF, 1) * sin2
            kr = (k * cos2 + pltpu.roll(k, HALF, 1) * sin2) * (RET_DK ** -0.5)

            scores = _dot_nt(qr.astype(bf), kr.astype(bf)) * tbl_ref[0, h]
            o = _dot(scores.astype(bf), v)
            o = o + _dot((qr * tbl_ref[1, h]).astype(bf), state.astype(bf))
            kv = _dot_tn((kr * tbl_ref[2, h]).astype(bf), v)
            state = state * math.exp(log_gamma * CHUNK) + kv

            ms = jnp.mean(o * o, axis=-1, keepdims=True)
            on = o * lax.rsqrt(ms + EPS) * gain_ref[:, sl]
            gate = g * (1.0 / (1.0 + jnp.exp(-g)))
            y_ref[0, j * CHUNK:(j + 1) * CHUNK, sl] = (gate * on).astype(y_ref.dtype)
        state_ref[h] = state


def _retention(r_all, nb, cos2, sin2, gain):
    B = r_all.shape[0]
    w = RET_HEADS * RET_DK
    nstep = (nb + 1) // 2
    rspec = lambda f: pl.BlockSpec((1, CHUNK, 4 * w), lambda b, s: (b, f(s), 0))
    tspec = lambda f: pl.BlockSpec((CHUNK, 128), lambda b, s: (f(s), 0))
    first = lambda s: jnp.maximum(2 * s - 1, 0)
    second = lambda s: 2 * s
    return pl.pallas_call(
        _retention_kernel,
        grid=(B, nstep),
        in_specs=[rspec(first), rspec(second), tspec(first), tspec(first), tspec(second),
                  tspec(second), pl.BlockSpec((1, w), lambda b, s: (0, 0))],
        out_specs=pl.BlockSpec((1, 2 * CHUNK, w), lambda b, s: (b, jnp.maximum(s - 1, 0), 0)),
        out_shape=jax.ShapeDtypeStruct((B, (nb - 1) * CHUNK, w), jnp.bfloat16),
        scratch_shapes=[pltpu.VMEM((RET_HEADS, RET_DK, RET_DK), jnp.float32),
                        pltpu.VMEM((3, RET_HEADS, CHUNK, CHUNK), jnp.float32)],
        compiler_params=pltpu.CompilerParams(
            dimension_semantics=("arbitrary", "arbitrary"), vmem_limit_bytes=VMEM_LIMIT),
        name="retention",
    )(r_all, r_all, cos2, sin2, cos2, sin2, gain)


def _bit_planes(tile):
    a = [tile[r * 8:(r + 1) * 8, :] for r in range(32)]
    j, m = 16, 0x0000FFFF
    while j:
        for k in range(32):
            if (k & j) == 0:
                t = (a[k] ^ lax.shift_right_logical(a[k + j], j)) & m
                a[k] = a[k] ^ t
                a[k + j] = a[k + j] ^ (t << j)
        j >>= 1
        if j:
            m = m ^ ((m << j) & 0xFFFFFFFF)
    return a


def _sublane_total(x):
    x = x + pltpu.roll(x, 4, 0)
    x = x + pltpu.roll(x, 2, 0)
    return x + pltpu.roll(x, 1, 0)


def _dsa_kernel(qa_ref, qb_ref, iwt_ref, k1_ref, k2_ref, vt_ref, bias_ref, tri_ref, y_ref,
                key_ref, plane_ref, thr_ref, ntie_ref, flag_ref, m_ref, acc_ref, sa_ref, sb_ref,
                rhs_ref, *, topk):
    t = pl.program_id(1)
    last_q = pl.num_programs(1) - 1
    bf = jnp.bfloat16
    H = DSA_HEADS
    KP = 2 * CHUNK
    G = plane_ref.shape[1]
    cur, nxt = t % 2, (t + 1) % 2

    @pl.when((pl.program_id(0) == 0) & (t == 0))
    def _():
        plane_ref[...] = jnp.zeros_like(plane_ref)
        key_ref[...] = jnp.zeros_like(key_ref)
        eye = (lax.broadcasted_iota(jnp.int32, (CHUNK, CHUNK), 0)
               == lax.broadcasted_iota(jnp.int32, (CHUNK, CHUNK), 1))
        for h in range(DSA_HEADS):
            rhs_ref[CHUNK:2 * CHUNK, h * 128:(h + 1) * 128] = jnp.where(eye, 1.0, 0.0).astype(bf)

    @pl.when(t == 0)
    def _():
        thr_ref[0] = jnp.zeros(thr_ref.shape[1:], jnp.int32)
        ntie_ref[0] = jnp.zeros(ntie_ref.shape[1:], jnp.float32)
        flag_ref[0] = 0

    row = lax.broadcasted_iota(jnp.int32, (KP, CHUNK), 0)
    col = lax.broadcasted_iota(jnp.int32, (KP, CHUNK), 1)
    hs = lambda h: slice(h * 128, (h + 1) * 128)
    hp = lambda c: slice(c * 256, (c + 1) * 256)
    nsteps = lambda n: (n + 2) // 2

    def key_rows(g):
        last = k1_ref.shape[1] // KP - 1
        return pl.ds(pl.multiple_of(jnp.minimum(g, last) * KP, KP), KP)

    nb_ = t + 1
    steps_b = jnp.where(t < last_q, nsteps(nb_), 0)
    iters_b, tail_b = steps_b // 2, steps_b % 2
    qb_idx = nb_ * CHUNK + col
    iw_rows = [jnp.broadcast_to(iwt_ref[0, 0, h:h + 1, :], (KP, CHUNK)) for h in range(IDX_HEADS)]

    def score_pair(g):
        kblk = k2_ref[0, key_rows(g), :]
        sc = None
        for c in range(H // 2):
            s = _dot(kblk, qb_ref[0, 0, :, hp(c)])
            for hh in range(2):
                term = iw_rows[2 * c + hh] * jnp.maximum(s[:, hs(hh)], 0.0)
                sc = term if sc is None else sc + term
        k_idx = g * KP + row
        valid = (k_idx <= qb_idx) & (k_idx >= PAD)
        sc = jnp.where(valid, sc, NEG)
        bits = lax.bitcast_convert_type(sc, jnp.int32)
        key = jnp.where(bits < 0, jnp.int32(INT_MIN) - bits, bits)
        key_ref[nxt, g] = key
        planes = _bit_planes(key ^ jnp.int32(INT_MIN))
        for p in range(32):
            plane_ref[p, g] = planes[p]

    def score_two(i):
        score_pair(2 * i)
        score_pair(2 * i + 1)

    na_ = t
    steps_a = jnp.where(t > 0, nsteps(na_), 0)
    iters_a, tail_a = steps_a // 2, steps_a % 2
    far_a = jnp.where(t > 0, ((na_ - 1) // 2) // 2, 0)
    neg_key = -(int(np.float32(NEG).view(np.int32)) & 0x7FFFFFFF)
    rep = lambda x: jnp.tile(x, (KP // 8, 1))
    thr = thr_ref[cur]
    n_tie = ntie_ref[cur]
    masked_thr = thr == neg_key
    thr_b = rep(thr)
    thr_sel_b = rep(jnp.where(masked_thr, jnp.int32(neg_key + 1), thr))
    n_tie_b = rep(n_tie)
    has_ties = flag_ref[cur] > 0
    va = acc_ref.shape[0]

    @pl.when(t > 0)
    def _():
        m_ref[...] = jnp.full_like(m_ref, NEG)
        acc_ref[...] = jnp.zeros_like(acc_ref)
        rhs_ref[0:CHUNK, :] = qa_ref[0, 0]

    def logits(g, dst_ref, ties):
        kblk = k1_ref[0, key_rows(g), :]
        if ties:
            dst_ref[...] = _dot(kblk, qa_ref[0, 0])
        else:
            off = jnp.where(key_ref[cur, g] >= thr_sel_b, 0.0, NEG).astype(bf)
            dst_ref[...] = _dot(jnp.concatenate([kblk, off], axis=1), rhs_ref[...])

    def attend(g, s_ref, ties_before, near, ties):
        if ties:
            key = key_ref[cur, g]
            eq = key == thr_b
            eqf = jnp.where(eq, 1.0, 0.0)
            before = ties_before + _dot(tri_ref[...], eqf.astype(bf))
            sel = (key > thr_b) | (eq & (before < n_tie_b))
            ties_before = ties_before + jnp.sum(eqf, axis=0, keepdims=True)
        ps, alphas = [], []
        for h in range(H):
            s = s_ref[:, hs(h)]
            if near:
                ta = jnp.clip(na_ - 2 * g, 0, 2)
                tb = jnp.clip(na_ - 2 * g - 1, 0, 2)
                s = s + jnp.concatenate([bias_ref[ta, :, hs(h)], bias_ref[tb, :, hs(h)]], axis=0)
            if ties:
                s = jnp.where(sel, s, NEG)
            m_prev = m_ref[h:h + 1, :]
            m_new = jnp.maximum(m_prev, jnp.max(s, axis=0, keepdims=True))
            m_ref[h:h + 1, :] = m_new
            alphas.append(jnp.exp2(m_prev - m_new))
            ps.append(jnp.exp2(s - m_new).astype(bf))
        pv = _dot(vt_ref[0, 0:va, key_rows(g)], jnp.concatenate(ps, axis=1))
        acc_ref[...] = acc_ref[...] * jnp.concatenate(alphas, axis=1) + pv
        return ties_before

    def attend_two(i, ties_before, near, ties, fused):
        logits(2 * i + 1, sb_ref, ties)
        ties_before = attend(2 * i, sa_ref, ties_before, near, ties)
        logits(2 * i + 2, sa_ref, ties)
        ties_before = attend(2 * i + 1, sb_ref, ties_before, near, ties)
        if fused:
            score_two(i)
        return ties_before

    zero_ties = jnp.zeros((KP, CHUNK), jnp.float32)
    loop = lambda lo, hi, **kw: (lambda c: lax.fori_loop(lo, hi, functools.partial(attend_two, **kw), c))
    fused_hi = jnp.minimum(iters_a, iters_b)

    @pl.when(has_ties)
    def _():
        logits(0, sa_ref, True)
        c = loop(0, far_a, near=False, ties=True, fused=False)(zero_ties)
        c = loop(far_a, iters_a, near=True, ties=True, fused=False)(c)

        @pl.when(tail_a == 1)
        def _():
            attend(2 * iters_a, sa_ref, c, True, True)

    @pl.when(jnp.logical_not(has_ties))
    def _():
        logits(0, sa_ref, False)
        far_hi = jnp.minimum(far_a, fused_hi)
        loop(0, far_hi, near=False, ties=False, fused=True)(zero_ties)
        loop(far_hi, fused_hi, near=True, ties=False, fused=True)(zero_ties)
        loop(fused_hi, iters_a, near=True, ties=False, fused=False)(zero_ties)

        @pl.when(tail_a == 1)
        def _():
            attend(2 * iters_a, sa_ref, zero_ties, True, False)

    lax.fori_loop(jnp.where(has_ties, 0, fused_hi), iters_b, lambda i, _: (score_two(i), 0)[1], 0)

    @pl.when(tail_b == 1)
    def _():
        score_pair(2 * iters_b)

    @pl.when(t > 0)
    def _():
        lo = lax.broadcasted_iota(jnp.int32, (CHUNK, CHUNK), 1) < HALF
        for c in range(H // 2):
            tiles = []
            for h in (2 * c, 2 * c + 1):
                a = acc_ref[:, hs(h)] * (1.0 / acc_ref[HALF:HALF + 1, hs(h)])
                a = jnp.concatenate([a, jnp.zeros((CHUNK - va, CHUNK), jnp.float32)], axis=0)
                tiles.append(a.T)
            y_ref[0, :, hs(c)] = jnp.where(lo, tiles[0], pltpu.roll(tiles[1], HALF, 1)).astype(y_ref.dtype)

    @pl.when(t < last_q)
    def _():
        npair_b = (nb_ + 2) // 2
        g_idx = lax.broadcasted_iota(jnp.int32, (G, 8, CHUNK), 0)
        alive0 = jnp.where(g_idx < npair_b, jnp.int32(-1), jnp.int32(0))

        def ones_in(words):
            return _sublane_total(lax.population_count(words).sum(axis=0))

        def decide(above, thr_u, count, bit):
            take = (above + count) >= topk
            return take, jnp.where(take, above, above + count), thr_u | jnp.where(take, bit, 0)

        def radix2(first, carry):
            alive, above, thr_u = carry
            bit = lax.shift_right_logical(jnp.int32(INT_MIN), first)
            p0, p1 = plane_ref[first], plane_ref[first + 1]
            a1 = alive & p0
            a0 = alive ^ a1
            l1, l0 = a1 & p1, a0 & p1
            c1, c11, c01 = ones_in(a1), ones_in(l1), ones_in(l0)
            b0, above, thr_u = decide(above, thr_u, c1, bit)
            b1, above, thr_u = decide(above, thr_u, jnp.where(b0, c11, c01),
                                      lax.shift_right_logical(bit, 1))
            group = jnp.where(b0[None], a1, a0)
            last = jnp.where(b0[None], l1, l0)
            return jnp.where(b1[None], last, group ^ last), above, thr_u

        def radix3(i, carry):
            alive, above, thr_u = carry
            first = 2 + 3 * i
            bit = lax.shift_right_logical(jnp.int32(INT_MIN), first)
            p0, p1, p2 = plane_ref[first], plane_ref[first + 1], plane_ref[first + 2]
            a1 = alive & p0
            a0 = alive ^ a1
            a11, a01 = a1 & p1, a0 & p1
            a10, a00 = a1 ^ a11, a0 ^ a01
            l11, l10, l01, l00 = a11 & p2, a10 & p2, a01 & p2, a00 & p2
            c1, c11, c01 = ones_in(a1), ones_in(a11), ones_in(a01)
            c111, c101, c011, c001 = ones_in(l11), ones_in(l10), ones_in(l01), ones_in(l00)
            b0, above, thr_u = decide(above, thr_u, c1, bit)
            b1, above, thr_u = decide(above, thr_u, jnp.where(b0, c11, c01),
                                      lax.shift_right_logical(bit, 1))
            c = jnp.where(b0, jnp.where(b1, c111, c101), jnp.where(b1, c011, c001))
            b2, above, thr_u = decide(above, thr_u, c, lax.shift_right_logical(bit, 2))
            e0, e1 = b0[None], b1[None]
            group = jnp.where(e0, jnp.where(e1, a11, a10), jnp.where(e1, a01, a00))
            last = jnp.where(e0, jnp.where(e1, l11, l10), jnp.where(e1, l01, l00))
            return jnp.where(b2[None], last, group ^ last), above, thr_u

        zero8 = jnp.zeros((8, CHUNK), jnp.int32)
        alive, n_gt, thr_u = lax.fori_loop(0, 10, radix3, radix2(0, (alive0, zero8, zero8)))
        n_gt = n_gt.astype(jnp.float32)
        thr_n = thr_u ^ jnp.int32(INT_MIN)
        n_eq = ones_in(alive).astype(jnp.float32)
        masked_n = thr_n == neg_key
        n_tie_n = jnp.where(masked_n, 0.0, topk - n_gt)
        thr_ref[nxt] = thr_n
        ntie_ref[nxt] = n_tie_n
        more = jnp.max(jnp.where((n_eq > n_tie_n) & ~masked_n, 1.0, 0.0)) > 0.5
        flag_ref[nxt] = more.astype(jnp.int32)


def _sparse_attention(qit, iwt, k1, k2, vt, bias_tbl, tri, topk, nb):
    B, nbp, _, hw = qit.shape
    tp = nbp * CHUNK
    nstep = nbp // 2
    nstep += nstep % 2
    kspec = pl.BlockSpec((1, tp, 128), lambda b, t: (b, 0, 0))
    acc_rows = 80
    nxt = lambda t: jnp.minimum(t + 1, nb - 1)
    return pl.pallas_call(
        functools.partial(_dsa_kernel, topk=topk),
        grid=(B, nb),
        in_specs=[
            pl.BlockSpec((1, 1, 128, hw), lambda b, t: (b, t, 0, 0)),
            pl.BlockSpec((1, 1, 128, hw), lambda b, t: (b, nxt(t), 0, 0)),
            pl.BlockSpec((1, 1, IDX_HEADS, CHUNK), lambda b, t: (b, nxt(t), 0, 0)),
            kspec, kspec,
            pl.BlockSpec((1, 128, tp), lambda b, t: (b, 0, 0)),
            pl.BlockSpec(bias_tbl.shape, lambda b, t: (0, 0, 0)),
            pl.BlockSpec(tri.shape, lambda b, t: (0, 0)),
        ],
        out_specs=pl.BlockSpec((1, CHUNK, DSA_HEADS * DSA_DH),
                               lambda b, t: (b, jnp.maximum(t - 1, 0), 0)),
        out_shape=jax.ShapeDtypeStruct((B, (nb - 1) * CHUNK, DSA_HEADS * DSA_DH), jnp.bfloat16),
        scratch_shapes=[
            pltpu.VMEM((2, nstep, 2 * CHUNK, CHUNK), jnp.int32),
            pltpu.VMEM((32, nstep, 8, CHUNK), jnp.int32),
            pltpu.VMEM((2, 8, CHUNK), jnp.int32),
            pltpu.VMEM((2, 8, CHUNK), jnp.float32),
            pltpu.SMEM((2,), jnp.int32),
            pltpu.VMEM((DSA_HEADS, CHUNK), jnp.float32),
            pltpu.VMEM((acc_rows, hw), jnp.float32),
            pltpu.VMEM((2 * CHUNK, hw), jnp.float32),
            pltpu.VMEM((2 * CHUNK, hw), jnp.float32),
            pltpu.VMEM((2 * CHUNK, hw), jnp.bfloat16),
        ],
        compiler_params=pltpu.CompilerParams(
            dimension_semantics=("arbitrary", "arbitrary"), vmem_limit_bytes=VMEM_LIMIT),
        name="sparse_attention",
    )(qit, qit, iwt, k1, k2, vt, bias_tbl, tri)


def _mlp_kernel(x_ref, yr_ref, yd_ref, wor_ref, wod_ref, n2_ref, w1_ref, w2_ref, o_ref):
    h1 = x_ref[...] + _dot(yr_ref[...], wor_ref[...]) + _dot(yd_ref[...], wod_ref[...])
    ms = jnp.mean(h1 * h1, axis=-1, keepdims=True)
    u = (h1 * lax.rsqrt(ms + EPS) * n2_ref[...]).astype(jnp.bfloat16)
    o_ref[...] = h1
    d_ff = w1_ref.shape[1]
    for c in range(d_ff // FFN_CHUNK):
        sl = slice(c * FFN_CHUNK, (c + 1) * FFN_CHUNK)
        f = jnp.maximum(_dot(u, w1_ref[:, sl]), 0.0)
        o_ref[...] += _dot((f * f).astype(jnp.bfloat16), w2_ref[sl, :])


def _out_mlp(x2, yr, yd, wor, wod, n2, w1, w2):
    rows, D = x2.shape
    const = lambda a: pl.BlockSpec(a.shape, lambda i: (0, 0), pipeline_mode=pl.Buffered(1))
    tile = lambda w: pl.BlockSpec((ROW_TILE, w), lambda i: (i, 0))
    return pl.pallas_call(
        _mlp_kernel,
        grid=(rows // ROW_TILE,),
        in_specs=[tile(D), tile(yr.shape[1]), tile(yd.shape[1]),
                  const(wor), const(wod), const(n2), const(w1), const(w2)],
        out_specs=tile(D),
        out_shape=jax.ShapeDtypeStruct((rows, D), jnp.float32),
        compiler_params=pltpu.CompilerParams(
            dimension_semantics=("arbitrary",), vmem_limit_bytes=VMEM_LIMIT),
        name="out_mlp",
    )(x2, yr, yd, wor, wod, n2, w1, w2)


def kernel(x, meta_tokens, norm1_w, w_in, ret_norm_w, q_norm_w, k_norm_w, rel_bias,
           w_out, norm2_w, w_ff1, w_ff2):
    B, L, D = x.shape
    assert L % (2 * CHUNK) == 0 and L % ROW_TILE == 0 and w_in.shape[0] == 1
    topk = min(TOPK_MAX, L // 4)
    nb = L // CHUNK + 1
    bf = jnp.bfloat16
    f32 = jnp.float32

    rw = RET_HEADS * RET_DK
    dw = DSA_HEADS * DSA_DH
    sizes = (rw, rw, rw, rw, dw, DSA_DH, DSA_DH, IDX_HEADS * DSA_DH, DSA_DH, IDX_HEADS)
    offs = np.concatenate([[0], np.cumsum(sizes)])
    col = lambda i: w_in[0][:, int(offs[i]):int(offs[i + 1])]
    wr = jnp.concatenate([col(0), col(1), col(2), col(3)], axis=1).astype(bf)
    wq = col(4).reshape(D, DSA_HEADS, DSA_DH)
    wiq = col(7).reshape(D, IDX_HEADS, DSA_DH)
    wqi = jnp.concatenate([wq, wiq], axis=2).reshape(D, DSA_HEADS * 128).astype(bf)
    wk = jnp.concatenate([col(5), col(8)], axis=1).astype(bf)
    wv = jnp.concatenate([col(6), jnp.zeros((D, 128 - DSA_DH), f32)], axis=1).astype(bf)
    ww = jnp.concatenate([col(9), jnp.zeros((D, 128 - IDX_HEADS), f32)], axis=1).astype(bf)

    idx_scale = jnp.full((DSA_DH,), DSA_DH ** -0.5, f32)
    qscale = jnp.concatenate([q_norm_w[0].astype(f32) * (DSA_DH ** -0.5 * LOG2E), idx_scale])[None]
    kscale = jnp.concatenate([k_norm_w[0].astype(f32), jnp.ones((DSA_DH,), f32)])[None]
    meta_pad = jnp.concatenate([jnp.zeros((PAD, D), x.dtype), meta_tokens.astype(x.dtype)], axis=0)

    cos2, sin2 = _rope_tables(nb)
    bias_tbl = _bias_tables(rel_bias)
    tri = jnp.asarray(np.tril(np.ones((2 * CHUNK, 2 * CHUNK), np.float32), k=-1), dtype=bf)

    r_all, qit, k1, k2, vt, iwt = _inproj(x, meta_pad, norm1_w[0][None].astype(f32),
                                          wr, wqi, wk, wv, ww, qscale, kscale)
    y_ret = _retention(r_all, nb, cos2, sin2, ret_norm_w[0][None].astype(f32))
    y_dsa = _sparse_attention(qit, iwt, k1, k2, vt, bias_tbl, tri, topk, nb)

    wo = w_out[0].astype(bf)
    out = _out_mlp(x.reshape(B * L, D), y_ret.reshape(B * L, rw), y_dsa.reshape(B * L, dw),
                   wo[:rw], wo[rw:], norm2_w[0][None].astype(f32),
                   w_ff1[0].astype(bf), w_ff2[0].astype(bf))
    return out.reshape(B, L, D)
```

```python
import functools
import math

import numpy as np
import jax
import jax.numpy as jnp
from jax import lax
from jax.experimental import pallas as pl
from jax.experimental.pallas import tpu as pltpu

N_META = 16
CHUNK = 128
RET_HEADS = 4
RET_DK = 128
DSA_HEADS = 8
DSA_DH = 64
IDX_HEADS = 8
TOPK_MAX = 256
N_BUCKETS = 32
MAX_DISTANCE = 128
ROPE_BASE = 10000.0
EPS = 1e-6
NEG = -1e30
PAD = CHUNK - N_META
LANES = 128
SUBLANES = 8
BF16_ROWS = 16
HALF = LANES // 2
INT_MIN = -(2 ** 31)
LOG2E = math.log2(math.e)

FFN_CHUNK = 1024
ROW_TILE = 512
VMEM_LIMIT = 56 * 1024 * 1024


def _dot(a, b):
    return jnp.dot(a, b, preferred_element_type=jnp.float32)


def _dot_nt(a, b):
    return lax.dot_general(a, b, (((1,), (1,)), ((), ())), preferred_element_type=jnp.float32)


def _dot_tn(a, b):
    return lax.dot_general(a, b, (((0,), (0,)), ((), ())), preferred_element_type=jnp.float32)


def _bucket_ranges():
    max_exact = N_BUCKETS // 2
    d = np.arange(0, 2 * CHUNK)
    large = max_exact + (np.log(np.maximum(d, 1) / max_exact) / math.log(MAX_DISTANCE / max_exact)
                         * (N_BUCKETS - max_exact)).astype(np.int64)
    bucket = np.where(d < max_exact, d, np.minimum(large, N_BUCKETS - 1))
    out = []
    for b in range(N_BUCKETS - 1):
        idx = np.nonzero(bucket == b)[0]
        out.append((int(idx.min()), int(idx.max())))
    return out


def _rope_kernel(cos_ref, sin_ref):
    n = pl.program_id(0)
    row = lax.broadcasted_iota(jnp.int32, (CHUNK, LANES), 0)
    lane = lax.broadcasted_iota(jnp.int32, (CHUNK, LANES), 1)
    pos = (n * CHUNK + row - PAD).astype(jnp.float32)
    frac = (lane % HALF).astype(jnp.float32) / HALF
    inv = jnp.exp(-frac * math.log(ROPE_BASE))
    ang = pos * inv
    cos_ref[...] = jnp.cos(ang)
    s = jnp.sin(ang)
    sin_ref[...] = jnp.where(lane < HALF, -s, s)


def _rope_tables(nb):
    return pl.pallas_call(
        _rope_kernel,
        grid=(nb,),
        out_specs=[pl.BlockSpec((CHUNK, LANES), lambda n: (n, 0))] * 2,
        out_shape=[jax.ShapeDtypeStruct((nb * CHUNK, LANES), jnp.float32)] * 2,
        name="rope_tables",
    )()


def _bias_kernel(rb_ref, out_ref):
    row = lax.broadcasted_iota(jnp.int32, (CHUNK, CHUNK), 0)
    col = lax.broadcasted_iota(jnp.int32, (CHUNK, CHUNK), 1)
    ranges = _bucket_ranges()
    out_ref[2] = jnp.zeros(out_ref.shape[1:], jnp.float32)
    for h in range(DSA_HEADS):
        far = rb_ref[N_BUCKETS - 1, h]
        for t in range(2):
            dist = col - row + t * CHUNK
            tile = jnp.zeros((CHUNK, CHUNK), jnp.float32)
            for b, (lo, hi) in enumerate(ranges):
                tile = jnp.where((dist >= lo) & (dist <= hi), (rb_ref[b, h] - far) * LOG2E, tile)
            out_ref[t, :, h * LANES:(h + 1) * LANES] = tile


def _bias_tables(rel_bias):
    return pl.pallas_call(
        _bias_kernel,
        in_specs=[pl.BlockSpec(memory_space=pltpu.SMEM)],
        out_specs=pl.BlockSpec(memory_space=pltpu.VMEM),
        out_shape=jax.ShapeDtypeStruct((3, CHUNK, DSA_HEADS * LANES), jnp.float32),
        name="bias_tables",
    )(rel_bias.astype(jnp.float32))


def _inproj_kernel(xa_ref, xb_ref, meta_ref, n1_ref, wr_ref, wqi_ref, wk_ref, wv_ref, ww_ref,
                   qs_ref, ks_ref,
                   r_ref, qit_ref, k1_ref, k2_ref, vt_ref, iwt_ref, *, pad_last):
    s = pl.program_id(1)
    top = jnp.where(s == 0, meta_ref[...], xa_ref[0])
    bot = xb_ref[0]
    if pad_last:
        bot = jnp.where(s == pl.num_programs(1) - 1, 0.0, bot)
    src = jnp.concatenate([top, bot], axis=0)
    ms = jnp.mean(src * src, axis=-1, keepdims=True)
    u = (src * lax.rsqrt(ms + EPS) * n1_ref[...]).astype(jnp.bfloat16)

    r_ref[0] = _dot(u, wr_ref[...])

    lane = lax.broadcasted_iota(jnp.int32, (2 * CHUNK, LANES), 1)
    lo = lane < HALF

    def head_rms(t):
        ssq = jnp.sum(jnp.where(lo, t * t, 0.0), axis=-1, keepdims=True)
        return lax.rsqrt(ssq / DSA_DH + EPS)

    pq = _dot(u, wqi_ref[...])
    for h in range(DSA_HEADS):
        t = pq[:, h * LANES:(h + 1) * LANES]
        t = t * (jnp.where(lo, head_rms(t), 1.0) * qs_ref[...])
        for j in range(2):
            qit_ref[0, j, :, h * LANES:(h + 1) * LANES] = (
                t[j * CHUNK:(j + 1) * CHUNK].T.astype(jnp.bfloat16))

    pk = _dot(u, wk_ref[...])
    kn = pk * head_rms(pk) * ks_ref[...]
    k1_ref[0] = jnp.where(lo, kn, 0.0).astype(jnp.bfloat16)
    k2_ref[0] = jnp.where(lo, 0.0, pk).astype(jnp.bfloat16)

    pv = _dot(u, wv_ref[...])
    vx = jnp.where(lo, pv, jnp.where(lane == HALF, 1.0, 0.0))
    iw = _dot(u, ww_ref[...]) * (IDX_HEADS ** -0.5)
    for j in range(2):
        rows = slice(j * CHUNK, (j + 1) * CHUNK)
        vt_ref[0, :, rows] = vx[rows].T.astype(jnp.bfloat16)
        iwt_ref[0, j] = iw[rows].T[0:IDX_HEADS, :]


def _inproj(x, meta_pad, n1, wr, wqi, wk, wv, ww, qscale, kscale):
    B, L, D = x.shape
    nx = L // CHUNK
    nbp = nx + 1 + ((nx + 1) % 2)
    tp = nbp * CHUNK
    const = lambda shape: pl.BlockSpec(shape, lambda b, s: (0,) * len(shape))
    blk = lambda w: pl.BlockSpec((1, 2 * CHUNK, w), lambda b, s: (b, s, 0))
    return pl.pallas_call(
        functools.partial(_inproj_kernel, pad_last=bool((nx + 1) % 2)),
        grid=(B, nbp // 2),
        in_specs=[
            pl.BlockSpec((1, CHUNK, D), lambda b, s: (b, jnp.maximum(2 * s - 1, 0), 0)),
            pl.BlockSpec((1, CHUNK, D), lambda b, s: (b, jnp.minimum(2 * s, nx - 1), 0)),
            const(meta_pad.shape), const(n1.shape), const(wr.shape), const(wqi.shape),
            const(wk.shape), const(wv.shape), const(ww.shape), const(qscale.shape),
            const(kscale.shape),
        ],
        out_specs=[
            blk(wr.shape[1]),
            pl.BlockSpec((1, 2, LANES, DSA_HEADS * LANES), lambda b, s: (b, s, 0, 0)),
            blk(LANES), blk(LANES),
            pl.BlockSpec((1, LANES, 2 * CHUNK), lambda b, s: (b, 0, s)),
            pl.BlockSpec((1, 2, IDX_HEADS, CHUNK), lambda b, s: (b, s, 0, 0)),
        ],
        out_shape=[
            jax.ShapeDtypeStruct((B, tp, wr.shape[1]), jnp.float32),
            jax.ShapeDtypeStruct((B, nbp, LANES, DSA_HEADS * LANES), jnp.bfloat16),
            jax.ShapeDtypeStruct((B, tp, LANES), jnp.bfloat16),
            jax.ShapeDtypeStruct((B, tp, LANES), jnp.bfloat16),
            jax.ShapeDtypeStruct((B, LANES, tp), jnp.bfloat16),
            jax.ShapeDtypeStruct((B, nbp, IDX_HEADS, CHUNK), jnp.float32),
        ],
        compiler_params=pltpu.CompilerParams(
            dimension_semantics=("arbitrary", "arbitrary"), vmem_limit_bytes=VMEM_LIMIT),
        name="in_projection",
    )(x, x, meta_pad, n1, wr, wqi, wk, wv, ww, qscale, kscale)


def _retention_kernel(ra_ref, rb_ref, cosa_ref, sina_ref, cosb_ref, sinb_ref, gain_ref, y_ref,
                      state_ref, tbl_ref):
    s = pl.program_id(1)
    w = RET_HEADS * RET_DK
    bf = jnp.bfloat16

    @pl.when(s == 0)
    def _():
        state_ref[...] = jnp.zeros_like(state_ref)
        row = lax.broadcasted_iota(jnp.int32, (CHUNK, CHUNK), 0)
        col = lax.broadcasted_iota(jnp.int32, (CHUNK, CHUNK), 1)
        rowf = row.astype(jnp.float32)
        diff = (row - col).astype(jnp.float32)
        for h in range(RET_HEADS):
            log_gamma = math.log(1.0 - 2.0 ** (-5.0 - h))
            tbl_ref[0, h] = jnp.where(diff >= 0, jnp.exp(log_gamma * jnp.maximum(diff, 0.0)), 0.0)
            tbl_ref[1, h] = jnp.exp(log_gamma * (rowf + 1.0))
            tbl_ref[2, h] = jnp.exp(log_gamma * (CHUNK - 1.0 - rowf))

    live = jnp.where(s == 0, 0.0, 1.0)
    chunks = ((ra_ref, cosa_ref, sina_ref, live), (rb_ref, cosb_ref, sinb_ref, None))
    for h in range(RET_HEADS):
        log_gamma = math.log(1.0 - 2.0 ** (-5.0 - h))
        sl = slice(h * RET_DK, (h + 1) * RET_DK)
        state = state_ref[h]
        for j, (r_ref, cos_ref, sin_ref, scale) in enumerate(chunks):
            cos2, sin2 = cos_ref[...], sin_ref[...]
            q = r_ref[0, :, sl]
            k = r_ref[0, :, w + h * RET_DK: w + (h + 1) * RET_DK]
            v = r_ref[0, :, 2 * w + h * RET_DK: 2 * w + (h + 1) * RET_DK]
            g = r_ref[0, :, 3 * w + h * RET_DK: 3 * w + (h + 1) * RET_DK]
            if scale is not None:
                k = k * scale
            v = v.astype(bf)
            qr = q * cos2 + pltpu.roll(q, HALF, 1) * sin2
            kr = (k * cos2 + pltpu.roll(k, HALF, 1) * sin2) * (RET_DK ** -0.5)

            scores = _dot_nt(qr.astype(bf), kr.astype(bf)) * tbl_ref[0, h]
            o = _dot(scores.astype(bf), v)
            o = o + _dot((qr * tbl_ref[1, h]).astype(bf), state.astype(bf))
            kv = _dot_tn((kr * tbl_ref[2, h]).astype(bf), v)
            state = state * math.exp(log_gamma * CHUNK) + kv

            ms = jnp.mean(o * o, axis=-1, keepdims=True)
            on = o * lax.rsqrt(ms + EPS) * gain_ref[:, sl]
            gate = g * (1.0 / (1.0 + jnp.exp(-g)))
            y_ref[0, j * CHUNK:(j + 1) * CHUNK, sl] = (gate * on).astype(y_ref.dtype)
        state_ref[h] = state


def _retention(r_all, nb, cos2, sin2, gain):
    B = r_all.shape[0]
    w = RET_HEADS * RET_DK
    nstep = (nb + 1) // 2
    rspec = lambda f: pl.BlockSpec((1, CHUNK, 4 * w), lambda b, s: (b, f(s), 0))
    tspec = lambda f: pl.BlockSpec((CHUNK, LANES), lambda b, s: (f(s), 0))
    first = lambda s: jnp.maximum(2 * s - 1, 0)
    second = lambda s: 2 * s
    return pl.pallas_call(
        _retention_kernel,
        grid=(B, nstep),
        in_specs=[rspec(first), rspec(second), tspec(first), tspec(first), tspec(second),
                  tspec(second), pl.BlockSpec((1, w), lambda b, s: (0, 0))],
        out_specs=pl.BlockSpec((1, 2 * CHUNK, w), lambda b, s: (b, jnp.maximum(s - 1, 0), 0)),
        out_shape=jax.ShapeDtypeStruct((B, (nb - 1) * CHUNK, w), jnp.bfloat16),
        scratch_shapes=[pltpu.VMEM((RET_HEADS, RET_DK, RET_DK), jnp.float32),
                        pltpu.VMEM((3, RET_HEADS, CHUNK, CHUNK), jnp.float32)],
        compiler_params=pltpu.CompilerParams(
            dimension_semantics=("arbitrary", "arbitrary"), vmem_limit_bytes=VMEM_LIMIT),
        name="retention",
    )(r_all, r_all, cos2, sin2, cos2, sin2, gain)


def _bit_planes(tile):
    a = [tile[r * 8:(r + 1) * 8, :] for r in range(32)]
    j, m = 16, 0x0000FFFF
    while j:
        for k in range(32):
            if (k & j) == 0:
                t = (a[k] ^ lax.shift_right_logical(a[k + j], j)) & m
                a[k] = a[k] ^ t
                a[k + j] = a[k + j] ^ (t << j)
        j >>= 1
        if j:
            m = m ^ ((m << j) & 0xFFFFFFFF)
    return a


def _sublane_total(x):
    x = x + pltpu.roll(x, 4, 0)
    x = x + pltpu.roll(x, 2, 0)
    return x + pltpu.roll(x, 1, 0)


def _dsa_kernel(qa_ref, qb_ref, iwt_ref, k1_ref, k2_ref, vt_ref, bias_ref, tri_ref, y_ref,
                key_ref, plane_ref, thr_ref, ntie_ref, flag_ref, m_ref, acc_ref, sa_ref, sb_ref,
                rhs_ref, *, topk):
    t = pl.program_id(1)
    last_q = pl.num_programs(1) - 1
    bf = jnp.bfloat16
    H = DSA_HEADS
    KP = 2 * CHUNK
    G = plane_ref.shape[1]
    cur, nxt = t % 2, (t + 1) % 2

    @pl.when((pl.program_id(0) == 0) & (t == 0))
    def _():
        plane_ref[...] = jnp.zeros_like(plane_ref)
        key_ref[...] = jnp.zeros_like(key_ref)
        eye = (lax.broadcasted_iota(jnp.int32, (CHUNK, CHUNK), 0)
               == lax.broadcasted_iota(jnp.int32, (CHUNK, CHUNK), 1))
        for h in range(DSA_HEADS):
            rhs_ref[CHUNK:2 * CHUNK, h * LANES:(h + 1) * LANES] = jnp.where(eye, 1.0, 0.0).astype(bf)

    @pl.when(t == 0)
    def _():
        thr_ref[0] = jnp.zeros(thr_ref.shape[1:], jnp.int32)
        ntie_ref[0] = jnp.zeros(ntie_ref.shape[1:], jnp.float32)
        flag_ref[0] = 0

    row = lax.broadcasted_iota(jnp.int32, (KP, CHUNK), 0)
    col = lax.broadcasted_iota(jnp.int32, (KP, CHUNK), 1)
    hs = lambda h: slice(h * LANES, (h + 1) * LANES)
    hp = lambda c: slice(2 * c * LANES, 2 * (c + 1) * LANES)
    nsteps = lambda n: (n + 2) // 2

    def key_rows(g):
        last = k1_ref.shape[1] // KP - 1
        return pl.ds(pl.multiple_of(jnp.minimum(g, last) * KP, KP), KP)

    blk_b = t + 1
    steps_b = jnp.where(t < last_q, nsteps(blk_b), 0)
    iters_b, tail_b = steps_b // 2, steps_b % 2
    qb_idx = blk_b * CHUNK + col
    iw_rows = [jnp.broadcast_to(iwt_ref[0, 0, h:h + 1, :], (KP, CHUNK)) for h in range(IDX_HEADS)]

    def score_pair(g):
        kblk = k2_ref[0, key_rows(g), :]
        sc = None
        for c in range(H // 2):
            s = _dot(kblk, qb_ref[0, 0, :, hp(c)])
            for hh in range(2):
                term = iw_rows[2 * c + hh] * jnp.maximum(s[:, hs(hh)], 0.0)
                sc = term if sc is None else sc + term
        k_idx = g * KP + row
        valid = (k_idx <= qb_idx) & (k_idx >= PAD)
        sc = jnp.where(valid, sc, NEG)
        bits = lax.bitcast_convert_type(sc, jnp.int32)
        key = jnp.where(bits < 0, jnp.int32(INT_MIN) - bits, bits)
        key_ref[nxt, g] = key
        planes = _bit_planes(key ^ jnp.int32(INT_MIN))
        for p in range(32):
            plane_ref[p, g] = planes[p]

    def score_two(i):
        score_pair(2 * i)
        score_pair(2 * i + 1)

    blk_a = t
    steps_a = jnp.where(t > 0, nsteps(blk_a), 0)
    iters_a, tail_a = steps_a // 2, steps_a % 2
    far_a = jnp.where(t > 0, ((blk_a - 1) // 2) // 2, 0)
    neg_key = -(int(np.float32(NEG).view(np.int32)) & 0x7FFFFFFF)
    rep = lambda x: jnp.tile(x, (KP // SUBLANES, 1))
    thr = thr_ref[cur]
    n_tie = ntie_ref[cur]
    masked_thr = thr == neg_key
    thr_b = rep(thr)
    thr_sel_b = rep(jnp.where(masked_thr, jnp.int32(neg_key + 1), thr))
    n_tie_b = rep(n_tie)
    has_ties = flag_ref[cur] > 0
    va = acc_ref.shape[0]

    @pl.when(t > 0)
    def _():
        m_ref[...] = jnp.full_like(m_ref, NEG)
        acc_ref[...] = jnp.zeros_like(acc_ref)
        rhs_ref[0:CHUNK, :] = qa_ref[0, 0]

    def logits(g, dst_ref, ties):
        kblk = k1_ref[0, key_rows(g), :]
        if ties:
            dst_ref[...] = _dot(kblk, qa_ref[0, 0])
        else:
            off = jnp.where(key_ref[cur, g] >= thr_sel_b, 0.0, NEG).astype(bf)
            dst_ref[...] = _dot(jnp.concatenate([kblk, off], axis=1), rhs_ref[...])

    def attend(g, s_ref, ties_before, near, ties):
        if ties:
            key = key_ref[cur, g]
            eq = key == thr_b
            eqf = jnp.where(eq, 1.0, 0.0)
            before = ties_before + _dot(tri_ref[...], eqf.astype(bf))
            sel = (key > thr_b) | (eq & (before < n_tie_b))
            ties_before = ties_before + jnp.sum(eqf, axis=0, keepdims=True)
        ps, alphas = [], []
        for h in range(H):
            s = s_ref[:, hs(h)]
            if near:
                ta = jnp.clip(blk_a - 2 * g, 0, 2)
                tb = jnp.clip(blk_a - 2 * g - 1, 0, 2)
                s = s + jnp.concatenate([bias_ref[ta, :, hs(h)], bias_ref[tb, :, hs(h)]], axis=0)
            if ties:
                s = jnp.where(sel, s, NEG)
            m_prev = m_ref[h:h + 1, :]
            m_new = jnp.maximum(m_prev, jnp.max(s, axis=0, keepdims=True))
            m_ref[h:h + 1, :] = m_new
            alphas.append(jnp.exp2(m_prev - m_new))
            ps.append(jnp.exp2(s - m_new).astype(bf))
        pv = _dot(vt_ref[0, 0:va, key_rows(g)], jnp.concatenate(ps, axis=1))
        acc_ref[...] = acc_ref[...] * jnp.concatenate(alphas, axis=1) + pv
        return ties_before

    def attend_two(i, ties_before, near, ties, fused):
        logits(2 * i + 1, sb_ref, ties)
        ties_before = attend(2 * i, sa_ref, ties_before, near, ties)
        logits(2 * i + 2, sa_ref, ties)
        ties_before = attend(2 * i + 1, sb_ref, ties_before, near, ties)
        if fused:
            score_two(i)
        return ties_before

    zero_ties = jnp.zeros((KP, CHUNK), jnp.float32)
    loop = lambda lo, hi, **kw: (lambda c: lax.fori_loop(lo, hi, functools.partial(attend_two, **kw), c))
    fused_hi = jnp.minimum(iters_a, iters_b)

    @pl.when(has_ties)
    def _():
        logits(0, sa_ref, True)
        c = loop(0, far_a, near=False, ties=True, fused=False)(zero_ties)
        c = loop(far_a, iters_a, near=True, ties=True, fused=False)(c)

        @pl.when(tail_a == 1)
        def _():
            attend(2 * iters_a, sa_ref, c, True, True)

    @pl.when(jnp.logical_not(has_ties))
    def _():
        logits(0, sa_ref, False)
        far_hi = jnp.minimum(far_a, fused_hi)
        loop(0, far_hi, near=False, ties=False, fused=True)(zero_ties)
        loop(far_hi, fused_hi, near=True, ties=False, fused=True)(zero_ties)
        loop(fused_hi, iters_a, near=True, ties=False, fused=False)(zero_ties)

        @pl.when(tail_a == 1)
        def _():
            attend(2 * iters_a, sa_ref, zero_ties, True, False)

    lax.fori_loop(jnp.where(has_ties, 0, fused_hi), iters_b, lambda i, _: (score_two(i), 0)[1], 0)

    @pl.when(tail_b == 1)
    def _():
        score_pair(2 * iters_b)

    @pl.when(t > 0)
    def _():
        lo = lax.broadcasted_iota(jnp.int32, (CHUNK, CHUNK), 1) < HALF
        for c in range(H // 2):
            tiles = []
            for h in (2 * c, 2 * c + 1):
                a = acc_ref[:, hs(h)] * (1.0 / acc_ref[HALF:HALF + 1, hs(h)])
                a = jnp.concatenate([a, jnp.zeros((CHUNK - va, CHUNK), jnp.float32)], axis=0)
                tiles.append(a.T)
            y_ref[0, :, hs(c)] = jnp.where(lo, tiles[0], pltpu.roll(tiles[1], HALF, 1)).astype(y_ref.dtype)

    @pl.when(t < last_q)
    def _():
        npair_b = (blk_b + 2) // 2
        g_idx = lax.broadcasted_iota(jnp.int32, (G, 8, CHUNK), 0)
        alive0 = jnp.where(g_idx < npair_b, jnp.int32(-1), jnp.int32(0))

        def ones_in(words):
            return _sublane_total(lax.population_count(words).sum(axis=0))

        def radix(i, carry):
            alive, above, thr_u = carry
            hi_plane, lo_plane = plane_ref[2 * i], plane_ref[2 * i + 1]
            a1 = alive & hi_plane
            a0 = alive ^ a1
            a11, a01 = a1 & lo_plane, a0 & lo_plane
            c1, c11, c01 = ones_in(a1), ones_in(a11), ones_in(a01)
            hi = (above + c1) >= topk
            above = jnp.where(hi, above, above + c1)
            c_lo = jnp.where(hi, c11, c01)
            lo = (above + c_lo) >= topk
            above = jnp.where(lo, above, above + c_lo)
            bit = lax.shift_right_logical(jnp.int32(INT_MIN), 2 * i)
            thr_u = (thr_u | jnp.where(hi, bit, 0)
                     | jnp.where(lo, lax.shift_right_logical(bit, 1), 0))
            group = jnp.where(hi[None], a1, a0)
            with_lo = jnp.where(hi[None], a11, a01)
            alive = jnp.where(lo[None], with_lo, group ^ with_lo)
            return alive, above, thr_u

        zero8 = jnp.zeros((8, CHUNK), jnp.int32)
        alive, n_gt, thr_u = lax.fori_loop(0, 16, radix, (alive0, zero8, zero8))
        n_gt = n_gt.astype(jnp.float32)
        thr_n = thr_u ^ jnp.int32(INT_MIN)
        n_eq = ones_in(alive).astype(jnp.float32)
        masked_n = thr_n == neg_key
        n_tie_n = jnp.where(masked_n, 0.0, topk - n_gt)
        thr_ref[nxt] = thr_n
        ntie_ref[nxt] = n_tie_n
        more = jnp.max(jnp.where((n_eq > n_tie_n) & ~masked_n, 1.0, 0.0)) > 0.5
        flag_ref[nxt] = more.astype(jnp.int32)


def _sparse_attention(qit, iwt, k1, k2, vt, bias_tbl, tri, topk, nb):
    B, nbp, _, hw = qit.shape
    tp = nbp * CHUNK
    nstep = nbp // 2
    nstep += nstep % 2
    kspec = pl.BlockSpec((1, tp, LANES), lambda b, t: (b, 0, 0))
    acc_rows = DSA_DH + BF16_ROWS
    nxt = lambda t: jnp.minimum(t + 1, nb - 1)
    return pl.pallas_call(
        functools.partial(_dsa_kernel, topk=topk),
        grid=(B, nb),
        in_specs=[
            pl.BlockSpec((1, 1, LANES, hw), lambda b, t: (b, t, 0, 0)),
            pl.BlockSpec((1, 1, LANES, hw), lambda b, t: (b, nxt(t), 0, 0)),
            pl.BlockSpec((1, 1, IDX_HEADS, CHUNK), lambda b, t: (b, nxt(t), 0, 0)),
            kspec, kspec,
            pl.BlockSpec((1, LANES, tp), lambda b, t: (b, 0, 0)),
            pl.BlockSpec(bias_tbl.shape, lambda b, t: (0, 0, 0)),
            pl.BlockSpec(tri.shape, lambda b, t: (0, 0)),
        ],
        out_specs=pl.BlockSpec((1, CHUNK, DSA_HEADS * DSA_DH),
                               lambda b, t: (b, jnp.maximum(t - 1, 0), 0)),
        out_shape=jax.ShapeDtypeStruct((B, (nb - 1) * CHUNK, DSA_HEADS * DSA_DH), jnp.bfloat16),
        scratch_shapes=[
            pltpu.VMEM((2, nstep, 2 * CHUNK, CHUNK), jnp.int32),
            pltpu.VMEM((32, nstep, 8, CHUNK), jnp.int32),
            pltpu.VMEM((2, 8, CHUNK), jnp.int32),
            pltpu.VMEM((2, 8, CHUNK), jnp.float32),
            pltpu.SMEM((2,), jnp.int32),
            pltpu.VMEM((DSA_HEADS, CHUNK), jnp.float32),
            pltpu.VMEM((acc_rows, hw), jnp.float32),
            pltpu.VMEM((2 * CHUNK, hw), jnp.float32),
            pltpu.VMEM((2 * CHUNK, hw), jnp.float32),
            pltpu.VMEM((2 * CHUNK, hw), jnp.bfloat16),
        ],
        compiler_params=pltpu.CompilerParams(
            dimension_semantics=("arbitrary", "arbitrary"), vmem_limit_bytes=VMEM_LIMIT),
        name="sparse_attention",
    )(qit, qit, iwt, k1, k2, vt, bias_tbl, tri)


def _mlp_kernel(x_ref, yr_ref, yd_ref, wor_ref, wod_ref, n2_ref, w1_ref, w2_ref, o_ref):
    h1 = x_ref[...] + _dot(yr_ref[...], wor_ref[...]) + _dot(yd_ref[...], wod_ref[...])
    ms = jnp.mean(h1 * h1, axis=-1, keepdims=True)
    u = (h1 * lax.rsqrt(ms + EPS) * n2_ref[...]).astype(jnp.bfloat16)
    o_ref[...] = h1
    d_ff = w1_ref.shape[1]
    for c in range(d_ff // FFN_CHUNK):
        sl = slice(c * FFN_CHUNK, (c + 1) * FFN_CHUNK)
        f = jnp.maximum(_dot(u, w1_ref[:, sl]), 0.0)
        o_ref[...] += _dot((f * f).astype(jnp.bfloat16), w2_ref[sl, :])


def _out_mlp(x2, yr, yd, wor, wod, n2, w1, w2):
    rows, D = x2.shape
    const = lambda a: pl.BlockSpec(a.shape, lambda i: (0, 0), pipeline_mode=pl.Buffered(1))
    tile = lambda w: pl.BlockSpec((ROW_TILE, w), lambda i: (i, 0))
    return pl.pallas_call(
        _mlp_kernel,
        grid=(rows // ROW_TILE,),
        in_specs=[tile(D), tile(yr.shape[1]), tile(yd.shape[1]),
                  const(wor), const(wod), const(n2), const(w1), const(w2)],
        out_specs=tile(D),
        out_shape=jax.ShapeDtypeStruct((rows, D), jnp.float32),
        compiler_params=pltpu.CompilerParams(
            dimension_semantics=("arbitrary",), vmem_limit_bytes=VMEM_LIMIT),
        name="out_mlp",
    )(x2, yr, yd, wor, wod, n2, w1, w2)


def kernel(x, meta_tokens, norm1_w, w_in, ret_norm_w, q_norm_w, k_norm_w, rel_bias,
           w_out, norm2_w, w_ff1, w_ff2):
    B, L, D = x.shape
    assert L % (2 * CHUNK) == 0 and L % ROW_TILE == 0 and w_in.shape[0] == 1
    topk = min(TOPK_MAX, L // 4)
    nb = L // CHUNK + 1
    bf = jnp.bfloat16
    f32 = jnp.float32

    rw = RET_HEADS * RET_DK
    dw = DSA_HEADS * DSA_DH
    sizes = (rw, rw, rw, rw, dw, DSA_DH, DSA_DH, IDX_HEADS * DSA_DH, DSA_DH, IDX_HEADS)
    offs = np.concatenate([[0], np.cumsum(sizes)])
    col = lambda i: w_in[0][:, int(offs[i]):int(offs[i + 1])]
    wr = jnp.concatenate([col(0), col(1), col(2), col(3)], axis=1).astype(bf)
    wq = col(4).reshape(D, DSA_HEADS, DSA_DH)
    wiq = col(7).reshape(D, IDX_HEADS, DSA_DH)
    wqi = jnp.concatenate([wq, wiq], axis=2).reshape(D, DSA_HEADS * LANES).astype(bf)
    wk = jnp.concatenate([col(5), col(8)], axis=1).astype(bf)
    wv = jnp.concatenate([col(6), jnp.zeros((D, LANES - DSA_DH), f32)], axis=1).astype(bf)
    ww = jnp.concatenate([col(9), jnp.zeros((D, LANES - IDX_HEADS), f32)], axis=1).astype(bf)

    idx_scale = jnp.full((DSA_DH,), DSA_DH ** -0.5, f32)
    qscale = jnp.concatenate([q_norm_w[0].astype(f32) * (DSA_DH ** -0.5 * LOG2E), idx_scale])[None]
    kscale = jnp.concatenate([k_norm_w[0].astype(f32), jnp.ones((DSA_DH,), f32)])[None]
    meta_pad = jnp.concatenate([jnp.zeros((PAD, D), x.dtype), meta_tokens.astype(x.dtype)], axis=0)

    cos2, sin2 = _rope_tables(nb)
    bias_tbl = _bias_tables(rel_bias)
    tri = jnp.asarray(np.tril(np.ones((2 * CHUNK, 2 * CHUNK), np.float32), k=-1), dtype=bf)

    r_all, qit, k1, k2, vt, iwt = _inproj(x, meta_pad, norm1_w[0][None].astype(f32),
                                          wr, wqi, wk, wv, ww, qscale, kscale)
    y_ret = _retention(r_all, nb, cos2, sin2, ret_norm_w[0][None].astype(f32))
    y_dsa = _sparse_attention(qit, iwt, k1, k2, vt, bias_tbl, tri, topk, nb)

    wo = w_out[0].astype(bf)
    out = _out_mlp(x.reshape(B * L, D), y_ret.reshape(B * L, rw), y_dsa.reshape(B * L, dw),
                   wo[:rw], wo[rw:], norm2_w[0][None].astype(f32),
                   w_ff1[0].astype(bf), w_ff2[0].astype(bf))
    return out.reshape(B, L, D)
```

```python
import functools
import math

import numpy as np
import jax
import jax.numpy as jnp
from jax import lax
from jax.experimental import pallas as pl
from jax.experimental.pallas import tpu as pltpu

N_META = 16
CHUNK = 128
RET_HEADS = 4
RET_DK = 128
DSA_HEADS = 8
DSA_DH = 64
IDX_HEADS = 8
TOPK_MAX = 256
N_BUCKETS = 32
MAX_DISTANCE = 128
ROPE_BASE = 10000.0
EPS = 1e-6
NEG = -1e30
PAD = CHUNK - N_META
LANES = 128
SUBLANES = 8
BF16_ROWS = 16
HALF = LANES // 2
INT_MIN = -(2 ** 31)
LOG2E = math.log2(math.e)

FFN_CHUNK = 1024
ROW_TILE = 512
VMEM_LIMIT = 56 * 1024 * 1024


def _dot(a, b):
    return jnp.dot(a, b, preferred_element_type=jnp.float32)


def _dot_nt(a, b):
    return lax.dot_general(a, b, (((1,), (1,)), ((), ())), preferred_element_type=jnp.float32)


def _dot_tn(a, b):
    return lax.dot_general(a, b, (((0,), (0,)), ((), ())), preferred_element_type=jnp.float32)


def _bucket_ranges():
    max_exact = N_BUCKETS // 2
    d = np.arange(0, 2 * CHUNK)
    large = max_exact + (np.log(np.maximum(d, 1) / max_exact) / math.log(MAX_DISTANCE / max_exact)
                         * (N_BUCKETS - max_exact)).astype(np.int64)
    bucket = np.where(d < max_exact, d, np.minimum(large, N_BUCKETS - 1))
    out = []
    for b in range(N_BUCKETS - 1):
        idx = np.nonzero(bucket == b)[0]
        out.append((int(idx.min()), int(idx.max())))
    return out


def _rope_kernel(cos_ref, sin_ref):
    n = pl.program_id(0)
    row = lax.broadcasted_iota(jnp.int32, (CHUNK, LANES), 0)
    lane = lax.broadcasted_iota(jnp.int32, (CHUNK, LANES), 1)
    pos = (n * CHUNK + row - PAD).astype(jnp.float32)
    frac = (lane % HALF).astype(jnp.float32) / HALF
    inv = jnp.exp(-frac * math.log(ROPE_BASE))
    ang = pos * inv
    cos_ref[...] = jnp.cos(ang)
    s = jnp.sin(ang)
    sin_ref[...] = jnp.where(lane < HALF, -s, s)


def _rope_tables(nb):
    return pl.pallas_call(
        _rope_kernel,
        grid=(nb,),
        out_specs=[pl.BlockSpec((CHUNK, LANES), lambda n: (n, 0))] * 2,
        out_shape=[jax.ShapeDtypeStruct((nb * CHUNK, LANES), jnp.float32)] * 2,
        name="rope_tables",
    )()


def _bias_kernel(rb_ref, out_ref):
    row = lax.broadcasted_iota(jnp.int32, (CHUNK, CHUNK), 0)
    col = lax.broadcasted_iota(jnp.int32, (CHUNK, CHUNK), 1)
    ranges = _bucket_ranges()
    out_ref[2] = jnp.zeros(out_ref.shape[1:], jnp.float32)
    for h in range(DSA_HEADS):
        far = rb_ref[N_BUCKETS - 1, h]
        for t in range(2):
            dist = col - row + t * CHUNK
            tile = jnp.zeros((CHUNK, CHUNK), jnp.float32)
            for b, (lo, hi) in enumerate(ranges):
                tile = jnp.where((dist >= lo) & (dist <= hi), (rb_ref[b, h] - far) * LOG2E, tile)
            out_ref[t, :, h * LANES:(h + 1) * LANES] = tile


def _bias_tables(rel_bias):
    return pl.pallas_call(
        _bias_kernel,
        in_specs=[pl.BlockSpec(memory_space=pltpu.SMEM)],
        out_specs=pl.BlockSpec(memory_space=pltpu.VMEM),
        out_shape=jax.ShapeDtypeStruct((3, CHUNK, DSA_HEADS * LANES), jnp.float32),
        name="bias_tables",
    )(rel_bias.astype(jnp.float32))


def _inproj_kernel(xa_ref, xb_ref, meta_ref, n1_ref, wr_ref, wqi_ref, wk_ref, wv_ref, ww_ref,
                   qs_ref, ks_ref,
                   r_ref, qit_ref, k1_ref, k2_ref, vt_ref, iwt_ref, *, pad_last):
    s = pl.program_id(1)
    top = jnp.where(s == 0, meta_ref[...], xa_ref[0])
    bot = xb_ref[0]
    if pad_last:
        bot = jnp.where(s == pl.num_programs(1) - 1, 0.0, bot)
    src = jnp.concatenate([top, bot], axis=0)
    ms = jnp.mean(src * src, axis=-1, keepdims=True)
    u = (src * lax.rsqrt(ms + EPS) * n1_ref[...]).astype(jnp.bfloat16)

    r_ref[0] = _dot(u, wr_ref[...])

    lane = lax.broadcasted_iota(jnp.int32, (2 * CHUNK, LANES), 1)
    lo = lane < HALF

    def head_rms(t):
        ssq = jnp.sum(jnp.where(lo, t * t, 0.0), axis=-1, keepdims=True)
        return lax.rsqrt(ssq / DSA_DH + EPS)

    pq = _dot(u, wqi_ref[...])
    for h in range(DSA_HEADS):
        t = pq[:, h * LANES:(h + 1) * LANES]
        t = t * (jnp.where(lo, head_rms(t), 1.0) * qs_ref[...])
        for j in range(2):
            qit_ref[0, j, :, h * LANES:(h + 1) * LANES] = (
                t[j * CHUNK:(j + 1) * CHUNK].T.astype(jnp.bfloat16))

    pk = _dot(u, wk_ref[...])
    kn = pk * head_rms(pk) * ks_ref[...]
    k1_ref[0] = jnp.where(lo, kn, 0.0).astype(jnp.bfloat16)
    k2_ref[0] = jnp.where(lo, 0.0, pk).astype(jnp.bfloat16)

    pv = _dot(u, wv_ref[...])
    vx = jnp.where(lo, pv, jnp.where(lane == HALF, 1.0, 0.0))
    iw = _dot(u, ww_ref[...]) * (IDX_HEADS ** -0.5)
    for j in range(2):
        rows = slice(j * CHUNK, (j + 1) * CHUNK)
        vt_ref[0, :, rows] = vx[rows].T.astype(jnp.bfloat16)
        iwt_ref[0, j] = iw[rows].T[0:IDX_HEADS, :]


def _inproj(x, meta_pad, n1, wr, wqi, wk, wv, ww, qscale, kscale):
    B, L, D = x.shape
    nx = L // CHUNK
    nbp = nx + 1 + ((nx + 1) % 2)
    tp = nbp * CHUNK
    const = lambda shape: pl.BlockSpec(shape, lambda b, s: (0,) * len(shape))
    blk = lambda w: pl.BlockSpec((1, 2 * CHUNK, w), lambda b, s: (b, s, 0))
    return pl.pallas_call(
        functools.partial(_inproj_kernel, pad_last=bool((nx + 1) % 2)),
        grid=(B, nbp // 2),
        in_specs=[
            pl.BlockSpec((1, CHUNK, D), lambda b, s: (b, jnp.maximum(2 * s - 1, 0), 0)),
            pl.BlockSpec((1, CHUNK, D), lambda b, s: (b, jnp.minimum(2 * s, nx - 1), 0)),
            const(meta_pad.shape), const(n1.shape), const(wr.shape), const(wqi.shape),
            const(wk.shape), const(wv.shape), const(ww.shape), const(qscale.shape),
            const(kscale.shape),
        ],
        out_specs=[
            blk(wr.shape[1]),
            pl.BlockSpec((1, 2, LANES, DSA_HEADS * LANES), lambda b, s: (b, s, 0, 0)),
            blk(LANES), blk(LANES),
            pl.BlockSpec((1, LANES, 2 * CHUNK), lambda b, s: (b, 0, s)),
            pl.BlockSpec((1, 2, IDX_HEADS, CHUNK), lambda b, s: (b, s, 0, 0)),
        ],
        out_shape=[
            jax.ShapeDtypeStruct((B, tp, wr.shape[1]), jnp.float32),
            jax.ShapeDtypeStruct((B, nbp, LANES, DSA_HEADS * LANES), jnp.bfloat16),
            jax.ShapeDtypeStruct((B, tp, LANES), jnp.bfloat16),
            jax.ShapeDtypeStruct((B, tp, LANES), jnp.bfloat16),
            jax.ShapeDtypeStruct((B, LANES, tp), jnp.bfloat16),
            jax.ShapeDtypeStruct((B, nbp, IDX_HEADS, CHUNK), jnp.float32),
        ],
        compiler_params=pltpu.CompilerParams(
            dimension_semantics=("arbitrary", "arbitrary"), vmem_limit_bytes=VMEM_LIMIT),
        name="in_projection",
    )(x, x, meta_pad, n1, wr, wqi, wk, wv, ww, qscale, kscale)


def _retention_kernel(ra_ref, rb_ref, cosa_ref, sina_ref, cosb_ref, sinb_ref, gain_ref, y_ref,
                      state_ref, tbl_ref):
    s = pl.program_id(1)
    w = RET_HEADS * RET_DK
    bf = jnp.bfloat16

    @pl.when(s == 0)
    def _():
        state_ref[...] = jnp.zeros_like(state_ref)
        row = lax.broadcasted_iota(jnp.int32, (CHUNK, CHUNK), 0)
        col = lax.broadcasted_iota(jnp.int32, (CHUNK, CHUNK), 1)
        rowf = row.astype(jnp.float32)
        diff = (row - col).astype(jnp.float32)
        for h in range(RET_HEADS):
            log_gamma = math.log(1.0 - 2.0 ** (-5.0 - h))
            tbl_ref[0, h] = jnp.where(diff >= 0, jnp.exp(log_gamma * jnp.maximum(diff, 0.0)), 0.0)
            tbl_ref[1, h] = jnp.exp(log_gamma * (rowf + 1.0))
            tbl_ref[2, h] = jnp.exp(log_gamma * (CHUNK - 1.0 - rowf))

    live = jnp.where(s == 0, 0.0, 1.0)
    chunks = ((ra_ref, cosa_ref, sina_ref, live), (rb_ref, cosb_ref, sinb_ref, None))
    for h in range(RET_HEADS):
        log_gamma = math.log(1.0 - 2.0 ** (-5.0 - h))
        sl = slice(h * RET_DK, (h + 1) * RET_DK)
        state = state_ref[h]
        for j, (r_ref, cos_ref, sin_ref, scale) in enumerate(chunks):
            cos2, sin2 = cos_ref[...], sin_ref[...]
            q = r_ref[0, :, sl]
            k = r_ref[0, :, w + h * RET_DK: w + (h + 1) * RET_DK]
            v = r_ref[0, :, 2 * w + h * RET_DK: 2 * w + (h + 1) * RET_DK]
            g = r_ref[0, :, 3 * w + h * RET_DK: 3 * w + (h + 1) * RET_DK]
            if scale is not None:
                k = k * scale
            v = v.astype(bf)
            qr = q * cos2 + pltpu.roll(q, HALF, 1) * sin2
            kr = (k * cos2 + pltpu.roll(k, HALF, 1) * sin2) * (RET_DK ** -0.5)

            scores = _dot_nt(qr.astype(bf), kr.astype(bf)) * tbl_ref[0, h]
            o = _dot(scores.astype(bf), v)
            o = o + _dot((qr * tbl_ref[1, h]).astype(bf), state.astype(bf))
            kv = _dot_tn((kr * tbl_ref[2, h]).astype(bf), v)
            state = state * math.exp(log_gamma * CHUNK) + kv

            ms = jnp.mean(o * o, axis=-1, keepdims=True)
            on = o * lax.rsqrt(ms + EPS) * gain_ref[:, sl]
            gate = g * (1.0 / (1.0 + jnp.exp(-g)))
            y_ref[0, j * CHUNK:(j + 1) * CHUNK, sl] = (gate * on).astype(y_ref.dtype)
        state_ref[h] = state


def _retention(r_all, nb, cos2, sin2, gain):
    B = r_all.shape[0]
    w = RET_HEADS * RET_DK
    nstep = (nb + 1) // 2
    rspec = lambda f: pl.BlockSpec((1, CHUNK, 4 * w), lambda b, s: (b, f(s), 0))
    tspec = lambda f: pl.BlockSpec((CHUNK, LANES), lambda b, s: (f(s), 0))
    first = lambda s: jnp.maximum(2 * s - 1, 0)
    second = lambda s: 2 * s
    return pl.pallas_call(
        _retention_kernel,
        grid=(B, nstep),
        in_specs=[rspec(first), rspec(second), tspec(first), tspec(first), tspec(second),
                  tspec(second), pl.BlockSpec((1, w), lambda b, s: (0, 0))],
        out_specs=pl.BlockSpec((1, 2 * CHUNK, w), lambda b, s: (b, jnp.maximum(s - 1, 0), 0)),
        out_shape=jax.ShapeDtypeStruct((B, (nb - 1) * CHUNK, w), jnp.bfloat16),
        scratch_shapes=[pltpu.VMEM((RET_HEADS, RET_DK, RET_DK), jnp.float32),
                        pltpu.VMEM((3, RET_HEADS, CHUNK, CHUNK), jnp.float32)],
        compiler_params=pltpu.CompilerParams(
            dimension_semantics=("arbitrary", "arbitrary"), vmem_limit_bytes=VMEM_LIMIT),
        name="retention",
    )(r_all, r_all, cos2, sin2, cos2, sin2, gain)


def _bit_planes(tile):
    a = [tile[r * 8:(r + 1) * 8, :] for r in range(32)]
    j, m = 16, 0x0000FFFF
    while j:
        for k in range(32):
            if (k & j) == 0:
                t = (a[k] ^ lax.shift_right_logical(a[k + j], j)) & m
                a[k] = a[k] ^ t
                a[k + j] = a[k + j] ^ (t << j)
        j >>= 1
        if j:
            m = m ^ ((m << j) & 0xFFFFFFFF)
    return a


def _sublane_total(x):
    x = x + pltpu.roll(x, 4, 0)
    x = x + pltpu.roll(x, 2, 0)
    return x + pltpu.roll(x, 1, 0)


def _dsa_kernel(qa_ref, qb_ref, iwt_ref, k1_ref, k2_ref, vt_ref, bias_ref, tri_ref, y_ref,
                key_ref, plane_ref, thr_ref, ntie_ref, flag_ref, m_ref, acc_ref, sa_ref, sb_ref,
                rhs_ref, *, topk):
    t = pl.program_id(1)
    last_q = pl.num_programs(1) - 1
    bf = jnp.bfloat16
    H = DSA_HEADS
    KP = 2 * CHUNK
    G = plane_ref.shape[1]
    cur, nxt = t % 2, (t + 1) % 2

    @pl.when((pl.program_id(0) == 0) & (t == 0))
    def _():
        plane_ref[...] = jnp.zeros_like(plane_ref)
        key_ref[...] = jnp.zeros_like(key_ref)
        eye = (lax.broadcasted_iota(jnp.int32, (CHUNK, CHUNK), 0)
               == lax.broadcasted_iota(jnp.int32, (CHUNK, CHUNK), 1))
        for h in range(DSA_HEADS):
            rhs_ref[CHUNK:2 * CHUNK, h * LANES:(h + 1) * LANES] = jnp.where(eye, 1.0, 0.0).astype(bf)

    @pl.when(t == 0)
    def _():
        thr_ref[0] = jnp.zeros(thr_ref.shape[1:], jnp.float32)
        ntie_ref[0] = jnp.zeros(ntie_ref.shape[1:], jnp.float32)
        flag_ref[0] = 0

    row = lax.broadcasted_iota(jnp.int32, (KP, CHUNK), 0)
    col = lax.broadcasted_iota(jnp.int32, (KP, CHUNK), 1)
    hs = lambda h: slice(h * LANES, (h + 1) * LANES)
    hp = lambda c: slice(2 * c * LANES, 2 * (c + 1) * LANES)
    nsteps = lambda n: (n + 2) // 2

    def key_rows(g):
        last = k1_ref.shape[1] // KP - 1
        return pl.ds(pl.multiple_of(jnp.minimum(g, last) * KP, KP), KP)

    blk_b = t + 1
    steps_b = jnp.where(t < last_q, nsteps(blk_b), 0)
    iters_b, tail_b = steps_b // 2, steps_b % 2
    qb_idx = blk_b * CHUNK + col
    iw_rows = [jnp.broadcast_to(iwt_ref[0, 0, h:h + 1, :], (KP, CHUNK)) for h in range(IDX_HEADS)]

    def score_pair(g):
        kblk = k2_ref[0, key_rows(g), :]
        sc = None
        for c in range(H // 2):
            s = _dot(kblk, qb_ref[0, 0, :, hp(c)])
            for hh in range(2):
                term = iw_rows[2 * c + hh] * jnp.maximum(s[:, hs(hh)], 0.0)
                sc = term if sc is None else sc + term
        k_idx = g * KP + row
        valid = (k_idx <= qb_idx) & (k_idx >= PAD)
        sc = jnp.where(valid, sc, NEG)
        key_ref[nxt, g] = sc
        bits = lax.bitcast_convert_type(sc, jnp.int32)
        key = jnp.where(bits < 0, jnp.int32(INT_MIN) - bits, bits)
        planes = _bit_planes(key ^ jnp.int32(INT_MIN))
        for p in range(32):
            plane_ref[p, g] = planes[p]

    def score_two(i):
        score_pair(2 * i)
        score_pair(2 * i + 1)

    blk_a = t
    steps_a = jnp.where(t > 0, nsteps(blk_a), 0)
    iters_a, tail_a = steps_a // 2, steps_a % 2
    far_a = jnp.where(t > 0, ((blk_a - 1) // 2) // 2, 0)
    rep = lambda x: jnp.tile(x, (KP // SUBLANES, 1))
    thr = thr_ref[cur]
    n_tie = ntie_ref[cur]
    thr_b = rep(thr)
    thr_sel_b = rep(jnp.where(thr <= NEG, 0.5 * NEG, thr))
    n_tie_b = rep(n_tie)
    has_ties = flag_ref[cur] > 0
    va = acc_ref.shape[0]

    @pl.when(t > 0)
    def _():
        m_ref[...] = jnp.full_like(m_ref, NEG)
        acc_ref[...] = jnp.zeros_like(acc_ref)
        rhs_ref[0:CHUNK, :] = qa_ref[0, 0]

    def logits(g, dst_ref, ties):
        kblk = k1_ref[0, key_rows(g), :]
        if ties:
            dst_ref[...] = _dot(kblk, qa_ref[0, 0])
        else:
            off = jnp.where(key_ref[cur, g] >= thr_sel_b, 0.0, NEG).astype(bf)
            dst_ref[...] = _dot(jnp.concatenate([kblk, off], axis=1), rhs_ref[...])

    def attend(g, s_ref, ties_before, near, ties):
        if ties:
            key = key_ref[cur, g]
            eq = key == thr_b
            eqf = jnp.where(eq, 1.0, 0.0)
            before = ties_before + _dot(tri_ref[...], eqf.astype(bf))
            sel = (key > thr_b) | (eq & (before < n_tie_b))
            ties_before = ties_before + jnp.sum(eqf, axis=0, keepdims=True)
        ps, alphas = [], []
        for h in range(H):
            s = s_ref[:, hs(h)]
            if near:
                ta = jnp.clip(blk_a - 2 * g, 0, 2)
                tb = jnp.clip(blk_a - 2 * g - 1, 0, 2)
                s = s + jnp.concatenate([bias_ref[ta, :, hs(h)], bias_ref[tb, :, hs(h)]], axis=0)
            if ties:
                s = jnp.where(sel, s, NEG)
            m_prev = m_ref[h:h + 1, :]
            m_new = jnp.maximum(m_prev, jnp.max(s, axis=0, keepdims=True))
            m_ref[h:h + 1, :] = m_new
            alphas.append(jnp.exp2(m_prev - m_new))
            ps.append(jnp.exp2(s - m_new).astype(bf))
        pv = _dot(vt_ref[0, 0:va, key_rows(g)], jnp.concatenate(ps, axis=1))
        acc_ref[...] = acc_ref[...] * jnp.concatenate(alphas, axis=1) + pv
        return ties_before

    def attend_two(i, ties_before, near, ties, fused):
        logits(2 * i + 1, sb_ref, ties)
        ties_before = attend(2 * i, sa_ref, ties_before, near, ties)
        logits(2 * i + 2, sa_ref, ties)
        ties_before = attend(2 * i + 1, sb_ref, ties_before, near, ties)
        if fused:
            score_two(i)
        return ties_before

    zero_ties = jnp.zeros((KP, CHUNK), jnp.float32)
    loop = lambda lo, hi, **kw: (lambda c: lax.fori_loop(lo, hi, functools.partial(attend_two, **kw), c))
    fused_hi = jnp.minimum(iters_a, iters_b)

    @pl.when(has_ties)
    def _():
        logits(0, sa_ref, True)
        c = loop(0, far_a, near=False, ties=True, fused=False)(zero_ties)
        c = loop(far_a, iters_a, near=True, ties=True, fused=False)(c)

        @pl.when(tail_a == 1)
        def _():
            attend(2 * iters_a, sa_ref, c, True, True)

    @pl.when(jnp.logical_not(has_ties))
    def _():
        logits(0, sa_ref, False)
        far_hi = jnp.minimum(far_a, fused_hi)
        loop(0, far_hi, near=False, ties=False, fused=True)(zero_ties)
        loop(far_hi, fused_hi, near=True, ties=False, fused=True)(zero_ties)
        loop(fused_hi, iters_a, near=True, ties=False, fused=False)(zero_ties)

        @pl.when(tail_a == 1)
        def _():
            attend(2 * iters_a, sa_ref, zero_ties, True, False)

    lax.fori_loop(jnp.where(has_ties, 0, fused_hi), iters_b, lambda i, _: (score_two(i), 0)[1], 0)

    @pl.when(tail_b == 1)
    def _():
        score_pair(2 * iters_b)

    @pl.when(t > 0)
    def _():
        lo = lax.broadcasted_iota(jnp.int32, (CHUNK, CHUNK), 1) < HALF
        for c in range(H // 2):
            tiles = []
            for h in (2 * c, 2 * c + 1):
                a = acc_ref[:, hs(h)] * (1.0 / acc_ref[HALF:HALF + 1, hs(h)])
                a = jnp.concatenate([a, jnp.zeros((CHUNK - va, CHUNK), jnp.float32)], axis=0)
                tiles.append(a.T)
            y_ref[0, :, hs(c)] = jnp.where(lo, tiles[0], pltpu.roll(tiles[1], HALF, 1)).astype(y_ref.dtype)

    @pl.when(t < last_q)
    def _():
        npair_b = (blk_b + 2) // 2
        g_idx = lax.broadcasted_iota(jnp.int32, (G, 8, CHUNK), 0)
        alive0 = jnp.where(g_idx < npair_b, jnp.int32(-1), jnp.int32(0))

        def ones_in(words):
            return _sublane_total(lax.population_count(words).sum(axis=0))

        def radix(i, carry):
            alive, above, thr_u = carry
            hi_plane, lo_plane = plane_ref[2 * i], plane_ref[2 * i + 1]
            a1 = alive & hi_plane
            a0 = alive ^ a1
            a11, a01 = a1 & lo_plane, a0 & lo_plane
            c1, c11, c01 = ones_in(a1), ones_in(a11), ones_in(a01)
            hi = (above + c1) >= topk
            above = jnp.where(hi, above, above + c1)
            c_lo = jnp.where(hi, c11, c01)
            lo = (above + c_lo) >= topk
            above = jnp.where(lo, above, above + c_lo)
            bit = lax.shift_right_logical(jnp.int32(INT_MIN), 2 * i)
            thr_u = (thr_u | jnp.where(hi, bit, 0)
                     | jnp.where(lo, lax.shift_right_logical(bit, 1), 0))
            group = jnp.where(hi[None], a1, a0)
            with_lo = jnp.where(hi[None], a11, a01)
            alive = jnp.where(lo[None], with_lo, group ^ with_lo)
            return alive, above, thr_u

        zero8 = jnp.zeros((8, CHUNK), jnp.int32)
        thr_u = lax.fori_loop(0, 16, radix, (alive0, zero8, zero8))[2]
        key_k = thr_u ^ jnp.int32(INT_MIN)
        as_score = lambda k: lax.bitcast_convert_type(
            jnp.where(k < 0, jnp.int32(INT_MIN) - k, k), jnp.float32)
        k_f = jnp.float32(topk)

        def count_ge(*ps):
            ps_b = [rep(p) for p in ps]

            def body(g, parts):
                sc = key_ref[nxt, g]
                return tuple(
                    part + jnp.where(sc >= p_b, 1.0, 0.0).reshape(KP // SUBLANES, SUBLANES, CHUNK).sum(axis=0)
                    for part, p_b in zip(parts, ps_b))

            zero = jnp.zeros((SUBLANES, CHUNK), jnp.float32)
            return [_sublane_total(c) for c in lax.fori_loop(0, steps_b, body, (zero,) * len(ps))]

        def pending(lo, hi, c_lo, c_hi):
            mid = 0.5 * (lo + hi)
            final = (c_lo >= k_f) & (c_hi < k_f) & ((c_lo == k_f) | (mid <= lo) | (mid >= hi))
            return jnp.logical_not(final)

        def any_lane(mask):
            return jnp.max(jnp.where(mask, 1.0, 0.0)) > 0.5

        def refine(state):
            lo, hi, c_lo, c_hi, step, it = state
            low_bad, high_bad = c_lo < k_f, c_hi >= k_f
            p = jnp.where(low_bad, lo - step, jnp.where(high_bad, hi + step, 0.5 * (lo + hi)))
            c_p, = count_ge(p)
            todo = pending(lo, hi, c_lo, c_hi)
            down = todo & low_bad
            up = todo & jnp.logical_not(low_bad) & high_bad
            split = todo & jnp.logical_not(low_bad | high_bad)
            as_lo = down | (split & (c_p >= k_f))
            as_hi = up | (split & (c_p < k_f))
            lo, c_lo, hi, c_hi = (
                jnp.where(as_lo, p, jnp.where(up, hi, lo)),
                jnp.where(as_lo, c_p, jnp.where(up, c_hi, c_lo)),
                jnp.where(as_hi, p, jnp.where(down, lo, hi)),
                jnp.where(as_hi, c_p, jnp.where(down, c_lo, c_hi)))
            return lo, hi, c_lo, c_hi, jnp.where(down | up, 2.0 * step, step), it + 1

        lo0, hi0 = as_score(key_k), as_score(key_k + 1)
        state = (lo0, hi0, *count_ge(lo0, hi0),
                 jnp.maximum(jnp.abs(lo0), 1e-30) * 2.0 ** -20, jnp.int32(0))
        lo, hi, c_lo, c_hi, _, _ = lax.while_loop(
            lambda st: any_lane(pending(*st[:4])) & (st[5] < 200), refine, state)
        masked_n = lo <= NEG
        exact_n = c_lo == k_f
        thr_ref[nxt] = lo
        ntie_ref[nxt] = jnp.where(masked_n, 0.0, jnp.where(exact_n, jnp.float32(2 * KP * G), k_f - c_hi))
        flag_ref[nxt] = any_lane((c_lo > k_f) & jnp.logical_not(masked_n)).astype(jnp.int32)


def _sparse_attention(qit, iwt, k1, k2, vt, bias_tbl, tri, topk, nb):
    B, nbp, _, hw = qit.shape
    tp = nbp * CHUNK
    nstep = nbp // 2
    nstep += nstep % 2
    kspec = pl.BlockSpec((1, tp, LANES), lambda b, t: (b, 0, 0))
    acc_rows = DSA_DH + BF16_ROWS
    nxt = lambda t: jnp.minimum(t + 1, nb - 1)
    return pl.pallas_call(
        functools.partial(_dsa_kernel, topk=topk),
        grid=(B, nb),
        in_specs=[
            pl.BlockSpec((1, 1, LANES, hw), lambda b, t: (b, t, 0, 0)),
            pl.BlockSpec((1, 1, LANES, hw), lambda b, t: (b, nxt(t), 0, 0)),
            pl.BlockSpec((1, 1, IDX_HEADS, CHUNK), lambda b, t: (b, nxt(t), 0, 0)),
            kspec, kspec,
            pl.BlockSpec((1, LANES, tp), lambda b, t: (b, 0, 0)),
            pl.BlockSpec(bias_tbl.shape, lambda b, t: (0, 0, 0)),
            pl.BlockSpec(tri.shape, lambda b, t: (0, 0)),
        ],
        out_specs=pl.BlockSpec((1, CHUNK, DSA_HEADS * DSA_DH),
                               lambda b, t: (b, jnp.maximum(t - 1, 0), 0)),
        out_shape=jax.ShapeDtypeStruct((B, (nb - 1) * CHUNK, DSA_HEADS * DSA_DH), jnp.bfloat16),
        scratch_shapes=[
            pltpu.VMEM((2, nstep, 2 * CHUNK, CHUNK), jnp.float32),
            pltpu.VMEM((32, nstep, 8, CHUNK), jnp.int32),
            pltpu.VMEM((2, 8, CHUNK), jnp.float32),
            pltpu.VMEM((2, 8, CHUNK), jnp.float32),
            pltpu.SMEM((2,), jnp.int32),
            pltpu.VMEM((DSA_HEADS, CHUNK), jnp.float32),
            pltpu.VMEM((acc_rows, hw), jnp.float32),
            pltpu.VMEM((2 * CHUNK, hw), jnp.float32),
            pltpu.VMEM((2 * CHUNK, hw), jnp.float32),
            pltpu.VMEM((2 * CHUNK, hw), jnp.bfloat16),
        ],
        compiler_params=pltpu.CompilerParams(
            dimension_semantics=("arbitrary", "arbitrary"), vmem_limit_bytes=VMEM_LIMIT),
        name="sparse_attention",
    )(qit, qit, iwt, k1, k2, vt, bias_tbl, tri)


def _mlp_kernel(x_ref, yr_ref, yd_ref, wor_ref, wod_ref, n2_ref, w1_ref, w2_ref, o_ref):
    h1 = x_ref[...] + _dot(yr_ref[...], wor_ref[...]) + _dot(yd_ref[...], wod_ref[...])
    ms = jnp.mean(h1 * h1, axis=-1, keepdims=True)
    u = (h1 * lax.rsqrt(ms + EPS) * n2_ref[...]).astype(jnp.bfloat16)
    o_ref[...] = h1
    d_ff = w1_ref.shape[1]
    for c in range(d_ff // FFN_CHUNK):
        sl = slice(c * FFN_CHUNK, (c + 1) * FFN_CHUNK)
        f = jnp.maximum(_dot(u, w1_ref[:, sl]), 0.0)
        o_ref[...] += _dot((f * f).astype(jnp.bfloat16), w2_ref[sl, :])


def _out_mlp(x2, yr, yd, wor, wod, n2, w1, w2):
    rows, D = x2.shape
    const = lambda a: pl.BlockSpec(a.shape, lambda i: (0, 0), pipeline_mode=pl.Buffered(1))
    tile = lambda w: pl.BlockSpec((ROW_TILE, w), lambda i: (i, 0))
    return pl.pallas_call(
        _mlp_kernel,
        grid=(rows // ROW_TILE,),
        in_specs=[tile(D), tile(yr.shape[1]), tile(yd.shape[1]),
                  const(wor), const(wod), const(n2), const(w1), const(w2)],
        out_specs=tile(D),
        out_shape=jax.ShapeDtypeStruct((rows, D), jnp.float32),
        compiler_params=pltpu.CompilerParams(
            dimension_semantics=("arbitrary",), vmem_limit_bytes=VMEM_LIMIT),
        name="out_mlp",
    )(x2, yr, yd, wor, wod, n2, w1, w2)


def kernel(x, meta_tokens, norm1_w, w_in, ret_norm_w, q_norm_w, k_norm_w, rel_bias,
           w_out, norm2_w, w_ff1, w_ff2):
    B, L, D = x.shape
    assert L % (2 * CHUNK) == 0 and L % ROW_TILE == 0 and w_in.shape[0] == 1
    topk = min(TOPK_MAX, L // 4)
    nb = L // CHUNK + 1
    bf = jnp.bfloat16
    f32 = jnp.float32

    rw = RET_HEADS * RET_DK
    dw = DSA_HEADS * DSA_DH
    sizes = (rw, rw, rw, rw, dw, DSA_DH, DSA_DH, IDX_HEADS * DSA_DH, DSA_DH, IDX_HEADS)
    offs = np.concatenate([[0], np.cumsum(sizes)])
    col = lambda i: w_in[0][:, int(offs[i]):int(offs[i + 1])]
    wr = jnp.concatenate([col(0), col(1), col(2), col(3)], axis=1).astype(bf)
    wq = col(4).reshape(D, DSA_HEADS, DSA_DH)
    wiq = col(7).reshape(D, IDX_HEADS, DSA_DH)
    wqi = jnp.concatenate([wq, wiq], axis=2).reshape(D, DSA_HEADS * LANES).astype(bf)
    wk = jnp.concatenate([col(5), col(8)], axis=1).astype(bf)
    wv = jnp.concatenate([col(6), jnp.zeros((D, LANES - DSA_DH), f32)], axis=1).astype(bf)
    ww = jnp.concatenate([col(9), jnp.zeros((D, LANES - IDX_HEADS), f32)], axis=1).astype(bf)

    idx_scale = jnp.full((DSA_DH,), DSA_DH ** -0.5, f32)
    qscale = jnp.concatenate([q_norm_w[0].astype(f32) * (DSA_DH ** -0.5 * LOG2E), idx_scale])[None]
    kscale = jnp.concatenate([k_norm_w[0].astype(f32), jnp.ones((DSA_DH,), f32)])[None]
    meta_pad = jnp.concatenate([jnp.zeros((PAD, D), x.dtype), meta_tokens.astype(x.dtype)], axis=0)

    cos2, sin2 = _rope_tables(nb)
    bias_tbl = _bias_tables(rel_bias)
    tri = jnp.asarray(np.tril(np.ones((2 * CHUNK, 2 * CHUNK), np.float32), k=-1), dtype=bf)

    r_all, qit, k1, k2, vt, iwt = _inproj(x, meta_pad, norm1_w[0][None].astype(f32),
                                          wr, wqi, wk, wv, ww, qscale, kscale)
    y_ret = _retention(r_all, nb, cos2, sin2, ret_norm_w[0][None].astype(f32))
    y_dsa = _sparse_attention(qit, iwt, k1, k2, vt, bias_tbl, tri, topk, nb)

    wo = w_out[0].astype(bf)
    out = _out_mlp(x.reshape(B * L, D), y_ret.reshape(B * L, rw), y_dsa.reshape(B * L, dw),
                   wo[:rw], wo[rw:], norm2_w[0][None].astype(f32),
                   w_ff1[0].astype(bf), w_ff2[0].astype(bf))
    return out.reshape(B, L, D)
```

```python
import functools
import math

import numpy as np
import jax
import jax.numpy as jnp
from jax import lax
from jax.experimental import pallas as pl
from jax.experimental.pallas import tpu as pltpu

N_META = 16
CHUNK = 128
RET_HEADS = 4
RET_DK = 128
DSA_HEADS = 8
DSA_DH = 64
IDX_HEADS = 8
TOPK_MAX = 256
N_BUCKETS = 32
MAX_DISTANCE = 128
ROPE_BASE = 10000.0
EPS = 1e-6
NEG = -1e30
PAD = CHUNK - N_META
LANES = 128
SUBLANES = 8
BF16_ROWS = 16
HALF = LANES // 2
INT_MIN = -(2 ** 31)
LOG2E = math.log2(math.e)

FFN_CHUNK = 1024
ROW_TILE = 512
VMEM_LIMIT = 56 * 1024 * 1024


def _dot(a, b):
    return jnp.dot(a, b, preferred_element_type=jnp.float32)


def _dot_nt(a, b):
    return lax.dot_general(a, b, (((1,), (1,)), ((), ())), preferred_element_type=jnp.float32)


def _dot_tn(a, b):
    return lax.dot_general(a, b, (((0,), (0,)), ((), ())), preferred_element_type=jnp.float32)


def _bucket_ranges():
    max_exact = N_BUCKETS // 2
    d = np.arange(0, 2 * CHUNK)
    large = max_exact + (np.log(np.maximum(d, 1) / max_exact) / math.log(MAX_DISTANCE / max_exact)
                         * (N_BUCKETS - max_exact)).astype(np.int64)
    bucket = np.where(d < max_exact, d, np.minimum(large, N_BUCKETS - 1))
    out = []
    for b in range(N_BUCKETS - 1):
        idx = np.nonzero(bucket == b)[0]
        out.append((int(idx.min()), int(idx.max())))
    return out


def _rope_kernel(cos_ref, sin_ref):
    n = pl.program_id(0)
    row = lax.broadcasted_iota(jnp.int32, (CHUNK, LANES), 0)
    lane = lax.broadcasted_iota(jnp.int32, (CHUNK, LANES), 1)
    pos = (n * CHUNK + row - PAD).astype(jnp.float32)
    frac = (lane % HALF).astype(jnp.float32) / HALF
    inv = jnp.exp(-frac * math.log(ROPE_BASE))
    ang = pos * inv
    cos_ref[...] = jnp.cos(ang)
    s = jnp.sin(ang)
    sin_ref[...] = jnp.where(lane < HALF, -s, s)


def _rope_tables(nb):
    return pl.pallas_call(
        _rope_kernel,
        grid=(nb,),
        out_specs=[pl.BlockSpec((CHUNK, LANES), lambda n: (n, 0))] * 2,
        out_shape=[jax.ShapeDtypeStruct((nb * CHUNK, LANES), jnp.float32)] * 2,
        name="rope_tables",
    )()


def _bias_kernel(rb_ref, out_ref):
    row = lax.broadcasted_iota(jnp.int32, (CHUNK, CHUNK), 0)
    col = lax.broadcasted_iota(jnp.int32, (CHUNK, CHUNK), 1)
    ranges = _bucket_ranges()
    out_ref[2] = jnp.zeros(out_ref.shape[1:], jnp.float32)
    for h in range(DSA_HEADS):
        far = rb_ref[N_BUCKETS - 1, h]
        for t in range(2):
            dist = col - row + t * CHUNK
            tile = jnp.zeros((CHUNK, CHUNK), jnp.float32)
            for b, (lo, hi) in enumerate(ranges):
                tile = jnp.where((dist >= lo) & (dist <= hi), (rb_ref[b, h] - far) * LOG2E, tile)
            out_ref[t, :, h * LANES:(h + 1) * LANES] = tile


def _bias_tables(rel_bias):
    return pl.pallas_call(
        _bias_kernel,
        in_specs=[pl.BlockSpec(memory_space=pltpu.SMEM)],
        out_specs=pl.BlockSpec(memory_space=pltpu.VMEM),
        out_shape=jax.ShapeDtypeStruct((3, CHUNK, DSA_HEADS * LANES), jnp.float32),
        name="bias_tables",
    )(rel_bias.astype(jnp.float32))


def _inproj_kernel(xa_ref, xb_ref, meta_ref, n1_ref, wr_ref, wqi_ref, wk_ref, wv_ref, ww_ref,
                   qs_ref, ks_ref,
                   r_ref, qit_ref, k1_ref, k2_ref, vt_ref, iwt_ref, *, pad_last):
    s = pl.program_id(1)
    top = jnp.where(s == 0, meta_ref[...], xa_ref[0])
    bot = xb_ref[0]
    if pad_last:
        bot = jnp.where(s == pl.num_programs(1) - 1, 0.0, bot)
    src = jnp.concatenate([top, bot], axis=0)
    ms = jnp.mean(src * src, axis=-1, keepdims=True)
    u = (src * lax.rsqrt(ms + EPS) * n1_ref[...]).astype(jnp.bfloat16)

    r_ref[0] = _dot(u, wr_ref[...])

    lane = lax.broadcasted_iota(jnp.int32, (2 * CHUNK, LANES), 1)
    lo = lane < HALF

    def head_rms(t):
        ssq = jnp.sum(jnp.where(lo, t * t, 0.0), axis=-1, keepdims=True)
        return lax.rsqrt(ssq / DSA_DH + EPS)

    pq = _dot(u, wqi_ref[...])
    for h in range(DSA_HEADS):
        t = pq[:, h * LANES:(h + 1) * LANES]
        t = t * (jnp.where(lo, head_rms(t), 1.0) * qs_ref[...])
        for j in range(2):
            qit_ref[0, j, :, h * LANES:(h + 1) * LANES] = (
                t[j * CHUNK:(j + 1) * CHUNK].T.astype(jnp.bfloat16))

    pk = _dot(u, wk_ref[...])
    kn = pk * head_rms(pk) * ks_ref[...]
    k1_ref[0] = jnp.where(lo, kn, 0.0).astype(jnp.bfloat16)
    k2_ref[0] = jnp.where(lo, 0.0, pk).astype(jnp.bfloat16)

    pv = _dot(u, wv_ref[...])
    vx = jnp.where(lo, pv, jnp.where(lane == HALF, 1.0, 0.0))
    iw = _dot(u, ww_ref[...]) * (IDX_HEADS ** -0.5)
    for j in range(2):
        rows = slice(j * CHUNK, (j + 1) * CHUNK)
        vt_ref[0, :, rows] = vx[rows].T.astype(jnp.bfloat16)
        iwt_ref[0, j] = iw[rows].T[0:IDX_HEADS, :]


def _inproj(x, meta_pad, n1, wr, wqi, wk, wv, ww, qscale, kscale):
    B, L, D = x.shape
    nx = L // CHUNK
    nbp = nx + 1 + ((nx + 1) % 2)
    tp = nbp * CHUNK
    const = lambda shape: pl.BlockSpec(shape, lambda b, s: (0,) * len(shape))
    blk = lambda w: pl.BlockSpec((1, 2 * CHUNK, w), lambda b, s: (b, s, 0))
    return pl.pallas_call(
        functools.partial(_inproj_kernel, pad_last=bool((nx + 1) % 2)),
        grid=(B, nbp // 2),
        in_specs=[
            pl.BlockSpec((1, CHUNK, D), lambda b, s: (b, jnp.maximum(2 * s - 1, 0), 0)),
            pl.BlockSpec((1, CHUNK, D), lambda b, s: (b, jnp.minimum(2 * s, nx - 1), 0)),
            const(meta_pad.shape), const(n1.shape), const(wr.shape), const(wqi.shape),
            const(wk.shape), const(wv.shape), const(ww.shape), const(qscale.shape),
            const(kscale.shape),
        ],
        out_specs=[
            blk(wr.shape[1]),
            pl.BlockSpec((1, 2, LANES, DSA_HEADS * LANES), lambda b, s: (b, s, 0, 0)),
            blk(LANES), blk(LANES),
            pl.BlockSpec((1, LANES, 2 * CHUNK), lambda b, s: (b, 0, s)),
            pl.BlockSpec((1, 2, IDX_HEADS, CHUNK), lambda b, s: (b, s, 0, 0)),
        ],
        out_shape=[
            jax.ShapeDtypeStruct((B, tp, wr.shape[1]), jnp.float32),
            jax.ShapeDtypeStruct((B, nbp, LANES, DSA_HEADS * LANES), jnp.bfloat16),
            jax.ShapeDtypeStruct((B, tp, LANES), jnp.bfloat16),
            jax.ShapeDtypeStruct((B, tp, LANES), jnp.bfloat16),
            jax.ShapeDtypeStruct((B, LANES, tp), jnp.bfloat16),
            jax.ShapeDtypeStruct((B, nbp, IDX_HEADS, CHUNK), jnp.float32),
        ],
        compiler_params=pltpu.CompilerParams(
            dimension_semantics=("arbitrary", "arbitrary"), vmem_limit_bytes=VMEM_LIMIT),
        name="in_projection",
    )(x, x, meta_pad, n1, wr, wqi, wk, wv, ww, qscale, kscale)


def _retention_kernel(ra_ref, rb_ref, cosa_ref, sina_ref, cosb_ref, sinb_ref, gain_ref, y_ref,
                      state_ref, tbl_ref):
    s = pl.program_id(1)
    w = RET_HEADS * RET_DK
    bf = jnp.bfloat16

    @pl.when(s == 0)
    def _():
        state_ref[...] = jnp.zeros_like(state_ref)
        row = lax.broadcasted_iota(jnp.int32, (CHUNK, CHUNK), 0)
        col = lax.broadcasted_iota(jnp.int32, (CHUNK, CHUNK), 1)
        rowf = row.astype(jnp.float32)
        diff = (row - col).astype(jnp.float32)
        for h in range(RET_HEADS):
            log_gamma = math.log(1.0 - 2.0 ** (-5.0 - h))
            tbl_ref[0, h] = jnp.where(diff >= 0, jnp.exp(log_gamma * jnp.maximum(diff, 0.0)), 0.0)
            tbl_ref[1, h] = jnp.exp(log_gamma * (rowf + 1.0))
            tbl_ref[2, h] = jnp.exp(log_gamma * (CHUNK - 1.0 - rowf))

    live = jnp.where(s == 0, 0.0, 1.0)
    chunks = ((ra_ref, cosa_ref, sina_ref, live), (rb_ref, cosb_ref, sinb_ref, None))
    for h in range(RET_HEADS):
        log_gamma = math.log(1.0 - 2.0 ** (-5.0 - h))
        sl = slice(h * RET_DK, (h + 1) * RET_DK)
        state = state_ref[h]
        for j, (r_ref, cos_ref, sin_ref, scale) in enumerate(chunks):
            cos2, sin2 = cos_ref[...], sin_ref[...]
            q = r_ref[0, :, sl]
            k = r_ref[0, :, w + h * RET_DK: w + (h + 1) * RET_DK]
            v = r_ref[0, :, 2 * w + h * RET_DK: 2 * w + (h + 1) * RET_DK]
            g = r_ref[0, :, 3 * w + h * RET_DK: 3 * w + (h + 1) * RET_DK]
            if scale is not None:
                k = k * scale
            v = v.astype(bf)
            qr = q * cos2 + pltpu.roll(q, HALF, 1) * sin2
            kr = (k * cos2 + pltpu.roll(k, HALF, 1) * sin2) * (RET_DK ** -0.5)

            scores = _dot_nt(qr.astype(bf), kr.astype(bf)) * tbl_ref[0, h]
            o = _dot(scores.astype(bf), v)
            o = o + _dot((qr * tbl_ref[1, h]).astype(bf), state.astype(bf))
            kv = _dot_tn((kr * tbl_ref[2, h]).astype(bf), v)
            state = state * math.exp(log_gamma * CHUNK) + kv

            ms = jnp.mean(o * o, axis=-1, keepdims=True)
            on = o * lax.rsqrt(ms + EPS) * gain_ref[:, sl]
            gate = g * (1.0 / (1.0 + jnp.exp(-g)))
            y_ref[0, j * CHUNK:(j + 1) * CHUNK, sl] = (gate * on).astype(y_ref.dtype)
        state_ref[h] = state


def _retention(r_all, nb, cos2, sin2, gain):
    B = r_all.shape[0]
    w = RET_HEADS * RET_DK
    nstep = (nb + 1) // 2
    rspec = lambda f: pl.BlockSpec((1, CHUNK, 4 * w), lambda b, s: (b, f(s), 0))
    tspec = lambda f: pl.BlockSpec((CHUNK, LANES), lambda b, s: (f(s), 0))
    first = lambda s: jnp.maximum(2 * s - 1, 0)
    second = lambda s: 2 * s
    return pl.pallas_call(
        _retention_kernel,
        grid=(B, nstep),
        in_specs=[rspec(first), rspec(second), tspec(first), tspec(first), tspec(second),
                  tspec(second), pl.BlockSpec((1, w), lambda b, s: (0, 0))],
        out_specs=pl.BlockSpec((1, 2 * CHUNK, w), lambda b, s: (b, jnp.maximum(s - 1, 0), 0)),
        out_shape=jax.ShapeDtypeStruct((B, (nb - 1) * CHUNK, w), jnp.bfloat16),
        scratch_shapes=[pltpu.VMEM((RET_HEADS, RET_DK, RET_DK), jnp.float32),
                        pltpu.VMEM((3, RET_HEADS, CHUNK, CHUNK), jnp.float32)],
        compiler_params=pltpu.CompilerParams(
            dimension_semantics=("arbitrary", "arbitrary"), vmem_limit_bytes=VMEM_LIMIT),
        name="retention",
    )(r_all, r_all, cos2, sin2, cos2, sin2, gain)


def _bit_planes(tile):
    a = [tile[r * 8:(r + 1) * 8, :] for r in range(32)]
    j, m = 16, 0x0000FFFF
    while j:
        for k in range(32):
            if (k & j) == 0:
                t = (a[k] ^ lax.shift_right_logical(a[k + j], j)) & m
                a[k] = a[k] ^ t
                a[k + j] = a[k + j] ^ (t << j)
        j >>= 1
        if j:
            m = m ^ ((m << j) & 0xFFFFFFFF)
    return a


def _sublane_total(x):
    x = x + pltpu.roll(x, 4, 0)
    x = x + pltpu.roll(x, 2, 0)
    return x + pltpu.roll(x, 1, 0)


def _dsa_kernel(qa_ref, qb_ref, iwt_ref, k1_ref, k2_ref, vt_ref, bias_ref, tri_ref, y_ref,
                key_ref, plane_ref, thr_ref, ntie_ref, flag_ref, m_ref, acc_ref, sa_ref, sb_ref,
                rhs_ref, *, topk):
    t = pl.program_id(1)
    last_q = pl.num_programs(1) - 1
    bf = jnp.bfloat16
    H = DSA_HEADS
    KP = 2 * CHUNK
    G = plane_ref.shape[1]
    cur, nxt = t % 2, (t + 1) % 2

    @pl.when((pl.program_id(0) == 0) & (t == 0))
    def _():
        plane_ref[...] = jnp.zeros_like(plane_ref)
        key_ref[...] = jnp.zeros_like(key_ref)
        eye = (lax.broadcasted_iota(jnp.int32, (CHUNK, CHUNK), 0)
               == lax.broadcasted_iota(jnp.int32, (CHUNK, CHUNK), 1))
        for h in range(DSA_HEADS):
            rhs_ref[CHUNK:2 * CHUNK, h * LANES:(h + 1) * LANES] = jnp.where(eye, 1.0, 0.0).astype(bf)

    @pl.when(t == 0)
    def _():
        thr_ref[0] = jnp.zeros(thr_ref.shape[1:], jnp.float32)
        ntie_ref[0] = jnp.zeros(ntie_ref.shape[1:], jnp.float32)
        flag_ref[0] = 0

    row = lax.broadcasted_iota(jnp.int32, (KP, CHUNK), 0)
    col = lax.broadcasted_iota(jnp.int32, (KP, CHUNK), 1)
    hs = lambda h: slice(h * LANES, (h + 1) * LANES)
    hp = lambda c: slice(2 * c * LANES, 2 * (c + 1) * LANES)
    nsteps = lambda n: (n + 2) // 2

    def key_rows(g):
        last = k1_ref.shape[1] // KP - 1
        return pl.ds(pl.multiple_of(jnp.minimum(g, last) * KP, KP), KP)

    blk_b = t + 1
    steps_b = jnp.where(t < last_q, nsteps(blk_b), 0)
    iters_b, tail_b = steps_b // 2, steps_b % 2
    qb_idx = blk_b * CHUNK + col
    iw_rows = [jnp.broadcast_to(iwt_ref[0, 0, h:h + 1, :], (KP, CHUNK)) for h in range(IDX_HEADS)]

    def score_pair(g):
        kblk = k2_ref[0, key_rows(g), :]
        sc = None
        for c in range(H // 2):
            s = _dot(kblk, qb_ref[0, 0, :, hp(c)])
            for hh in range(2):
                term = iw_rows[2 * c + hh] * jnp.maximum(s[:, hs(hh)], 0.0)
                sc = term if sc is None else sc + term
        k_idx = g * KP + row
        valid = (k_idx <= qb_idx) & (k_idx >= PAD)
        sc = jnp.where(valid, sc, NEG)
        key_ref[nxt, g] = sc
        bits = lax.bitcast_convert_type(sc, jnp.int32)
        key = jnp.where(bits < 0, jnp.int32(INT_MIN) - bits, bits)
        planes = _bit_planes(key ^ jnp.int32(INT_MIN))
        for p in range(32):
            plane_ref[p, g] = planes[p]

    def score_two(i):
        score_pair(2 * i)
        score_pair(2 * i + 1)

    blk_a = t
    steps_a = jnp.where(t > 0, nsteps(blk_a), 0)
    iters_a, tail_a = steps_a // 2, steps_a % 2
    far_a = jnp.where(t > 0, ((blk_a - 1) // 2) // 2, 0)
    rep = lambda x: jnp.tile(x, (KP // SUBLANES, 1))
    thr = thr_ref[cur]
    few = thr <= 0.999 * NEG
    thr_b = rep(jnp.where(few, 0.5 * NEG, thr))
    thr_sel_b = thr_b
    n_tie_b = rep(jnp.where(few, 0.0, ntie_ref[cur]))
    has_ties = flag_ref[cur] > 0
    va = acc_ref.shape[0]

    @pl.when(t > 0)
    def _():
        m_ref[...] = jnp.full_like(m_ref, NEG)
        acc_ref[...] = jnp.zeros_like(acc_ref)
        rhs_ref[0:CHUNK, :] = qa_ref[0, 0]

    def logits(g, dst_ref, ties):
        kblk = k1_ref[0, key_rows(g), :]
        if ties:
            dst_ref[...] = _dot(kblk, qa_ref[0, 0])
        else:
            off = jnp.where(key_ref[cur, g] >= thr_sel_b, 0.0, NEG).astype(bf)
            dst_ref[...] = _dot(jnp.concatenate([kblk, off], axis=1), rhs_ref[...])

    def attend(g, s_ref, ties_before, near, ties):
        if ties:
            key = key_ref[cur, g]
            eq = key == thr_b
            eqf = jnp.where(eq, 1.0, 0.0)
            before = ties_before + _dot(tri_ref[...], eqf.astype(bf))
            sel = (key > thr_b) | (eq & (before < n_tie_b))
            ties_before = ties_before + jnp.sum(eqf, axis=0, keepdims=True)
        ps, alphas = [], []
        for h in range(H):
            s = s_ref[:, hs(h)]
            if near:
                ta = jnp.clip(blk_a - 2 * g, 0, 2)
                tb = jnp.clip(blk_a - 2 * g - 1, 0, 2)
                s = s + jnp.concatenate([bias_ref[ta, :, hs(h)], bias_ref[tb, :, hs(h)]], axis=0)
            if ties:
                s = jnp.where(sel, s, NEG)
            m_prev = m_ref[h:h + 1, :]
            m_new = jnp.maximum(m_prev, jnp.max(s, axis=0, keepdims=True))
            m_ref[h:h + 1, :] = m_new
            alphas.append(jnp.exp2(m_prev - m_new))
            ps.append(jnp.exp2(s - m_new).astype(bf))
        pv = _dot(vt_ref[0, 0:va, key_rows(g)], jnp.concatenate(ps, axis=1))
        acc_ref[...] = acc_ref[...] * jnp.concatenate(alphas, axis=1) + pv
        return ties_before

    def attend_two(i, ties_before, near, ties, fused):
        logits(2 * i + 1, sb_ref, ties)
        ties_before = attend(2 * i, sa_ref, ties_before, near, ties)
        logits(2 * i + 2, sa_ref, ties)
        ties_before = attend(2 * i + 1, sb_ref, ties_before, near, ties)
        if fused:
            score_two(i)
        return ties_before

    zero_ties = jnp.zeros((KP, CHUNK), jnp.float32)
    loop = lambda lo, hi, **kw: (lambda c: lax.fori_loop(lo, hi, functools.partial(attend_two, **kw), c))
    fused_hi = jnp.minimum(iters_a, iters_b)

    @pl.when(has_ties)
    def _():
        logits(0, sa_ref, True)
        c = loop(0, far_a, near=False, ties=True, fused=False)(zero_ties)
        c = loop(far_a, iters_a, near=True, ties=True, fused=False)(c)

        @pl.when(tail_a == 1)
        def _():
            attend(2 * iters_a, sa_ref, c, True, True)

    @pl.when(jnp.logical_not(has_ties))
    def _():
        logits(0, sa_ref, False)
        far_hi = jnp.minimum(far_a, fused_hi)
        loop(0, far_hi, near=False, ties=False, fused=True)(zero_ties)
        loop(far_hi, fused_hi, near=True, ties=False, fused=True)(zero_ties)
        loop(fused_hi, iters_a, near=True, ties=False, fused=False)(zero_ties)

        @pl.when(tail_a == 1)
        def _():
            attend(2 * iters_a, sa_ref, zero_ties, True, False)

    lax.fori_loop(jnp.where(has_ties, 0, fused_hi), iters_b, lambda i, _: (score_two(i), 0)[1], 0)

    @pl.when(tail_b == 1)
    def _():
        score_pair(2 * iters_b)

    @pl.when(t > 0)
    def _():
        lo = lax.broadcasted_iota(jnp.int32, (CHUNK, CHUNK), 1) < HALF
        for c in range(H // 2):
            tiles = []
            for h in (2 * c, 2 * c + 1):
                a = acc_ref[:, hs(h)] * (1.0 / acc_ref[HALF:HALF + 1, hs(h)])
                a = jnp.concatenate([a, jnp.zeros((CHUNK - va, CHUNK), jnp.float32)], axis=0)
                tiles.append(a.T)
            y_ref[0, :, hs(c)] = jnp.where(lo, tiles[0], pltpu.roll(tiles[1], HALF, 1)).astype(y_ref.dtype)

    @pl.when(t < last_q)
    def _():
        npair_b = (blk_b + 2) // 2
        g_idx = lax.broadcasted_iota(jnp.int32, (G, 8, CHUNK), 0)
        alive0 = jnp.where(g_idx < npair_b, jnp.int32(-1), jnp.int32(0))

        def ones_in(words):
            return _sublane_total(lax.population_count(words).sum(axis=0))

        def radix(i, carry):
            alive, above, thr_u = carry
            hi_plane, lo_plane = plane_ref[2 * i], plane_ref[2 * i + 1]
            a1 = alive & hi_plane
            a0 = alive ^ a1
            a11, a01 = a1 & lo_plane, a0 & lo_plane
            c1, c11, c01 = ones_in(a1), ones_in(a11), ones_in(a01)
            hi = (above + c1) >= topk
            above = jnp.where(hi, above, above + c1)
            c_lo = jnp.where(hi, c11, c01)
            lo = (above + c_lo) >= topk
            above = jnp.where(lo, above, above + c_lo)
            bit = lax.shift_right_logical(jnp.int32(INT_MIN), 2 * i)
            thr_u = (thr_u | jnp.where(hi, bit, 0)
                     | jnp.where(lo, lax.shift_right_logical(bit, 1), 0))
            group = jnp.where(hi[None], a1, a0)
            with_lo = jnp.where(hi[None], a11, a01)
            alive = jnp.where(lo[None], with_lo, group ^ with_lo)
            return alive, above, thr_u

        zero8 = jnp.zeros((8, CHUNK), jnp.int32)
        thr_u = lax.fori_loop(0, 16, radix, (alive0, zero8, zero8))[2]
        key_k = thr_u ^ jnp.int32(INT_MIN)
        as_score = lambda k: lax.bitcast_convert_type(
            jnp.where(k < 0, jnp.int32(INT_MIN) - k, k), jnp.float32)
        k_f = jnp.float32(topk)

        def count_ge(*ps):
            ps_b = [rep(p) for p in ps]

            def body(g, parts):
                sc = key_ref[nxt, g]
                return tuple(
                    part + jnp.where(sc >= p_b, 1.0, 0.0).reshape(KP // SUBLANES, SUBLANES, CHUNK).sum(axis=0)
                    for part, p_b in zip(parts, ps_b))

            zero = jnp.zeros((SUBLANES, CHUNK), jnp.float32)
            return [_sublane_total(c) for c in lax.fori_loop(0, steps_b, body, (zero,) * len(ps))]

        def pending(lo, hi, c_lo, c_hi):
            mid = 0.5 * (lo + hi)
            final = (c_lo >= k_f) & (c_hi < k_f) & ((c_lo == k_f) | (mid <= lo) | (mid >= hi))
            return jnp.logical_not(final)

        def any_lane(mask):
            return jnp.max(jnp.where(mask, 1.0, 0.0)) > 0.5

        def refine(state):
            lo, hi, c_lo, c_hi, step, it = state
            low_bad, high_bad = c_lo < k_f, c_hi >= k_f
            p = jnp.where(low_bad, lo - step, jnp.where(high_bad, hi + step, 0.5 * (lo + hi)))
            c_p, = count_ge(p)
            todo = pending(lo, hi, c_lo, c_hi)
            down = todo & low_bad
            up = todo & jnp.logical_not(low_bad) & high_bad
            split = todo & jnp.logical_not(low_bad | high_bad)
            as_lo = down | (split & (c_p >= k_f))
            as_hi = up | (split & (c_p < k_f))
            lo, c_lo, hi, c_hi = (
                jnp.where(as_lo, p, jnp.where(up, hi, lo)),
                jnp.where(as_lo, c_p, jnp.where(up, c_hi, c_lo)),
                jnp.where(as_hi, p, jnp.where(down, lo, hi)),
                jnp.where(as_hi, c_p, jnp.where(down, c_lo, c_hi)))
            return lo, hi, c_lo, c_hi, jnp.where(down | up, 2.0 * step, step), it + 1

        lo0, hi0 = as_score(key_k), as_score(key_k + 1)
        state = (lo0, hi0, *count_ge(lo0, hi0),
                 jnp.maximum(jnp.abs(lo0), 1e-30) * 2.0 ** -20, jnp.int32(0))
        lo, hi, c_lo, c_hi, _, _ = lax.while_loop(
            lambda st: any_lane(pending(*st[:4])) & (st[5] < 200), refine, state)
        masked_n = lo <= 0.999 * NEG
        exact_n = c_lo == k_f
        thr_ref[nxt] = lo
        ntie_ref[nxt] = jnp.where(masked_n, 0.0, jnp.where(exact_n, jnp.float32(2 * KP * G), k_f - c_hi))
        flag_ref[nxt] = any_lane((c_lo > k_f) & jnp.logical_not(masked_n)).astype(jnp.int32)


def _sparse_attention(qit, iwt, k1, k2, vt, bias_tbl, tri, topk, nb):
    B, nbp, _, hw = qit.shape
    tp = nbp * CHUNK
    nstep = nbp // 2
    nstep += nstep % 2
    kspec = pl.BlockSpec((1, tp, LANES), lambda b, t: (b, 0, 0))
    acc_rows = DSA_DH + BF16_ROWS
    nxt = lambda t: jnp.minimum(t + 1, nb - 1)
    return pl.pallas_call(
        functools.partial(_dsa_kernel, topk=topk),
        grid=(B, nb),
        in_specs=[
            pl.BlockSpec((1, 1, LANES, hw), lambda b, t: (b, t, 0, 0)),
            pl.BlockSpec((1, 1, LANES, hw), lambda b, t: (b, nxt(t), 0, 0)),
            pl.BlockSpec((1, 1, IDX_HEADS, CHUNK), lambda b, t: (b, nxt(t), 0, 0)),
            kspec, kspec,
            pl.BlockSpec((1, LANES, tp), lambda b, t: (b, 0, 0)),
            pl.BlockSpec(bias_tbl.shape, lambda b, t: (0, 0, 0)),
            pl.BlockSpec(tri.shape, lambda b, t: (0, 0)),
        ],
        out_specs=pl.BlockSpec((1, CHUNK, DSA_HEADS * DSA_DH),
                               lambda b, t: (b, jnp.maximum(t - 1, 0), 0)),
        out_shape=jax.ShapeDtypeStruct((B, (nb - 1) * CHUNK, DSA_HEADS * DSA_DH), jnp.bfloat16),
        scratch_shapes=[
            pltpu.VMEM((2, nstep, 2 * CHUNK, CHUNK), jnp.float32),
            pltpu.VMEM((32, nstep, 8, CHUNK), jnp.int32),
            pltpu.VMEM((2, 8, CHUNK), jnp.float32),
            pltpu.VMEM((2, 8, CHUNK), jnp.float32),
            pltpu.SMEM((2,), jnp.int32),
            pltpu.VMEM((DSA_HEADS, CHUNK), jnp.float32),
            pltpu.VMEM((acc_rows, hw), jnp.float32),
            pltpu.VMEM((2 * CHUNK, hw), jnp.float32),
            pltpu.VMEM((2 * CHUNK, hw), jnp.float32),
            pltpu.VMEM((2 * CHUNK, hw), jnp.bfloat16),
        ],
        compiler_params=pltpu.CompilerParams(
            dimension_semantics=("arbitrary", "arbitrary"), vmem_limit_bytes=VMEM_LIMIT),
        name="sparse_attention",
    )(qit, qit, iwt, k1, k2, vt, bias_tbl, tri)


def _mlp_kernel(x_ref, yr_ref, yd_ref, wor_ref, wod_ref, n2_ref, w1_ref, w2_ref, o_ref):
    h1 = x_ref[...] + _dot(yr_ref[...], wor_ref[...]) + _dot(yd_ref[...], wod_ref[...])
    ms = jnp.mean(h1 * h1, axis=-1, keepdims=True)
    u = (h1 * lax.rsqrt(ms + EPS) * n2_ref[...]).astype(jnp.bfloat16)
    o_ref[...] = h1
    d_ff = w1_ref.shape[1]
    for c in range(d_ff // FFN_CHUNK):
        sl = slice(c * FFN_CHUNK, (c + 1) * FFN_CHUNK)
        f = jnp.maximum(_dot(u, w1_ref[:, sl]), 0.0)
        o_ref[...] += _dot((f * f).astype(jnp.bfloat16), w2_ref[sl, :])


def _out_mlp(x2, yr, yd, wor, wod, n2, w1, w2):
    rows, D = x2.shape
    const = lambda a: pl.BlockSpec(a.shape, lambda i: (0, 0), pipeline_mode=pl.Buffered(1))
    tile = lambda w: pl.BlockSpec((ROW_TILE, w), lambda i: (i, 0))
    return pl.pallas_call(
        _mlp_kernel,
        grid=(rows // ROW_TILE,),
        in_specs=[tile(D), tile(yr.shape[1]), tile(yd.shape[1]),
                  const(wor), const(wod), const(n2), const(w1), const(w2)],
        out_specs=tile(D),
        out_shape=jax.ShapeDtypeStruct((rows, D), jnp.float32),
        compiler_params=pltpu.CompilerParams(
            dimension_semantics=("arbitrary",), vmem_limit_bytes=VMEM_LIMIT),
        name="out_mlp",
    )(x2, yr, yd, wor, wod, n2, w1, w2)


def kernel(x, meta_tokens, norm1_w, w_in, ret_norm_w, q_norm_w, k_norm_w, rel_bias,
           w_out, norm2_w, w_ff1, w_ff2):
    B, L, D = x.shape
    assert L % (2 * CHUNK) == 0 and L % ROW_TILE == 0 and w_in.shape[0] == 1
    topk = min(TOPK_MAX, L // 4)
    nb = L // CHUNK + 1
    bf = jnp.bfloat16
    f32 = jnp.float32

    rw = RET_HEADS * RET_DK
    dw = DSA_HEADS * DSA_DH
    sizes = (rw, rw, rw, rw, dw, DSA_DH, DSA_DH, IDX_HEADS * DSA_DH, DSA_DH, IDX_HEADS)
    offs = np.concatenate([[0], np.cumsum(sizes)])
    col = lambda i: w_in[0][:, int(offs[i]):int(offs[i + 1])]
    wr = jnp.concatenate([col(0), col(1), col(2), col(3)], axis=1).astype(bf)
    wq = col(4).reshape(D, DSA_HEADS, DSA_DH)
    wiq = col(7).reshape(D, IDX_HEADS, DSA_DH)
    wqi = jnp.concatenate([wq, wiq], axis=2).reshape(D, DSA_HEADS * LANES).astype(bf)
    wk = jnp.concatenate([col(5), col(8)], axis=1).astype(bf)
    wv = jnp.concatenate([col(6), jnp.zeros((D, LANES - DSA_DH), f32)], axis=1).astype(bf)
    ww = jnp.concatenate([col(9), jnp.zeros((D, LANES - IDX_HEADS), f32)], axis=1).astype(bf)

    idx_scale = jnp.full((DSA_DH,), DSA_DH ** -0.5, f32)
    qscale = jnp.concatenate([q_norm_w[0].astype(f32) * (DSA_DH ** -0.5 * LOG2E), idx_scale])[None]
    kscale = jnp.concatenate([k_norm_w[0].astype(f32), jnp.ones((DSA_DH,), f32)])[None]
    meta_pad = jnp.concatenate([jnp.zeros((PAD, D), x.dtype), meta_tokens.astype(x.dtype)], axis=0)

    cos2, sin2 = _rope_tables(nb)
    bias_tbl = _bias_tables(rel_bias)
    tri = jnp.asarray(np.tril(np.ones((2 * CHUNK, 2 * CHUNK), np.float32), k=-1), dtype=bf)

    r_all, qit, k1, k2, vt, iwt = _inproj(x, meta_pad, norm1_w[0][None].astype(f32),
                                          wr, wqi, wk, wv, ww, qscale, kscale)
    y_ret = _retention(r_all, nb, cos2, sin2, ret_norm_w[0][None].astype(f32))
    y_dsa = _sparse_attention(qit, iwt, k1, k2, vt, bias_tbl, tri, topk, nb)

    wo = w_out[0].astype(bf)
    out = _out_mlp(x.reshape(B * L, D), y_ret.reshape(B * L, rw), y_dsa.reshape(B * L, dw),
                   wo[:rw], wo[rw:], norm2_w[0][None].astype(f32),
                   w_ff1[0].astype(bf), w_ff2[0].astype(bf))
    return out.reshape(B, L, D)
```

```python
import functools
import math

import numpy as np
import jax
import jax.numpy as jnp
from jax import lax
from jax.experimental import pallas as pl
from jax.experimental.pallas import tpu as pltpu

N_META = 16
CHUNK = 128
RET_HEADS = 4
RET_DK = 128
DSA_HEADS = 8
DSA_DH = 64
IDX_HEADS = 8
TOPK_MAX = 256
N_BUCKETS = 32
MAX_DISTANCE = 128
ROPE_BASE = 10000.0
EPS = 1e-6
NEG = -1e30
PAD = CHUNK - N_META
LANES = 128
SUBLANES = 8
BF16_ROWS = 16
HALF = LANES // 2
INT_MIN = -(2 ** 31)
LOG2E = math.log2(math.e)

FFN_CHUNK = 1024
ROW_TILE = 512
VMEM_LIMIT = 56 * 1024 * 1024


def _dot(a, b):
    return jnp.dot(a, b, preferred_element_type=jnp.float32)


def _dot_nt(a, b):
    return lax.dot_general(a, b, (((1,), (1,)), ((), ())), preferred_element_type=jnp.float32)


def _dot_tn(a, b):
    return lax.dot_general(a, b, (((0,), (0,)), ((), ())), preferred_element_type=jnp.float32)


def _bucket_ranges():
    max_exact = N_BUCKETS // 2
    d = np.arange(0, 2 * CHUNK)
    large = max_exact + (np.log(np.maximum(d, 1) / max_exact) / math.log(MAX_DISTANCE / max_exact)
                         * (N_BUCKETS - max_exact)).astype(np.int64)
    bucket = np.where(d < max_exact, d, np.minimum(large, N_BUCKETS - 1))
    out = []
    for b in range(N_BUCKETS - 1):
        idx = np.nonzero(bucket == b)[0]
        out.append((int(idx.min()), int(idx.max())))
    return out


def _rope_kernel(cos_ref, sin_ref):
    n = pl.program_id(0)
    row = lax.broadcasted_iota(jnp.int32, (CHUNK, LANES), 0)
    lane = lax.broadcasted_iota(jnp.int32, (CHUNK, LANES), 1)
    pos = (n * CHUNK + row - PAD).astype(jnp.float32)
    frac = (lane % HALF).astype(jnp.float32) / HALF
    inv = jnp.exp(-frac * math.log(ROPE_BASE))
    ang = pos * inv
    cos_ref[...] = jnp.cos(ang)
    s = jnp.sin(ang)
    sin_ref[...] = jnp.where(lane < HALF, -s, s)


def _rope_tables(nb):
    return pl.pallas_call(
        _rope_kernel,
        grid=(nb,),
        out_specs=[pl.BlockSpec((CHUNK, LANES), lambda n: (n, 0))] * 2,
        out_shape=[jax.ShapeDtypeStruct((nb * CHUNK, LANES), jnp.float32)] * 2,
        name="rope_tables",
    )()


def _bias_kernel(rb_ref, out_ref):
    row = lax.broadcasted_iota(jnp.int32, (CHUNK, CHUNK), 0)
    col = lax.broadcasted_iota(jnp.int32, (CHUNK, CHUNK), 1)
    ranges = _bucket_ranges()
    out_ref[2] = jnp.zeros(out_ref.shape[1:], jnp.float32)
    for h in range(DSA_HEADS):
        far = rb_ref[N_BUCKETS - 1, h]
        for t in range(2):
            dist = col - row + t * CHUNK
            tile = jnp.zeros((CHUNK, CHUNK), jnp.float32)
            for b, (lo, hi) in enumerate(ranges):
                tile = jnp.where((dist >= lo) & (dist <= hi), (rb_ref[b, h] - far) * LOG2E, tile)
            out_ref[t, :, h * LANES:(h + 1) * LANES] = tile


def _bias_tables(rel_bias):
    return pl.pallas_call(
        _bias_kernel,
        in_specs=[pl.BlockSpec(memory_space=pltpu.SMEM)],
        out_specs=pl.BlockSpec(memory_space=pltpu.VMEM),
        out_shape=jax.ShapeDtypeStruct((3, CHUNK, DSA_HEADS * LANES), jnp.float32),
        name="bias_tables",
    )(rel_bias.astype(jnp.float32))


def _inproj_kernel(xa_ref, xb_ref, meta_ref, n1_ref, wr_ref, wqi_ref, wk_ref, wv_ref, ww_ref,
                   qs_ref, ks_ref,
                   r_ref, qit_ref, k1_ref, k2_ref, vt_ref, iwt_ref, *, pad_last):
    s = pl.program_id(1)
    top = jnp.where(s == 0, meta_ref[...], xa_ref[0])
    bot = xb_ref[0]
    if pad_last:
        bot = jnp.where(s == pl.num_programs(1) - 1, 0.0, bot)
    src = jnp.concatenate([top, bot], axis=0)
    ms = jnp.mean(src * src, axis=-1, keepdims=True)
    u = (src * lax.rsqrt(ms + EPS) * n1_ref[...]).astype(jnp.bfloat16)

    r_ref[0] = _dot(u, wr_ref[...])

    lane = lax.broadcasted_iota(jnp.int32, (2 * CHUNK, LANES), 1)
    lo = lane < HALF

    def head_rms(t):
        ssq = jnp.sum(jnp.where(lo, t * t, 0.0), axis=-1, keepdims=True)
        return lax.rsqrt(ssq / DSA_DH + EPS)

    pq = _dot(u, wqi_ref[...])
    for h in range(DSA_HEADS):
        t = pq[:, h * LANES:(h + 1) * LANES]
        t = t * (jnp.where(lo, head_rms(t), 1.0) * qs_ref[...])
        for j in range(2):
            qit_ref[0, j, :, h * LANES:(h + 1) * LANES] = (
                t[j * CHUNK:(j + 1) * CHUNK].T.astype(jnp.bfloat16))

    pk = _dot(u, wk_ref[...])
    kn = pk * head_rms(pk) * ks_ref[...]
    k1_ref[0] = jnp.where(lo, kn, 0.0).astype(jnp.bfloat16)
    k2_ref[0] = jnp.where(lo, 0.0, pk).astype(jnp.bfloat16)

    pv = _dot(u, wv_ref[...])
    vx = jnp.where(lo, pv, jnp.where(lane == HALF, 1.0, 0.0))
    iw = _dot(u, ww_ref[...]) * (IDX_HEADS ** -0.5)
    for j in range(2):
        rows = slice(j * CHUNK, (j + 1) * CHUNK)
        vt_ref[0, :, rows] = vx[rows].T.astype(jnp.bfloat16)
        iwt_ref[0, j] = iw[rows].T[0:IDX_HEADS, :]


def _inproj(x, meta_pad, n1, wr, wqi, wk, wv, ww, qscale, kscale):
    B, L, D = x.shape
    nx = L // CHUNK
    nbp = nx + 1 + ((nx + 1) % 2)
    tp = nbp * CHUNK
    const = lambda shape: pl.BlockSpec(shape, lambda b, s: (0,) * len(shape))
    blk = lambda w: pl.BlockSpec((1, 2 * CHUNK, w), lambda b, s: (b, s, 0))
    return pl.pallas_call(
        functools.partial(_inproj_kernel, pad_last=bool((nx + 1) % 2)),
        grid=(B, nbp // 2),
        in_specs=[
            pl.BlockSpec((1, CHUNK, D), lambda b, s: (b, jnp.maximum(2 * s - 1, 0), 0)),
            pl.BlockSpec((1, CHUNK, D), lambda b, s: (b, jnp.minimum(2 * s, nx - 1), 0)),
            const(meta_pad.shape), const(n1.shape), const(wr.shape), const(wqi.shape),
            const(wk.shape), const(wv.shape), const(ww.shape), const(qscale.shape),
            const(kscale.shape),
        ],
        out_specs=[
            blk(wr.shape[1]),
            pl.BlockSpec((1, 2, LANES, DSA_HEADS * LANES), lambda b, s: (b, s, 0, 0)),
            blk(LANES), blk(LANES),
            pl.BlockSpec((1, LANES, 2 * CHUNK), lambda b, s: (b, 0, s)),
            pl.BlockSpec((1, 2, IDX_HEADS, CHUNK), lambda b, s: (b, s, 0, 0)),
        ],
        out_shape=[
            jax.ShapeDtypeStruct((B, tp, wr.shape[1]), jnp.float32),
            jax.ShapeDtypeStruct((B, nbp, LANES, DSA_HEADS * LANES), jnp.bfloat16),
            jax.ShapeDtypeStruct((B, tp, LANES), jnp.bfloat16),
            jax.ShapeDtypeStruct((B, tp, LANES), jnp.bfloat16),
            jax.ShapeDtypeStruct((B, LANES, tp), jnp.bfloat16),
            jax.ShapeDtypeStruct((B, nbp, IDX_HEADS, CHUNK), jnp.float32),
        ],
        compiler_params=pltpu.CompilerParams(
            dimension_semantics=("arbitrary", "arbitrary"), vmem_limit_bytes=VMEM_LIMIT),
        name="in_projection",
    )(x, x, meta_pad, n1, wr, wqi, wk, wv, ww, qscale, kscale)


def _retention_kernel(ra_ref, rb_ref, cosa_ref, sina_ref, cosb_ref, sinb_ref, gain_ref, y_ref,
                      state_ref, tbl_ref):
    s = pl.program_id(1)
    w = RET_HEADS * RET_DK
    bf = jnp.bfloat16

    @pl.when(s == 0)
    def _():
        state_ref[...] = jnp.zeros_like(state_ref)
        row = lax.broadcasted_iota(jnp.int32, (CHUNK, CHUNK), 0)
        col = lax.broadcasted_iota(jnp.int32, (CHUNK, CHUNK), 1)
        rowf = row.astype(jnp.float32)
        diff = (row - col).astype(jnp.float32)
        for h in range(RET_HEADS):
            log_gamma = math.log(1.0 - 2.0 ** (-5.0 - h))
            tbl_ref[0, h] = jnp.where(diff >= 0, jnp.exp(log_gamma * jnp.maximum(diff, 0.0)), 0.0)
            tbl_ref[1, h] = jnp.exp(log_gamma * (rowf + 1.0))
            tbl_ref[2, h] = jnp.exp(log_gamma * (CHUNK - 1.0 - rowf))

    live = jnp.where(s == 0, 0.0, 1.0)
    chunks = ((ra_ref, cosa_ref, sina_ref, live), (rb_ref, cosb_ref, sinb_ref, None))
    for h in range(RET_HEADS):
        log_gamma = math.log(1.0 - 2.0 ** (-5.0 - h))
        sl = slice(h * RET_DK, (h + 1) * RET_DK)
        state = state_ref[h]
        for j, (r_ref, cos_ref, sin_ref, scale) in enumerate(chunks):
            cos2, sin2 = cos_ref[...], sin_ref[...]
            q = r_ref[0, :, sl]
            k = r_ref[0, :, w + h * RET_DK: w + (h + 1) * RET_DK]
            v = r_ref[0, :, 2 * w + h * RET_DK: 2 * w + (h + 1) * RET_DK]
            g = r_ref[0, :, 3 * w + h * RET_DK: 3 * w + (h + 1) * RET_DK]
            if scale is not None:
                k = k * scale
            v = v.astype(bf)
            qr = q * cos2 + pltpu.roll(q, HALF, 1) * sin2
            kr = (k * cos2 + pltpu.roll(k, HALF, 1) * sin2) * (RET_DK ** -0.5)

            scores = _dot_nt(qr.astype(bf), kr.astype(bf)) * tbl_ref[0, h]
            o = _dot(scores.astype(bf), v)
            o = o + _dot((qr * tbl_ref[1, h]).astype(bf), state.astype(bf))
            kv = _dot_tn((kr * tbl_ref[2, h]).astype(bf), v)
            state = state * math.exp(log_gamma * CHUNK) + kv

            ms = jnp.mean(o * o, axis=-1, keepdims=True)
            on = o * lax.rsqrt(ms + EPS) * gain_ref[:, sl]
            gate = g * (1.0 / (1.0 + jnp.exp(-g)))
            y_ref[0, j * CHUNK:(j + 1) * CHUNK, sl] = (gate * on).astype(y_ref.dtype)
        state_ref[h] = state


def _retention(r_all, nb, cos2, sin2, gain):
    B = r_all.shape[0]
    w = RET_HEADS * RET_DK
    nstep = (nb + 1) // 2
    rspec = lambda f: pl.BlockSpec((1, CHUNK, 4 * w), lambda b, s: (b, f(s), 0))
    tspec = lambda f: pl.BlockSpec((CHUNK, LANES), lambda b, s: (f(s), 0))
    first = lambda s: jnp.maximum(2 * s - 1, 0)
    second = lambda s: 2 * s
    return pl.pallas_call(
        _retention_kernel,
        grid=(B, nstep),
        in_specs=[rspec(first), rspec(second), tspec(first), tspec(first), tspec(second),
                  tspec(second), pl.BlockSpec((1, w), lambda b, s: (0, 0))],
        out_specs=pl.BlockSpec((1, 2 * CHUNK, w), lambda b, s: (b, jnp.maximum(s - 1, 0), 0)),
        out_shape=jax.ShapeDtypeStruct((B, (nb - 1) * CHUNK, w), jnp.bfloat16),
        scratch_shapes=[pltpu.VMEM((RET_HEADS, RET_DK, RET_DK), jnp.float32),
                        pltpu.VMEM((3, RET_HEADS, CHUNK, CHUNK), jnp.float32)],
        compiler_params=pltpu.CompilerParams(
            dimension_semantics=("arbitrary", "arbitrary"), vmem_limit_bytes=VMEM_LIMIT),
        name="retention",
    )(r_all, r_all, cos2, sin2, cos2, sin2, gain)


def _bit_planes(tile):
    a = [tile[r * 8:(r + 1) * 8, :] for r in range(32)]
    j, m = 16, 0x0000FFFF
    while j:
        for k in range(32):
            if (k & j) == 0:
                t = (a[k] ^ lax.shift_right_logical(a[k + j], j)) & m
                a[k] = a[k] ^ t
                a[k + j] = a[k + j] ^ (t << j)
        j >>= 1
        if j:
            m = m ^ ((m << j) & 0xFFFFFFFF)
    return a


def _sublane_total(x):
    x = x + pltpu.roll(x, 4, 0)
    x = x + pltpu.roll(x, 2, 0)
    return x + pltpu.roll(x, 1, 0)


def _dsa_kernel(qa_ref, qb_ref, iwt_ref, k1_ref, k2_ref, vt_ref, bias_ref, tri_ref, y_ref,
                key_ref, plane_ref, thr_ref, ntie_ref, flag_ref, m_ref, acc_ref, sa_ref, sb_ref,
                rhs_ref, *, topk):
    t = pl.program_id(1)
    last_q = pl.num_programs(1) - 1
    bf = jnp.bfloat16
    H = DSA_HEADS
    KP = 2 * CHUNK
    G = plane_ref.shape[1]
    cur, nxt = t % 2, (t + 1) % 2

    @pl.when((pl.program_id(0) == 0) & (t == 0))
    def _():
        plane_ref[...] = jnp.zeros_like(plane_ref)
        key_ref[...] = jnp.zeros_like(key_ref)
        eye = (lax.broadcasted_iota(jnp.int32, (CHUNK, CHUNK), 0)
               == lax.broadcasted_iota(jnp.int32, (CHUNK, CHUNK), 1))
        for h in range(DSA_HEADS):
            rhs_ref[CHUNK:2 * CHUNK, h * LANES:(h + 1) * LANES] = jnp.where(eye, 1.0, 0.0).astype(bf)

    @pl.when(t == 0)
    def _():
        thr_ref[0] = jnp.zeros(thr_ref.shape[1:], jnp.float32)
        ntie_ref[0] = jnp.zeros(ntie_ref.shape[1:], jnp.float32)
        flag_ref[0] = 0

    row = lax.broadcasted_iota(jnp.int32, (KP, CHUNK), 0)
    col = lax.broadcasted_iota(jnp.int32, (KP, CHUNK), 1)
    hs = lambda h: slice(h * LANES, (h + 1) * LANES)
    hp = lambda c: slice(2 * c * LANES, 2 * (c + 1) * LANES)
    nsteps = lambda n: (n + 2) // 2

    def key_rows(g):
        last = k1_ref.shape[1] // KP - 1
        return pl.ds(pl.multiple_of(jnp.minimum(g, last) * KP, KP), KP)

    blk_b = t + 1
    steps_b = jnp.where(t < last_q, nsteps(blk_b), 0)
    iters_b, tail_b = steps_b // 2, steps_b % 2
    qb_idx = blk_b * CHUNK + col
    iw_rows = [jnp.broadcast_to(iwt_ref[0, 0, h:h + 1, :], (KP, CHUNK)) for h in range(IDX_HEADS)]

    def score_pair(g):
        kblk = k2_ref[0, key_rows(g), :]
        sc = None
        for c in range(H // 2):
            s = _dot(kblk, qb_ref[0, 0, :, hp(c)])
            for hh in range(2):
                term = iw_rows[2 * c + hh] * jnp.maximum(s[:, hs(hh)], 0.0)
                sc = term if sc is None else sc + term
        k_idx = g * KP + row
        valid = (k_idx <= qb_idx) & (k_idx >= PAD)
        sc = jnp.where(valid, sc, NEG)
        key_ref[nxt, g] = sc
        bits = lax.bitcast_convert_type(sc, jnp.int32)
        key = jnp.where(bits < 0, jnp.int32(INT_MIN) - bits, bits)
        planes = _bit_planes(key ^ jnp.int32(INT_MIN))
        for p in range(32):
            plane_ref[p, g] = planes[p]

    def score_two(i):
        score_pair(2 * i)
        score_pair(2 * i + 1)

    blk_a = t
    steps_a = jnp.where(t > 0, nsteps(blk_a), 0)
    iters_a, tail_a = steps_a // 2, steps_a % 2
    far_a = jnp.where(t > 0, ((blk_a - 1) // 2) // 2, 0)
    rep = lambda x: jnp.tile(x, (KP // SUBLANES, 1))
    thr = thr_ref[cur]
    few = thr <= 0.999 * NEG
    thr_b = rep(jnp.where(few, 0.5 * NEG, thr))
    thr_sel_b = thr_b
    n_tie_b = rep(jnp.where(few, 0.0, ntie_ref[cur]))
    has_ties = flag_ref[cur] > 0
    va = acc_ref.shape[0]

    @pl.when(t > 0)
    def _():
        m_ref[...] = jnp.full_like(m_ref, NEG)
        acc_ref[...] = jnp.zeros_like(acc_ref)
        rhs_ref[0:CHUNK, :] = qa_ref[0, 0]

    def logits(g, dst_ref, ties):
        kblk = k1_ref[0, key_rows(g), :]
        if ties:
            dst_ref[...] = _dot(kblk, qa_ref[0, 0])
        else:
            off = jnp.where(key_ref[cur, g] >= thr_sel_b, 0.0, NEG).astype(bf)
            dst_ref[...] = _dot(jnp.concatenate([kblk, off], axis=1), rhs_ref[...])

    def attend(g, s_ref, ties_before, near, ties):
        if ties:
            key = key_ref[cur, g]
            eq = key == thr_b
            eqf = jnp.where(eq, 1.0, 0.0)
            before = ties_before + _dot(tri_ref[...], eqf.astype(bf))
            sel = (key > thr_b) | (eq & (before < n_tie_b))
            ties_before = ties_before + jnp.sum(eqf, axis=0, keepdims=True)
        ps, alphas = [], []
        for h in range(H):
            s = s_ref[:, hs(h)]
            if near:
                ta = jnp.clip(blk_a - 2 * g, 0, 2)
                tb = jnp.clip(blk_a - 2 * g - 1, 0, 2)
                s = s + jnp.concatenate([bias_ref[ta, :, hs(h)], bias_ref[tb, :, hs(h)]], axis=0)
            if ties:
                s = jnp.where(sel, s, NEG)
            m_prev = m_ref[h:h + 1, :]
            m_new = jnp.maximum(m_prev, jnp.max(s, axis=0, keepdims=True))
            m_ref[h:h + 1, :] = m_new
            alphas.append(jnp.exp2(m_prev - m_new))
            ps.append(jnp.exp2(s - m_new).astype(bf))
        pv = _dot(vt_ref[0, 0:va, key_rows(g)], jnp.concatenate(ps, axis=1))
        acc_ref[...] = acc_ref[...] * jnp.concatenate(alphas, axis=1) + pv
        return ties_before

    def attend_two(i, ties_before, near, ties, fused):
        logits(2 * i + 1, sb_ref, ties)
        ties_before = attend(2 * i, sa_ref, ties_before, near, ties)
        logits(2 * i + 2, sa_ref, ties)
        ties_before = attend(2 * i + 1, sb_ref, ties_before, near, ties)
        if fused:
            score_two(i)
        return ties_before

    zero_ties = jnp.zeros((KP, CHUNK), jnp.float32)
    loop = lambda lo, hi, **kw: (lambda c: lax.fori_loop(lo, hi, functools.partial(attend_two, **kw), c))
    fused_hi = jnp.minimum(iters_a, iters_b)

    @pl.when(has_ties)
    def _():
        logits(0, sa_ref, True)
        c = loop(0, far_a, near=False, ties=True, fused=False)(zero_ties)
        c = loop(far_a, iters_a, near=True, ties=True, fused=False)(c)

        @pl.when(tail_a == 1)
        def _():
            attend(2 * iters_a, sa_ref, c, True, True)

    @pl.when(jnp.logical_not(has_ties))
    def _():
        logits(0, sa_ref, False)
        far_hi = jnp.minimum(far_a, fused_hi)
        loop(0, far_hi, near=False, ties=False, fused=True)(zero_ties)
        loop(far_hi, fused_hi, near=True, ties=False, fused=True)(zero_ties)
        loop(fused_hi, iters_a, near=True, ties=False, fused=False)(zero_ties)

        @pl.when(tail_a == 1)
        def _():
            attend(2 * iters_a, sa_ref, zero_ties, True, False)

    lax.fori_loop(jnp.where(has_ties, 0, fused_hi), iters_b, lambda i, _: (score_two(i), 0)[1], 0)

    @pl.when(tail_b == 1)
    def _():
        score_pair(2 * iters_b)

    @pl.when(t > 0)
    def _():
        lo = lax.broadcasted_iota(jnp.int32, (CHUNK, CHUNK), 1) < HALF
        for c in range(H // 2):
            tiles = []
            for h in (2 * c, 2 * c + 1):
                a = acc_ref[:, hs(h)] * (1.0 / acc_ref[HALF:HALF + 1, hs(h)])
                a = jnp.concatenate([a, jnp.zeros((CHUNK - va, CHUNK), jnp.float32)], axis=0)
                tiles.append(a.T)
            y_ref[0, :, hs(c)] = jnp.where(lo, tiles[0], pltpu.roll(tiles[1], HALF, 1)).astype(y_ref.dtype)

    @pl.when(t < last_q)
    def _():
        npair_b = (blk_b + 2) // 2
        g_idx = lax.broadcasted_iota(jnp.int32, (G, 8, CHUNK), 0)
        alive0 = jnp.where(g_idx < npair_b, jnp.int32(-1), jnp.int32(0))

        def ones_in(words):
            return _sublane_total(lax.population_count(words).sum(axis=0))

        def radix(i, carry):
            alive, above, thr_u = carry
            hi_plane, lo_plane = plane_ref[2 * i], plane_ref[2 * i + 1]
            a1 = alive & hi_plane
            a0 = alive ^ a1
            a11, a01 = a1 & lo_plane, a0 & lo_plane
            c1, c11, c01 = ones_in(a1), ones_in(a11), ones_in(a01)
            hi = (above + c1) >= topk
            above = jnp.where(hi, above, above + c1)
            c_lo = jnp.where(hi, c11, c01)
            lo = (above + c_lo) >= topk
            above = jnp.where(lo, above, above + c_lo)
            bit = lax.shift_right_logical(jnp.int32(INT_MIN), 2 * i)
            thr_u = (thr_u | jnp.where(hi, bit, 0)
                     | jnp.where(lo, lax.shift_right_logical(bit, 1), 0))
            group = jnp.where(hi[None], a1, a0)
            with_lo = jnp.where(hi[None], a11, a01)
            alive = jnp.where(lo[None], with_lo, group ^ with_lo)
            return alive, above, thr_u

        zero8 = jnp.zeros((8, CHUNK), jnp.int32)
        thr_u = lax.fori_loop(0, 16, radix, (alive0, zero8, zero8))[2]
        key_k = thr_u ^ jnp.int32(INT_MIN)
        as_score = lambda k: lax.bitcast_convert_type(
            jnp.where(k < 0, jnp.int32(INT_MIN) - k, k), jnp.float32)
        k_f = jnp.float32(topk)

        def count_ge(*ps):
            ps_b = [rep(p) for p in ps]

            def body(g, parts):
                sc = key_ref[nxt, g]
                return tuple(
                    part + jnp.where(sc >= p_b, 1.0, 0.0).reshape(KP // SUBLANES, SUBLANES, CHUNK).sum(axis=0)
                    for part, p_b in zip(parts, ps_b))

            zero = jnp.zeros((SUBLANES, CHUNK), jnp.float32)
            return [_sublane_total(c) for c in lax.fori_loop(0, steps_b, body, (zero,) * len(ps))]

        def pending(lo, hi, c_lo, c_hi):
            mid = 0.5 * (lo + hi)
            final = (c_lo >= k_f) & (c_hi < k_f) & ((c_lo == k_f) | (mid <= lo) | (mid >= hi))
            return jnp.logical_not(final)

        def status(lo, hi, c_lo, c_hi):
            tied = (c_lo > k_f) & (lo > 0.999 * NEG)
            return jnp.max(jnp.where(pending(lo, hi, c_lo, c_hi), 2.0, jnp.where(tied, 1.0, 0.0)))

        def refine(state):
            lo, hi, c_lo, c_hi, step, it, _ = state
            low_bad, high_bad = c_lo < k_f, c_hi >= k_f
            p = jnp.where(low_bad, lo - step, jnp.where(high_bad, hi + step, 0.5 * (lo + hi)))
            c_p, = count_ge(p)
            todo = pending(lo, hi, c_lo, c_hi)
            down = todo & low_bad
            up = todo & jnp.logical_not(low_bad) & high_bad
            split = todo & jnp.logical_not(low_bad | high_bad)
            as_lo = down | (split & (c_p >= k_f))
            as_hi = up | (split & (c_p < k_f))
            lo, c_lo, hi, c_hi = (
                jnp.where(as_lo, p, jnp.where(up, hi, lo)),
                jnp.where(as_lo, c_p, jnp.where(up, c_hi, c_lo)),
                jnp.where(as_hi, p, jnp.where(down, lo, hi)),
                jnp.where(as_hi, c_p, jnp.where(down, c_lo, c_hi)))
            return (lo, hi, c_lo, c_hi, jnp.where(down | up, 2.0 * step, step), it + 1,
                    status(lo, hi, c_lo, c_hi))

        lo0, hi0 = as_score(key_k), as_score(key_k + 1)
        c_lo0, c_hi0 = count_ge(lo0, hi0)
        state = (lo0, hi0, c_lo0, c_hi0, jnp.maximum(jnp.abs(lo0), 1e-30) * 2.0 ** -20,
                 jnp.int32(0), status(lo0, hi0, c_lo0, c_hi0))
        lo, hi, c_lo, c_hi, _, _, left = lax.while_loop(
            lambda st: (st[6] > 1.5) & (st[5] < 200), refine, state)
        masked_n = lo <= 0.999 * NEG
        exact_n = c_lo == k_f
        thr_ref[nxt] = lo
        ntie_ref[nxt] = jnp.where(masked_n, 0.0, jnp.where(exact_n, jnp.float32(2 * KP * G), k_f - c_hi))
        flag_ref[nxt] = (left > 0.5).astype(jnp.int32)


def _sparse_attention(qit, iwt, k1, k2, vt, bias_tbl, tri, topk, nb):
    B, nbp, _, hw = qit.shape
    tp = nbp * CHUNK
    nstep = nbp // 2
    nstep += nstep % 2
    kspec = pl.BlockSpec((1, tp, LANES), lambda b, t: (b, 0, 0))
    acc_rows = DSA_DH + BF16_ROWS
    nxt = lambda t: jnp.minimum(t + 1, nb - 1)
    return pl.pallas_call(
        functools.partial(_dsa_kernel, topk=topk),
        grid=(B, nb),
        in_specs=[
            pl.BlockSpec((1, 1, LANES, hw), lambda b, t: (b, t, 0, 0)),
            pl.BlockSpec((1, 1, LANES, hw), lambda b, t: (b, nxt(t), 0, 0)),
            pl.BlockSpec((1, 1, IDX_HEADS, CHUNK), lambda b, t: (b, nxt(t), 0, 0)),
            kspec, kspec,
            pl.BlockSpec((1, LANES, tp), lambda b, t: (b, 0, 0)),
            pl.BlockSpec(bias_tbl.shape, lambda b, t: (0, 0, 0)),
            pl.BlockSpec(tri.shape, lambda b, t: (0, 0)),
        ],
        out_specs=pl.BlockSpec((1, CHUNK, DSA_HEADS * DSA_DH),
                               lambda b, t: (b, jnp.maximum(t - 1, 0), 0)),
        out_shape=jax.ShapeDtypeStruct((B, (nb - 1) * CHUNK, DSA_HEADS * DSA_DH), jnp.bfloat16),
        scratch_shapes=[
            pltpu.VMEM((2, nstep, 2 * CHUNK, CHUNK), jnp.float32),
            pltpu.VMEM((32, nstep, 8, CHUNK), jnp.int32),
            pltpu.VMEM((2, 8, CHUNK), jnp.float32),
            pltpu.VMEM((2, 8, CHUNK), jnp.float32),
            pltpu.SMEM((2,), jnp.int32),
            pltpu.VMEM((DSA_HEADS, CHUNK), jnp.float32),
            pltpu.VMEM((acc_rows, hw), jnp.float32),
            pltpu.VMEM((2 * CHUNK, hw), jnp.float32),
            pltpu.VMEM((2 * CHUNK, hw), jnp.float32),
            pltpu.VMEM((2 * CHUNK, hw), jnp.bfloat16),
        ],
        compiler_params=pltpu.CompilerParams(
            dimension_semantics=("arbitrary", "arbitrary"), vmem_limit_bytes=VMEM_LIMIT),
        name="sparse_attention",
    )(qit, qit, iwt, k1, k2, vt, bias_tbl, tri)


def _mlp_kernel(x_ref, yr_ref, yd_ref, wor_ref, wod_ref, n2_ref, w1_ref, w2_ref, o_ref):
    h1 = x_ref[...] + _dot(yr_ref[...], wor_ref[...]) + _dot(yd_ref[...], wod_ref[...])
    ms = jnp.mean(h1 * h1, axis=-1, keepdims=True)
    u = (h1 * lax.rsqrt(ms + EPS) * n2_ref[...]).astype(jnp.bfloat16)
    o_ref[...] = h1
    d_ff = w1_ref.shape[1]
    for c in range(d_ff // FFN_CHUNK):
        sl = slice(c * FFN_CHUNK, (c + 1) * FFN_CHUNK)
        f = jnp.maximum(_dot(u, w1_ref[:, sl]), 0.0)
        o_ref[...] += _dot((f * f).astype(jnp.bfloat16), w2_ref[sl, :])


def _out_mlp(x2, yr, yd, wor, wod, n2, w1, w2):
    rows, D = x2.shape
    const = lambda a: pl.BlockSpec(a.shape, lambda i: (0, 0), pipeline_mode=pl.Buffered(1))
    tile = lambda w: pl.BlockSpec((ROW_TILE, w), lambda i: (i, 0))
    return pl.pallas_call(
        _mlp_kernel,
        grid=(rows // ROW_TILE,),
        in_specs=[tile(D), tile(yr.shape[1]), tile(yd.shape[1]),
                  const(wor), const(wod), const(n2), const(w1), const(w2)],
        out_specs=tile(D),
        out_shape=jax.ShapeDtypeStruct((rows, D), jnp.float32),
        compiler_params=pltpu.CompilerParams(
            dimension_semantics=("arbitrary",), vmem_limit_bytes=VMEM_LIMIT),
        name="out_mlp",
    )(x2, yr, yd, wor, wod, n2, w1, w2)


def kernel(x, meta_tokens, norm1_w, w_in, ret_norm_w, q_norm_w, k_norm_w, rel_bias,
           w_out, norm2_w, w_ff1, w_ff2):
    B, L, D = x.shape
    assert L % (2 * CHUNK) == 0 and L % ROW_TILE == 0 and w_in.shape[0] == 1
    topk = min(TOPK_MAX, L // 4)
    nb = L // CHUNK + 1
    bf = jnp.bfloat16
    f32 = jnp.float32

    rw = RET_HEADS * RET_DK
    dw = DSA_HEADS * DSA_DH
    sizes = (rw, rw, rw, rw, dw, DSA_DH, DSA_DH, IDX_HEADS * DSA_DH, DSA_DH, IDX_HEADS)
    offs = np.concatenate([[0], np.cumsum(sizes)])
    col = lambda i: w_in[0][:, int(offs[i]):int(offs[i + 1])]
    wr = jnp.concatenate([col(0), col(1), col(2), col(3)], axis=1).astype(bf)
    wq = col(4).reshape(D, DSA_HEADS, DSA_DH)
    wiq = col(7).reshape(D, IDX_HEADS, DSA_DH)
    wqi = jnp.concatenate([wq, wiq], axis=2).reshape(D, DSA_HEADS * LANES).astype(bf)
    wk = jnp.concatenate([col(5), col(8)], axis=1).astype(bf)
    wv = jnp.concatenate([col(6), jnp.zeros((D, LANES - DSA_DH), f32)], axis=1).astype(bf)
    ww = jnp.concatenate([col(9), jnp.zeros((D, LANES - IDX_HEADS), f32)], axis=1).astype(bf)

    idx_scale = jnp.full((DSA_DH,), DSA_DH ** -0.5, f32)
    qscale = jnp.concatenate([q_norm_w[0].astype(f32) * (DSA_DH ** -0.5 * LOG2E), idx_scale])[None]
    kscale = jnp.concatenate([k_norm_w[0].astype(f32), jnp.ones((DSA_DH,), f32)])[None]
    meta_pad = jnp.concatenate([jnp.zeros((PAD, D), x.dtype), meta_tokens.astype(x.dtype)], axis=0)

    cos2, sin2 = _rope_tables(nb)
    bias_tbl = _bias_tables(rel_bias)
    tri = jnp.asarray(np.tril(np.ones((2 * CHUNK, 2 * CHUNK), np.float32), k=-1), dtype=bf)

    r_all, qit, k1, k2, vt, iwt = _inproj(x, meta_pad, norm1_w[0][None].astype(f32),
                                          wr, wqi, wk, wv, ww, qscale, kscale)
    y_ret = _retention(r_all, nb, cos2, sin2, ret_norm_w[0][None].astype(f32))
    y_dsa = _sparse_attention(qit, iwt, k1, k2, vt, bias_tbl, tri, topk, nb)

    wo = w_out[0].astype(bf)
    out = _out_mlp(x.reshape(B * L, D), y_ret.reshape(B * L, rw), y_dsa.reshape(B * L, dw),
                   wo[:rw], wo[rw:], norm2_w[0][None].astype(f32),
                   w_ff1[0].astype(bf), w_ff2[0].astype(bf))
    return out.reshape(B, L, D)
```

```python
import functools
import math

import numpy as np
import jax
import jax.numpy as jnp
from jax import lax
from jax.experimental import pallas as pl
from jax.experimental.pallas import tpu as pltpu

N_META = 16
CHUNK = 128
RET_HEADS = 4
RET_DK = 128
DSA_HEADS = 8
DSA_DH = 64
IDX_HEADS = 8
TOPK_MAX = 256
N_BUCKETS = 32
MAX_DISTANCE = 128
ROPE_BASE = 10000.0
EPS = 1e-6
NEG = -1e30
PAD = CHUNK - N_META
LANES = 128
SUBLANES = 8
BF16_ROWS = 16
HALF = LANES // 2
INT_MIN = -(2 ** 31)
LOG2E = math.log2(math.e)

FFN_CHUNK = 1024
ROW_TILE = 512
VMEM_LIMIT = 56 * 1024 * 1024


def _dot(a, b):
    return jnp.dot(a, b, preferred_element_type=jnp.float32)


def _dot_nt(a, b):
    return lax.dot_general(a, b, (((1,), (1,)), ((), ())), preferred_element_type=jnp.float32)


def _dot_tn(a, b):
    return lax.dot_general(a, b, (((0,), (0,)), ((), ())), preferred_element_type=jnp.float32)


def _bucket_ranges():
    max_exact = N_BUCKETS // 2
    d = np.arange(0, 2 * CHUNK)
    large = max_exact + (np.log(np.maximum(d, 1) / max_exact) / math.log(MAX_DISTANCE / max_exact)
                         * (N_BUCKETS - max_exact)).astype(np.int64)
    bucket = np.where(d < max_exact, d, np.minimum(large, N_BUCKETS - 1))
    out = []
    for b in range(N_BUCKETS - 1):
        idx = np.nonzero(bucket == b)[0]
        out.append((int(idx.min()), int(idx.max())))
    return out


def _rope_kernel(cos_ref, sin_ref):
    n = pl.program_id(0)
    row = lax.broadcasted_iota(jnp.int32, (CHUNK, LANES), 0)
    lane = lax.broadcasted_iota(jnp.int32, (CHUNK, LANES), 1)
    pos = (n * CHUNK + row - PAD).astype(jnp.float32)
    frac = (lane % HALF).astype(jnp.float32) / HALF
    inv = jnp.exp(-frac * math.log(ROPE_BASE))
    ang = pos * inv
    cos_ref[...] = jnp.cos(ang)
    s = jnp.sin(ang)
    sin_ref[...] = jnp.where(lane < HALF, -s, s)


def _rope_tables(nb):
    return pl.pallas_call(
        _rope_kernel,
        grid=(nb,),
        out_specs=[pl.BlockSpec((CHUNK, LANES), lambda n: (n, 0))] * 2,
        out_shape=[jax.ShapeDtypeStruct((nb * CHUNK, LANES), jnp.float32)] * 2,
        name="rope_tables",
    )()


def _bias_kernel(rb_ref, out_ref):
    row = lax.broadcasted_iota(jnp.int32, (CHUNK, CHUNK), 0)
    col = lax.broadcasted_iota(jnp.int32, (CHUNK, CHUNK), 1)
    ranges = _bucket_ranges()
    out_ref[2] = jnp.zeros(out_ref.shape[1:], jnp.float32)
    for h in range(DSA_HEADS):
        far = rb_ref[N_BUCKETS - 1, h]
        for t in range(2):
            dist = col - row + t * CHUNK
            tile = jnp.zeros((CHUNK, CHUNK), jnp.float32)
            for b, (lo, hi) in enumerate(ranges):
                tile = jnp.where((dist >= lo) & (dist <= hi), (rb_ref[b, h] - far) * LOG2E, tile)
            out_ref[t, :, h * LANES:(h + 1) * LANES] = tile


def _bias_tables(rel_bias):
    return pl.pallas_call(
        _bias_kernel,
        in_specs=[pl.BlockSpec(memory_space=pltpu.SMEM)],
        out_specs=pl.BlockSpec(memory_space=pltpu.VMEM),
        out_shape=jax.ShapeDtypeStruct((3, CHUNK, DSA_HEADS * LANES), jnp.float32),
        name="bias_tables",
    )(rel_bias.astype(jnp.float32))


def _inproj_kernel(xa_ref, xb_ref, meta_ref, n1_ref, wr_ref, wqi_ref, wk_ref, wv_ref, ww_ref,
                   qs_ref, ks_ref,
                   r_ref, qit_ref, k1_ref, k2_ref, vt_ref, iwt_ref, *, pad_last):
    s = pl.program_id(1)
    top = jnp.where(s == 0, meta_ref[...], xa_ref[0])
    bot = xb_ref[0]
    if pad_last:
        bot = jnp.where(s == pl.num_programs(1) - 1, 0.0, bot)
    src = jnp.concatenate([top, bot], axis=0)
    ms = jnp.mean(src * src, axis=-1, keepdims=True)
    u = (src * lax.rsqrt(ms + EPS) * n1_ref[...]).astype(jnp.bfloat16)

    r_ref[0] = _dot(u, wr_ref[...])

    lane = lax.broadcasted_iota(jnp.int32, (2 * CHUNK, LANES), 1)
    lo = lane < HALF

    def head_rms(t):
        ssq = jnp.sum(jnp.where(lo, t * t, 0.0), axis=-1, keepdims=True)
        return lax.rsqrt(ssq / DSA_DH + EPS)

    pq = _dot(u, wqi_ref[...])
    for h in range(DSA_HEADS):
        t = pq[:, h * LANES:(h + 1) * LANES]
        t = t * (jnp.where(lo, head_rms(t), 1.0) * qs_ref[...])
        for j in range(2):
            qit_ref[0, j, :, h * LANES:(h + 1) * LANES] = (
                t[j * CHUNK:(j + 1) * CHUNK].T.astype(jnp.bfloat16))

    pk = _dot(u, wk_ref[...])
    kn = pk * head_rms(pk) * ks_ref[...]
    k1_ref[0] = jnp.where(lo, kn, 0.0).astype(jnp.bfloat16)
    k2_ref[0] = jnp.where(lo, 0.0, pk).astype(jnp.bfloat16)

    pv = _dot(u, wv_ref[...])
    vx = jnp.where(lo, pv, jnp.where(lane == HALF, 1.0, 0.0))
    iw = _dot(u, ww_ref[...]) * (IDX_HEADS ** -0.5)
    for j in range(2):
        rows = slice(j * CHUNK, (j + 1) * CHUNK)
        vt_ref[0, :, rows] = vx[rows].T.astype(jnp.bfloat16)
        iwt_ref[0, j] = iw[rows].T[0:IDX_HEADS, :]


def _inproj(x, meta_pad, n1, wr, wqi, wk, wv, ww, qscale, kscale):
    B, L, D = x.shape
    nx = L // CHUNK
    nbp = nx + 1 + ((nx + 1) % 2)
    tp = nbp * CHUNK
    const = lambda shape: pl.BlockSpec(shape, lambda b, s: (0,) * len(shape))
    blk = lambda w: pl.BlockSpec((1, 2 * CHUNK, w), lambda b, s: (b, s, 0))
    return pl.pallas_call(
        functools.partial(_inproj_kernel, pad_last=bool((nx + 1) % 2)),
        grid=(B, nbp // 2),
        in_specs=[
            pl.BlockSpec((1, CHUNK, D), lambda b, s: (b, jnp.maximum(2 * s - 1, 0), 0)),
            pl.BlockSpec((1, CHUNK, D), lambda b, s: (b, jnp.minimum(2 * s, nx - 1), 0)),
            const(meta_pad.shape), const(n1.shape), const(wr.shape), const(wqi.shape),
            const(wk.shape), const(wv.shape), const(ww.shape), const(qscale.shape),
            const(kscale.shape),
        ],
        out_specs=[
            blk(wr.shape[1]),
            pl.BlockSpec((1, 2, LANES, DSA_HEADS * LANES), lambda b, s: (b, s, 0, 0)),
            blk(LANES), blk(LANES),
            pl.BlockSpec((1, LANES, 2 * CHUNK), lambda b, s: (b, 0, s)),
            pl.BlockSpec((1, 2, IDX_HEADS, CHUNK), lambda b, s: (b, s, 0, 0)),
        ],
        out_shape=[
            jax.ShapeDtypeStruct((B, tp, wr.shape[1]), jnp.float32),
            jax.ShapeDtypeStruct((B, nbp, LANES, DSA_HEADS * LANES), jnp.bfloat16),
            jax.ShapeDtypeStruct((B, tp, LANES), jnp.bfloat16),
            jax.ShapeDtypeStruct((B, tp, LANES), jnp.bfloat16),
            jax.ShapeDtypeStruct((B, LANES, tp), jnp.bfloat16),
            jax.ShapeDtypeStruct((B, nbp, IDX_HEADS, CHUNK), jnp.float32),
        ],
        compiler_params=pltpu.CompilerParams(
            dimension_semantics=("arbitrary", "arbitrary"), vmem_limit_bytes=VMEM_LIMIT),
        name="in_projection",
    )(x, x, meta_pad, n1, wr, wqi, wk, wv, ww, qscale, kscale)


def _retention_kernel(ra_ref, rb_ref, cosa_ref, sina_ref, cosb_ref, sinb_ref, gain_ref, y_ref,
                      state_ref, tbl_ref):
    s = pl.program_id(1)
    w = RET_HEADS * RET_DK
    bf = jnp.bfloat16

    @pl.when(s == 0)
    def _():
        state_ref[...] = jnp.zeros_like(state_ref)
        row = lax.broadcasted_iota(jnp.int32, (CHUNK, CHUNK), 0)
        col = lax.broadcasted_iota(jnp.int32, (CHUNK, CHUNK), 1)
        rowf = row.astype(jnp.float32)
        diff = (row - col).astype(jnp.float32)
        for h in range(RET_HEADS):
            log_gamma = math.log(1.0 - 2.0 ** (-5.0 - h))
            tbl_ref[0, h] = jnp.where(diff >= 0, jnp.exp(log_gamma * jnp.maximum(diff, 0.0)), 0.0)
            tbl_ref[1, h] = jnp.exp(log_gamma * (rowf + 1.0))
            tbl_ref[2, h] = jnp.exp(log_gamma * (CHUNK - 1.0 - rowf))

    live = jnp.where(s == 0, 0.0, 1.0)
    chunks = ((ra_ref, cosa_ref, sina_ref, live), (rb_ref, cosb_ref, sinb_ref, None))
    for h in range(RET_HEADS):
        log_gamma = math.log(1.0 - 2.0 ** (-5.0 - h))
        sl = slice(h * RET_DK, (h + 1) * RET_DK)
        state = state_ref[h]
        for j, (r_ref, cos_ref, sin_ref, scale) in enumerate(chunks):
            cos2, sin2 = cos_ref[...], sin_ref[...]
            q = r_ref[0, :, sl]
            k = r_ref[0, :, w + h * RET_DK: w + (h + 1) * RET_DK]
            v = r_ref[0, :, 2 * w + h * RET_DK: 2 * w + (h + 1) * RET_DK]
            g = r_ref[0, :, 3 * w + h * RET_DK: 3 * w + (h + 1) * RET_DK]
            if scale is not None:
                k = k * scale
            v = v.astype(bf)
            qr = q * cos2 + pltpu.roll(q, HALF, 1) * sin2
            kr = (k * cos2 + pltpu.roll(k, HALF, 1) * sin2) * (RET_DK ** -0.5)

            scores = _dot_nt(qr.astype(bf), kr.astype(bf)) * tbl_ref[0, h]
            o = _dot(scores.astype(bf), v)
            o = o + _dot((qr * tbl_ref[1, h]).astype(bf), state.astype(bf))
            kv = _dot_tn((kr * tbl_ref[2, h]).astype(bf), v)
            state = state * math.exp(log_gamma * CHUNK) + kv

            ms = jnp.mean(o * o, axis=-1, keepdims=True)
            on = o * lax.rsqrt(ms + EPS) * gain_ref[:, sl]
            gate = g * (1.0 / (1.0 + jnp.exp(-g)))
            y_ref[0, j * CHUNK:(j + 1) * CHUNK, sl] = (gate * on).astype(y_ref.dtype)
        state_ref[h] = state


def _retention(r_all, nb, cos2, sin2, gain):
    B = r_all.shape[0]
    w = RET_HEADS * RET_DK
    nstep = (nb + 1) // 2
    rspec = lambda f: pl.BlockSpec((1, CHUNK, 4 * w), lambda b, s: (b, f(s), 0))
    tspec = lambda f: pl.BlockSpec((CHUNK, LANES), lambda b, s: (f(s), 0))
    first = lambda s: jnp.maximum(2 * s - 1, 0)
    second = lambda s: 2 * s
    return pl.pallas_call(
        _retention_kernel,
        grid=(B, nstep),
        in_specs=[rspec(first), rspec(second), tspec(first), tspec(first), tspec(second),
                  tspec(second), pl.BlockSpec((1, w), lambda b, s: (0, 0))],
        out_specs=pl.BlockSpec((1, 2 * CHUNK, w), lambda b, s: (b, jnp.maximum(s - 1, 0), 0)),
        out_shape=jax.ShapeDtypeStruct((B, (nb - 1) * CHUNK, w), jnp.bfloat16),
        scratch_shapes=[pltpu.VMEM((RET_HEADS, RET_DK, RET_DK), jnp.float32),
                        pltpu.VMEM((3, RET_HEADS, CHUNK, CHUNK), jnp.float32)],
        compiler_params=pltpu.CompilerParams(
            dimension_semantics=("arbitrary", "arbitrary"), vmem_limit_bytes=VMEM_LIMIT),
        name="retention",
    )(r_all, r_all, cos2, sin2, cos2, sin2, gain)


def _bit_planes(tile):
    a = [tile[r * 8:(r + 1) * 8, :] for r in range(32)]
    j, m = 16, 0x0000FFFF
    while j:
        for k in range(32):
            if (k & j) == 0:
                t = (a[k] ^ lax.shift_right_logical(a[k + j], j)) & m
                a[k] = a[k] ^ t
                a[k + j] = a[k + j] ^ (t << j)
        j >>= 1
        if j:
            m = m ^ ((m << j) & 0xFFFFFFFF)
    return a


def _sublane_total(x):
    x = x + pltpu.roll(x, 4, 0)
    x = x + pltpu.roll(x, 2, 0)
    return x + pltpu.roll(x, 1, 0)


def _dsa_kernel(qa_ref, qb_ref, iwt_ref, k1_ref, k2_ref, vt_ref, bias_ref, tri_ref, y_ref,
                key_ref, plane_ref, thr_ref, ntie_ref, flag_ref, m_ref, acc_ref, sa_ref, sb_ref,
                rhs_ref, *, topk):
    t = pl.program_id(1)
    last_q = pl.num_programs(1) - 1
    bf = jnp.bfloat16
    H = DSA_HEADS
    KP = 2 * CHUNK
    G = plane_ref.shape[1]
    cur, nxt = t % 2, (t + 1) % 2

    @pl.when((pl.program_id(0) == 0) & (t == 0))
    def _():
        plane_ref[...] = jnp.zeros_like(plane_ref)
        key_ref[...] = jnp.zeros_like(key_ref)
        eye = (lax.broadcasted_iota(jnp.int32, (CHUNK, CHUNK), 0)
               == lax.broadcasted_iota(jnp.int32, (CHUNK, CHUNK), 1))
        for h in range(DSA_HEADS):
            rhs_ref[CHUNK:2 * CHUNK, h * LANES:(h + 1) * LANES] = jnp.where(eye, 1.0, 0.0).astype(bf)

    @pl.when(t == 0)
    def _():
        thr_ref[0] = jnp.zeros(thr_ref.shape[1:], jnp.float32)
        ntie_ref[0] = jnp.zeros(ntie_ref.shape[1:], jnp.float32)
        flag_ref[0] = 0

    row = lax.broadcasted_iota(jnp.int32, (KP, CHUNK), 0)
    col = lax.broadcasted_iota(jnp.int32, (KP, CHUNK), 1)
    hs = lambda h: slice(h * LANES, (h + 1) * LANES)
    hp = lambda c: slice(2 * c * LANES, 2 * (c + 1) * LANES)
    nsteps = lambda n: (n + 2) // 2

    def key_rows(g):
        last = k1_ref.shape[1] // KP - 1
        return pl.ds(pl.multiple_of(jnp.minimum(g, last) * KP, KP), KP)

    blk_b = t + 1
    steps_b = jnp.where(t < last_q, nsteps(blk_b), 0)
    iters_b, tail_b = steps_b // 2, steps_b % 2
    qb_idx = blk_b * CHUNK + col
    iw_rows = [jnp.broadcast_to(iwt_ref[0, 0, h:h + 1, :], (KP, CHUNK)) for h in range(IDX_HEADS)]

    def score_pair(g):
        kblk = k2_ref[0, key_rows(g), :]
        sc = None
        for c in range(H // 2):
            s = _dot(kblk, qb_ref[0, 0, :, hp(c)])
            for hh in range(2):
                term = iw_rows[2 * c + hh] * jnp.maximum(s[:, hs(hh)], 0.0)
                sc = term if sc is None else sc + term
        k_idx = g * KP + row
        valid = (k_idx <= qb_idx) & (k_idx >= PAD)
        sc = jnp.where(valid, sc, NEG)
        key_ref[nxt, g] = sc
        bits = lax.bitcast_convert_type(sc, jnp.int32)
        key = jnp.where(bits < 0, jnp.int32(INT_MIN) - bits, bits)
        planes = _bit_planes(key ^ jnp.int32(INT_MIN))
        for p in range(32):
            plane_ref[p, g] = planes[p]

    def score_two(i):
        score_pair(2 * i)
        score_pair(2 * i + 1)

    blk_a = t
    steps_a = jnp.where(t > 0, nsteps(blk_a), 0)
    iters_a, tail_a = steps_a // 2, steps_a % 2
    far_a = jnp.where(t > 0, ((blk_a - 1) // 2) // 2, 0)
    rep = lambda x: jnp.tile(x, (KP // SUBLANES, 1))
    thr = thr_ref[cur]
    few = thr <= 0.999 * NEG
    thr_b = rep(jnp.where(few, 0.5 * NEG, thr))
    thr_sel_b = thr_b
    n_tie_b = rep(jnp.where(few, 0.0, ntie_ref[cur]))
    has_ties = flag_ref[cur] > 0
    va = acc_ref.shape[0]

    @pl.when(t > 0)
    def _():
        m_ref[...] = jnp.full_like(m_ref, NEG)
        acc_ref[...] = jnp.zeros_like(acc_ref)
        rhs_ref[0:CHUNK, :] = qa_ref[0, 0]

    def logits(g, dst_ref, ties):
        kblk = k1_ref[0, key_rows(g), :]
        if ties:
            dst_ref[...] = _dot(kblk, qa_ref[0, 0])
        else:
            off = jnp.where(key_ref[cur, g] >= thr_sel_b, 0.0, NEG).astype(bf)
            dst_ref[...] = _dot(jnp.concatenate([kblk, off], axis=1), rhs_ref[...])

    def attend(g, s_ref, ties_before, near, ties):
        if ties:
            key = key_ref[cur, g]
            eq = key == thr_b
            eqf = jnp.where(eq, 1.0, 0.0)
            before = ties_before + _dot(tri_ref[...], eqf.astype(bf))
            sel = (key > thr_b) | (eq & (before < n_tie_b))
            ties_before = ties_before + jnp.sum(eqf, axis=0, keepdims=True)
        ps, alphas = [], []
        for h in range(H):
            s = s_ref[:, hs(h)]
            if near:
                ta = jnp.clip(blk_a - 2 * g, 0, 2)
                tb = jnp.clip(blk_a - 2 * g - 1, 0, 2)
                s = s + jnp.concatenate([bias_ref[ta, :, hs(h)], bias_ref[tb, :, hs(h)]], axis=0)
            if ties:
                s = jnp.where(sel, s, NEG)
            m_prev = m_ref[h:h + 1, :]
            m_new = jnp.maximum(m_prev, jnp.max(s, axis=0, keepdims=True))
            m_ref[h:h + 1, :] = m_new
            alphas.append(jnp.exp2(m_prev - m_new))
            ps.append(jnp.exp2(s - m_new).astype(bf))
        pv = _dot(vt_ref[0, 0:va, key_rows(g)], jnp.concatenate(ps, axis=1))
        acc_ref[...] = acc_ref[...] * jnp.concatenate(alphas, axis=1) + pv
        return ties_before

    def attend_two(i, ties_before, near, ties, fused):
        logits(2 * i + 1, sb_ref, ties)
        ties_before = attend(2 * i, sa_ref, ties_before, near, ties)
        logits(2 * i + 2, sa_ref, ties)
        ties_before = attend(2 * i + 1, sb_ref, ties_before, near, ties)
        if fused:
            score_two(i)
        return ties_before

    zero_ties = jnp.zeros((KP, CHUNK), jnp.float32)
    loop = lambda lo, hi, **kw: (lambda c: lax.fori_loop(lo, hi, functools.partial(attend_two, **kw), c))
    fused_hi = jnp.minimum(iters_a, iters_b)

    @pl.when(has_ties)
    def _():
        logits(0, sa_ref, True)
        c = loop(0, far_a, near=False, ties=True, fused=False)(zero_ties)
        c = loop(far_a, iters_a, near=True, ties=True, fused=False)(c)

        @pl.when(tail_a == 1)
        def _():
            attend(2 * iters_a, sa_ref, c, True, True)

    @pl.when(jnp.logical_not(has_ties))
    def _():
        logits(0, sa_ref, False)
        far_hi = jnp.minimum(far_a, fused_hi)
        loop(0, far_hi, near=False, ties=False, fused=True)(zero_ties)
        loop(far_hi, fused_hi, near=True, ties=False, fused=True)(zero_ties)
        loop(fused_hi, iters_a, near=True, ties=False, fused=False)(zero_ties)

        @pl.when(tail_a == 1)
        def _():
            attend(2 * iters_a, sa_ref, zero_ties, True, False)

    lax.fori_loop(jnp.where(has_ties, 0, fused_hi), iters_b, lambda i, _: (score_two(i), 0)[1], 0)

    @pl.when(tail_b == 1)
    def _():
        score_pair(2 * iters_b)

    @pl.when(t > 0)
    def _():
        lo = lax.broadcasted_iota(jnp.int32, (CHUNK, CHUNK), 1) < HALF
        for c in range(H // 2):
            tiles = []
            for h in (2 * c, 2 * c + 1):
                a = acc_ref[:, hs(h)] * (1.0 / acc_ref[HALF:HALF + 1, hs(h)])
                a = jnp.concatenate([a, jnp.zeros((CHUNK - va, CHUNK), jnp.float32)], axis=0)
                tiles.append(a.T)
            y_ref[0, :, hs(c)] = jnp.where(lo, tiles[0], pltpu.roll(tiles[1], HALF, 1)).astype(y_ref.dtype)

    @pl.when(t < last_q)
    def _():
        npair_b = (blk_b + 2) // 2
        g_idx = lax.broadcasted_iota(jnp.int32, (G, 8, CHUNK), 0)
        alive0 = jnp.where(g_idx < npair_b, jnp.int32(-1), jnp.int32(0))

        def ones_in(words):
            return _sublane_total(lax.population_count(words).sum(axis=0))

        def radix(i, carry):
            alive, above, thr_u = carry
            hi_plane, lo_plane = plane_ref[2 * i], plane_ref[2 * i + 1]
            a1 = alive & hi_plane
            a0 = alive ^ a1
            a11, a01 = a1 & lo_plane, a0 & lo_plane
            c1, c11, c01 = ones_in(a1), ones_in(a11), ones_in(a01)
            hi = (above + c1) >= topk
            above = jnp.where(hi, above, above + c1)
            c_lo = jnp.where(hi, c11, c01)
            lo = (above + c_lo) >= topk
            above = jnp.where(lo, above, above + c_lo)
            bit = lax.shift_right_logical(jnp.int32(INT_MIN), 2 * i)
            thr_u = (thr_u | jnp.where(hi, bit, 0)
                     | jnp.where(lo, lax.shift_right_logical(bit, 1), 0))
            group = jnp.where(hi[None], a1, a0)
            with_lo = jnp.where(hi[None], a11, a01)
            alive = jnp.where(lo[None], with_lo, group ^ with_lo)
            return alive, above, thr_u

        zero8 = jnp.zeros((8, CHUNK), jnp.int32)
        thr_u = lax.fori_loop(0, 16, radix, (alive0, zero8, zero8))[2]
        key_k = thr_u ^ jnp.int32(INT_MIN)
        as_score = lambda k: lax.bitcast_convert_type(
            jnp.where(k < 0, jnp.int32(INT_MIN) - k, k), jnp.float32)
        k_f = jnp.float32(topk)

        def count_ge(*ps):
            ps_b = [rep(p) for p in ps]

            def body(g, parts):
                sc = key_ref[nxt, g]
                return tuple(
                    part + jnp.where(sc >= p_b, 1.0, 0.0).reshape(KP // SUBLANES, SUBLANES, CHUNK).sum(axis=0)
                    for part, p_b in zip(parts, ps_b))

            zero = jnp.zeros((SUBLANES, CHUNK), jnp.float32)
            pairs = steps_b // 2
            parts = lax.fori_loop(0, pairs, lambda i, c: body(2 * i + 1, body(2 * i, c)), (zero,) * len(ps))
            parts = lax.fori_loop(2 * pairs, steps_b, body, parts)
            return [_sublane_total(c) for c in parts]

        def pending(lo, hi, c_lo, c_hi):
            mid = 0.5 * (lo + hi)
            final = (c_lo >= k_f) & (c_hi < k_f) & ((c_lo == k_f) | (mid <= lo) | (mid >= hi))
            return jnp.logical_not(final)

        def status(lo, hi, c_lo, c_hi):
            tied = (c_lo > k_f) & (lo > 0.999 * NEG)
            return jnp.max(jnp.where(pending(lo, hi, c_lo, c_hi), 2.0, jnp.where(tied, 1.0, 0.0)))

        def refine(state):
            lo, hi, c_lo, c_hi, step, it, _ = state
            low_bad, high_bad = c_lo < k_f, c_hi >= k_f
            p = jnp.where(low_bad, lo - step, jnp.where(high_bad, hi + step, 0.5 * (lo + hi)))
            c_p, = count_ge(p)
            todo = pending(lo, hi, c_lo, c_hi)
            down = todo & low_bad
            up = todo & jnp.logical_not(low_bad) & high_bad
            split = todo & jnp.logical_not(low_bad | high_bad)
            as_lo = down | (split & (c_p >= k_f))
            as_hi = up | (split & (c_p < k_f))
            lo, c_lo, hi, c_hi = (
                jnp.where(as_lo, p, jnp.where(up, hi, lo)),
                jnp.where(as_lo, c_p, jnp.where(up, c_hi, c_lo)),
                jnp.where(as_hi, p, jnp.where(down, lo, hi)),
                jnp.where(as_hi, c_p, jnp.where(down, c_lo, c_hi)))
            return (lo, hi, c_lo, c_hi, jnp.where(down | up, 2.0 * step, step), it + 1,
                    status(lo, hi, c_lo, c_hi))

        lo0, hi0 = as_score(key_k), as_score(key_k + 1)
        c_lo0, c_hi0 = count_ge(lo0, hi0)
        state = (lo0, hi0, c_lo0, c_hi0, jnp.maximum(jnp.abs(lo0), 1e-30) * 2.0 ** -20,
                 jnp.int32(0), status(lo0, hi0, c_lo0, c_hi0))
        lo, hi, c_lo, c_hi, _, _, left = lax.while_loop(
            lambda st: (st[6] > 1.5) & (st[5] < 200), refine, state)
        masked_n = lo <= 0.999 * NEG
        exact_n = c_lo == k_f
        thr_ref[nxt] = lo
        ntie_ref[nxt] = jnp.where(masked_n, 0.0, jnp.where(exact_n, jnp.float32(2 * KP * G), k_f - c_hi))
        flag_ref[nxt] = (left > 0.5).astype(jnp.int32)


def _sparse_attention(qit, iwt, k1, k2, vt, bias_tbl, tri, topk, nb):
    B, nbp, _, hw = qit.shape
    tp = nbp * CHUNK
    nstep = nbp // 2
    nstep += nstep % 2
    kspec = pl.BlockSpec((1, tp, LANES), lambda b, t: (b, 0, 0))
    acc_rows = DSA_DH + BF16_ROWS
    nxt = lambda t: jnp.minimum(t + 1, nb - 1)
    return pl.pallas_call(
        functools.partial(_dsa_kernel, topk=topk),
        grid=(B, nb),
        in_specs=[
            pl.BlockSpec((1, 1, LANES, hw), lambda b, t: (b, t, 0, 0)),
            pl.BlockSpec((1, 1, LANES, hw), lambda b, t: (b, nxt(t), 0, 0)),
            pl.BlockSpec((1, 1, IDX_HEADS, CHUNK), lambda b, t: (b, nxt(t), 0, 0)),
            kspec, kspec,
            pl.BlockSpec((1, LANES, tp), lambda b, t: (b, 0, 0)),
            pl.BlockSpec(bias_tbl.shape, lambda b, t: (0, 0, 0)),
            pl.BlockSpec(tri.shape, lambda b, t: (0, 0)),
        ],
        out_specs=pl.BlockSpec((1, CHUNK, DSA_HEADS * DSA_DH),
                               lambda b, t: (b, jnp.maximum(t - 1, 0), 0)),
        out_shape=jax.ShapeDtypeStruct((B, (nb - 1) * CHUNK, DSA_HEADS * DSA_DH), jnp.bfloat16),
        scratch_shapes=[
            pltpu.VMEM((2, nstep, 2 * CHUNK, CHUNK), jnp.float32),
            pltpu.VMEM((32, nstep, 8, CHUNK), jnp.int32),
            pltpu.VMEM((2, 8, CHUNK), jnp.float32),
            pltpu.VMEM((2, 8, CHUNK), jnp.float32),
            pltpu.SMEM((2,), jnp.int32),
            pltpu.VMEM((DSA_HEADS, CHUNK), jnp.float32),
            pltpu.VMEM((acc_rows, hw), jnp.float32),
            pltpu.VMEM((2 * CHUNK, hw), jnp.float32),
            pltpu.VMEM((2 * CHUNK, hw), jnp.float32),
            pltpu.VMEM((2 * CHUNK, hw), jnp.bfloat16),
        ],
        compiler_params=pltpu.CompilerParams(
            dimension_semantics=("arbitrary", "arbitrary"), vmem_limit_bytes=VMEM_LIMIT),
        name="sparse_attention",
    )(qit, qit, iwt, k1, k2, vt, bias_tbl, tri)


def _mlp_kernel(x_ref, yr_ref, yd_ref, wor_ref, wod_ref, n2_ref, w1_ref, w2_ref, o_ref):
    h1 = x_ref[...] + _dot(yr_ref[...], wor_ref[...]) + _dot(yd_ref[...], wod_ref[...])
    ms = jnp.mean(h1 * h1, axis=-1, keepdims=True)
    u = (h1 * lax.rsqrt(ms + EPS) * n2_ref[...]).astype(jnp.bfloat16)
    o_ref[...] = h1
    d_ff = w1_ref.shape[1]
    for c in range(d_ff // FFN_CHUNK):
        sl = slice(c * FFN_CHUNK, (c + 1) * FFN_CHUNK)
        f = jnp.maximum(_dot(u, w1_ref[:, sl]), 0.0)
        o_ref[...] += _dot((f * f).astype(jnp.bfloat16), w2_ref[sl, :])


def _out_mlp(x2, yr, yd, wor, wod, n2, w1, w2):
    rows, D = x2.shape
    const = lambda a: pl.BlockSpec(a.shape, lambda i: (0, 0), pipeline_mode=pl.Buffered(1))
    tile = lambda w: pl.BlockSpec((ROW_TILE, w), lambda i: (i, 0))
    return pl.pallas_call(
        _mlp_kernel,
        grid=(rows // ROW_TILE,),
        in_specs=[tile(D), tile(yr.shape[1]), tile(yd.shape[1]),
                  const(wor), const(wod), const(n2), const(w1), const(w2)],
        out_specs=tile(D),
        out_shape=jax.ShapeDtypeStruct((rows, D), jnp.float32),
        compiler_params=pltpu.CompilerParams(
            dimension_semantics=("arbitrary",), vmem_limit_bytes=VMEM_LIMIT),
        name="out_mlp",
    )(x2, yr, yd, wor, wod, n2, w1, w2)


def kernel(x, meta_tokens, norm1_w, w_in, ret_norm_w, q_norm_w, k_norm_w, rel_bias,
           w_out, norm2_w, w_ff1, w_ff2):
    B, L, D = x.shape
    assert L % (2 * CHUNK) == 0 and L % ROW_TILE == 0 and w_in.shape[0] == 1
    topk = min(TOPK_MAX, L // 4)
    nb = L // CHUNK + 1
    bf = jnp.bfloat16
    f32 = jnp.float32

    rw = RET_HEADS * RET_DK
    dw = DSA_HEADS * DSA_DH
    sizes = (rw, rw, rw, rw, dw, DSA_DH, DSA_DH, IDX_HEADS * DSA_DH, DSA_DH, IDX_HEADS)
    offs = np.concatenate([[0], np.cumsum(sizes)])
    col = lambda i: w_in[0][:, int(offs[i]):int(offs[i + 1])]
    wr = jnp.concatenate([col(0), col(1), col(2), col(3)], axis=1).astype(bf)
    wq = col(4).reshape(D, DSA_HEADS, DSA_DH)
    wiq = col(7).reshape(D, IDX_HEADS, DSA_DH)
    wqi = jnp.concatenate([wq, wiq], axis=2).reshape(D, DSA_HEADS * LANES).astype(bf)
    wk = jnp.concatenate([col(5), col(8)], axis=1).astype(bf)
    wv = jnp.concatenate([col(6), jnp.zeros((D, LANES - DSA_DH), f32)], axis=1).astype(bf)
    ww = jnp.concatenate([col(9), jnp.zeros((D, LANES - IDX_HEADS), f32)], axis=1).astype(bf)

    idx_scale = jnp.full((DSA_DH,), DSA_DH ** -0.5, f32)
    qscale = jnp.concatenate([q_norm_w[0].astype(f32) * (DSA_DH ** -0.5 * LOG2E), idx_scale])[None]
    kscale = jnp.concatenate([k_norm_w[0].astype(f32), jnp.ones((DSA_DH,), f32)])[None]
    meta_pad = jnp.concatenate([jnp.zeros((PAD, D), x.dtype), meta_tokens.astype(x.dtype)], axis=0)

    cos2, sin2 = _rope_tables(nb)
    bias_tbl = _bias_tables(rel_bias)
    tri = jnp.asarray(np.tril(np.ones((2 * CHUNK, 2 * CHUNK), np.float32), k=-1), dtype=bf)

    r_all, qit, k1, k2, vt, iwt = _inproj(x, meta_pad, norm1_w[0][None].astype(f32),
                                          wr, wqi, wk, wv, ww, qscale, kscale)
    y_ret = _retention(r_all, nb, cos2, sin2, ret_norm_w[0][None].astype(f32))
    y_dsa = _sparse_attention(qit, iwt, k1, k2, vt, bias_tbl, tri, topk, nb)

    wo = w_out[0].astype(bf)
    out = _out_mlp(x.reshape(B * L, D), y_ret.reshape(B * L, rw), y_dsa.reshape(B * L, dw),
                   wo[:rw], wo[rw:], norm2_w[0][None].astype(f32),
                   w_ff1[0].astype(bf), w_ff2[0].astype(bf))
    return out.reshape(B, L, D)
```

```python
import functools
import math

import numpy as np
import jax
import jax.numpy as jnp
from jax import lax
from jax.experimental import pallas as pl
from jax.experimental.pallas import tpu as pltpu

N_META = 16
CHUNK = 128
RET_HEADS = 4
RET_DK = 128
DSA_HEADS = 8
DSA_DH = 64
IDX_HEADS = 8
TOPK_MAX = 256
N_BUCKETS = 32
MAX_DISTANCE = 128
ROPE_BASE = 10000.0
EPS = 1e-6
NEG = -1e30
PAD = CHUNK - N_META
LANES = 128
SUBLANES = 8
BF16_ROWS = 16
HALF = LANES // 2
INT_MIN = -(2 ** 31)
LOG2E = math.log2(math.e)

FFN_CHUNK = 1024
ROW_TILE = 512
VMEM_LIMIT = 56 * 1024 * 1024


def _dot(a, b):
    return jnp.dot(a, b, preferred_element_type=jnp.float32)


def _dot_nt(a, b):
    return lax.dot_general(a, b, (((1,), (1,)), ((), ())), preferred_element_type=jnp.float32)


def _dot_tn(a, b):
    return lax.dot_general(a, b, (((0,), (0,)), ((), ())), preferred_element_type=jnp.float32)


def _bucket_ranges():
    max_exact = N_BUCKETS // 2
    d = np.arange(0, 2 * CHUNK)
    large = max_exact + (np.log(np.maximum(d, 1) / max_exact) / math.log(MAX_DISTANCE / max_exact)
                         * (N_BUCKETS - max_exact)).astype(np.int64)
    bucket = np.where(d < max_exact, d, np.minimum(large, N_BUCKETS - 1))
    out = []
    for b in range(N_BUCKETS - 1):
        idx = np.nonzero(bucket == b)[0]
        out.append((int(idx.min()), int(idx.max())))
    return out


def _rope_kernel(cos_ref, sin_ref):
    n = pl.program_id(0)
    row = lax.broadcasted_iota(jnp.int32, (CHUNK, LANES), 0)
    lane = lax.broadcasted_iota(jnp.int32, (CHUNK, LANES), 1)
    pos = (n * CHUNK + row - PAD).astype(jnp.float32)
    frac = (lane % HALF).astype(jnp.float32) / HALF
    inv = jnp.exp(-frac * math.log(ROPE_BASE))
    ang = pos * inv
    cos_ref[...] = jnp.cos(ang)
    s = jnp.sin(ang)
    sin_ref[...] = jnp.where(lane < HALF, -s, s)


def _rope_tables(nb):
    return pl.pallas_call(
        _rope_kernel,
        grid=(nb,),
        out_specs=[pl.BlockSpec((CHUNK, LANES), lambda n: (n, 0))] * 2,
        out_shape=[jax.ShapeDtypeStruct((nb * CHUNK, LANES), jnp.float32)] * 2,
        name="rope_tables",
    )()


def _bias_kernel(rb_ref, out_ref):
    row = lax.broadcasted_iota(jnp.int32, (CHUNK, CHUNK), 0)
    col = lax.broadcasted_iota(jnp.int32, (CHUNK, CHUNK), 1)
    ranges = _bucket_ranges()
    out_ref[2] = jnp.zeros(out_ref.shape[1:], jnp.float32)
    for h in range(DSA_HEADS):
        far = rb_ref[N_BUCKETS - 1, h]
        for t in range(2):
            dist = col - row + t * CHUNK
            tile = jnp.zeros((CHUNK, CHUNK), jnp.float32)
            for b, (lo, hi) in enumerate(ranges):
                tile = jnp.where((dist >= lo) & (dist <= hi), (rb_ref[b, h] - far) * LOG2E, tile)
            out_ref[t, :, h * LANES:(h + 1) * LANES] = tile


def _bias_tables(rel_bias):
    return pl.pallas_call(
        _bias_kernel,
        in_specs=[pl.BlockSpec(memory_space=pltpu.SMEM)],
        out_specs=pl.BlockSpec(memory_space=pltpu.VMEM),
        out_shape=jax.ShapeDtypeStruct((3, CHUNK, DSA_HEADS * LANES), jnp.float32),
        name="bias_tables",
    )(rel_bias.astype(jnp.float32))


def _inproj_kernel(xa_ref, xb_ref, meta_ref, n1_ref, wr_ref, wqi_ref, wk_ref, wv_ref, ww_ref,
                   qs_ref, ks_ref,
                   r_ref, qit_ref, k1_ref, k2_ref, vt_ref, iwt_ref, *, pad_last):
    s = pl.program_id(1)
    top = jnp.where(s == 0, meta_ref[...], xa_ref[0])
    bot = xb_ref[0]
    if pad_last:
        bot = jnp.where(s == pl.num_programs(1) - 1, 0.0, bot)
    src = jnp.concatenate([top, bot], axis=0)
    ms = jnp.mean(src * src, axis=-1, keepdims=True)
    u = (src * lax.rsqrt(ms + EPS) * n1_ref[...]).astype(jnp.bfloat16)

    r_ref[0] = _dot(u, wr_ref[...])

    lane = lax.broadcasted_iota(jnp.int32, (2 * CHUNK, LANES), 1)
    lo = lane < HALF

    def head_rms(t):
        ssq = jnp.sum(jnp.where(lo, t * t, 0.0), axis=-1, keepdims=True)
        return lax.rsqrt(ssq / DSA_DH + EPS)

    pq = _dot(u, wqi_ref[...])
    for h in range(DSA_HEADS):
        t = pq[:, h * LANES:(h + 1) * LANES]
        t = t * (jnp.where(lo, head_rms(t), 1.0) * qs_ref[...])
        for j in range(2):
            qit_ref[0, j, :, h * LANES:(h + 1) * LANES] = (
                t[j * CHUNK:(j + 1) * CHUNK].T.astype(jnp.bfloat16))

    pk = _dot(u, wk_ref[...])
    kn = pk * head_rms(pk) * ks_ref[...]
    k1_ref[0] = jnp.where(lo, kn, 0.0).astype(jnp.bfloat16)
    k2_ref[0] = jnp.where(lo, 0.0, pk).astype(jnp.bfloat16)

    pv = _dot(u, wv_ref[...])
    vx = jnp.where(lo, pv, jnp.where(lane == HALF, 1.0, 0.0))
    iw = _dot(u, ww_ref[...]) * (IDX_HEADS ** -0.5)
    for j in range(2):
        rows = slice(j * CHUNK, (j + 1) * CHUNK)
        vt_ref[0, :, rows] = vx[rows].T.astype(jnp.bfloat16)
        iwt_ref[0, j] = iw[rows].T[0:IDX_HEADS, :]


def _inproj(x, meta_pad, n1, wr, wqi, wk, wv, ww, qscale, kscale):
    B, L, D = x.shape
    nx = L // CHUNK
    nbp = nx + 1 + ((nx + 1) % 2)
    tp = nbp * CHUNK
    const = lambda shape: pl.BlockSpec(shape, lambda b, s: (0,) * len(shape))
    blk = lambda w: pl.BlockSpec((1, 2 * CHUNK, w), lambda b, s: (b, s, 0))
    return pl.pallas_call(
        functools.partial(_inproj_kernel, pad_last=bool((nx + 1) % 2)),
        grid=(B, nbp // 2),
        in_specs=[
            pl.BlockSpec((1, CHUNK, D), lambda b, s: (b, jnp.maximum(2 * s - 1, 0), 0)),
            pl.BlockSpec((1, CHUNK, D), lambda b, s: (b, jnp.minimum(2 * s, nx - 1), 0)),
            const(meta_pad.shape), const(n1.shape), const(wr.shape), const(wqi.shape),
            const(wk.shape), const(wv.shape), const(ww.shape), const(qscale.shape),
            const(kscale.shape),
        ],
        out_specs=[
            blk(wr.shape[1]),
            pl.BlockSpec((1, 2, LANES, DSA_HEADS * LANES), lambda b, s: (b, s, 0, 0)),
            blk(LANES), blk(LANES),
            pl.BlockSpec((1, LANES, 2 * CHUNK), lambda b, s: (b, 0, s)),
            pl.BlockSpec((1, 2, IDX_HEADS, CHUNK), lambda b, s: (b, s, 0, 0)),
        ],
        out_shape=[
            jax.ShapeDtypeStruct((B, tp, wr.shape[1]), jnp.float32),
            jax.ShapeDtypeStruct((B, nbp, LANES, DSA_HEADS * LANES), jnp.bfloat16),
            jax.ShapeDtypeStruct((B, tp, LANES), jnp.bfloat16),
            jax.ShapeDtypeStruct((B, tp, LANES), jnp.bfloat16),
            jax.ShapeDtypeStruct((B, LANES, tp), jnp.bfloat16),
            jax.ShapeDtypeStruct((B, nbp, IDX_HEADS, CHUNK), jnp.float32),
        ],
        compiler_params=pltpu.CompilerParams(
            dimension_semantics=("arbitrary", "arbitrary"), vmem_limit_bytes=VMEM_LIMIT),
        name="in_projection",
    )(x, x, meta_pad, n1, wr, wqi, wk, wv, ww, qscale, kscale)


def _retention_kernel(ra_ref, rb_ref, cosa_ref, sina_ref, cosb_ref, sinb_ref, gain_ref, y_ref,
                      state_ref, tbl_ref):
    s = pl.program_id(1)
    w = RET_HEADS * RET_DK
    bf = jnp.bfloat16

    @pl.when(s == 0)
    def _():
        state_ref[...] = jnp.zeros_like(state_ref)
        row = lax.broadcasted_iota(jnp.int32, (CHUNK, CHUNK), 0)
        col = lax.broadcasted_iota(jnp.int32, (CHUNK, CHUNK), 1)
        rowf = row.astype(jnp.float32)
        diff = (row - col).astype(jnp.float32)
        for h in range(RET_HEADS):
            log_gamma = math.log(1.0 - 2.0 ** (-5.0 - h))
            tbl_ref[0, h] = jnp.where(diff >= 0, jnp.exp(log_gamma * jnp.maximum(diff, 0.0)), 0.0)
            tbl_ref[1, h] = jnp.exp(log_gamma * (rowf + 1.0))
            tbl_ref[2, h] = jnp.exp(log_gamma * (CHUNK - 1.0 - rowf))

    live = jnp.where(s == 0, 0.0, 1.0)
    chunks = ((ra_ref, cosa_ref, sina_ref, live), (rb_ref, cosb_ref, sinb_ref, None))
    for h in range(RET_HEADS):
        log_gamma = math.log(1.0 - 2.0 ** (-5.0 - h))
        sl = slice(h * RET_DK, (h + 1) * RET_DK)
        state = state_ref[h]
        for j, (r_ref, cos_ref, sin_ref, scale) in enumerate(chunks):
            cos2, sin2 = cos_ref[...], sin_ref[...]
            q = r_ref[0, :, sl]
            k = r_ref[0, :, w + h * RET_DK: w + (h + 1) * RET_DK]
            v = r_ref[0, :, 2 * w + h * RET_DK: 2 * w + (h + 1) * RET_DK]
            g = r_ref[0, :, 3 * w + h * RET_DK: 3 * w + (h + 1) * RET_DK]
            if scale is not None:
                k = k * scale
            v = v.astype(bf)
            qr = q * cos2 + pltpu.roll(q, HALF, 1) * sin2
            kr = (k * cos2 + pltpu.roll(k, HALF, 1) * sin2) * (RET_DK ** -0.5)

            scores = _dot_nt(qr.astype(bf), kr.astype(bf)) * tbl_ref[0, h]
            o = _dot(scores.astype(bf), v)
            o = o + _dot((qr * tbl_ref[1, h]).astype(bf), state.astype(bf))
            kv = _dot_tn((kr * tbl_ref[2, h]).astype(bf), v)
            state = state * math.exp(log_gamma * CHUNK) + kv

            ms = jnp.mean(o * o, axis=-1, keepdims=True)
            on = o * lax.rsqrt(ms + EPS) * gain_ref[:, sl]
            gate = g * (1.0 / (1.0 + jnp.exp(-g)))
            y_ref[0, j * CHUNK:(j + 1) * CHUNK, sl] = (gate * on).astype(y_ref.dtype)
        state_ref[h] = state


def _retention(r_all, nb, cos2, sin2, gain):
    B = r_all.shape[0]
    w = RET_HEADS * RET_DK
    nstep = (nb + 1) // 2
    rspec = lambda f: pl.BlockSpec((1, CHUNK, 4 * w), lambda b, s: (b, f(s), 0))
    tspec = lambda f: pl.BlockSpec((CHUNK, LANES), lambda b, s: (f(s), 0))
    first = lambda s: jnp.maximum(2 * s - 1, 0)
    second = lambda s: 2 * s
    return pl.pallas_call(
        _retention_kernel,
        grid=(B, nstep),
        in_specs=[rspec(first), rspec(second), tspec(first), tspec(first), tspec(second),
                  tspec(second), pl.BlockSpec((1, w), lambda b, s: (0, 0))],
        out_specs=pl.BlockSpec((1, 2 * CHUNK, w), lambda b, s: (b, jnp.maximum(s - 1, 0), 0)),
        out_shape=jax.ShapeDtypeStruct((B, (nb - 1) * CHUNK, w), jnp.bfloat16),
        scratch_shapes=[pltpu.VMEM((RET_HEADS, RET_DK, RET_DK), jnp.float32),
                        pltpu.VMEM((3, RET_HEADS, CHUNK, CHUNK), jnp.float32)],
        compiler_params=pltpu.CompilerParams(
            dimension_semantics=("arbitrary", "arbitrary"), vmem_limit_bytes=VMEM_LIMIT),
        name="retention",
    )(r_all, r_all, cos2, sin2, cos2, sin2, gain)


def _bit_planes(tile):
    a = [tile[r * 8:(r + 1) * 8, :] for r in range(32)]
    j, m = 16, 0x0000FFFF
    while j:
        for k in range(32):
            if (k & j) == 0:
                t = (a[k] ^ lax.shift_right_logical(a[k + j], j)) & m
                a[k] = a[k] ^ t
                a[k + j] = a[k + j] ^ (t << j)
        j >>= 1
        if j:
            m = m ^ ((m << j) & 0xFFFFFFFF)
    return a


def _sublane_total(x):
    x = x + pltpu.roll(x, 4, 0)
    x = x + pltpu.roll(x, 2, 0)
    return x + pltpu.roll(x, 1, 0)


def _dsa_kernel(qa_ref, qb_ref, iwt_ref, k1_ref, k2_ref, vt_ref, bias_ref, tri_ref, y_ref,
                key_ref, plane_ref, thr_ref, ntie_ref, flag_ref, m_ref, acc_ref, sa_ref, sb_ref,
                rhs_ref, *, topk):
    t = pl.program_id(1)
    last_q = pl.num_programs(1) - 1
    bf = jnp.bfloat16
    H = DSA_HEADS
    KP = 2 * CHUNK
    G = plane_ref.shape[1]
    cur, nxt = t % 2, (t + 1) % 2

    @pl.when((pl.program_id(0) == 0) & (t == 0))
    def _():
        plane_ref[...] = jnp.zeros_like(plane_ref)
        key_ref[...] = jnp.zeros_like(key_ref)
        eye = (lax.broadcasted_iota(jnp.int32, (CHUNK, CHUNK), 0)
               == lax.broadcasted_iota(jnp.int32, (CHUNK, CHUNK), 1))
        for h in range(DSA_HEADS):
            rhs_ref[CHUNK:2 * CHUNK, h * LANES:(h + 1) * LANES] = jnp.where(eye, 1.0, 0.0).astype(bf)

    @pl.when(t == 0)
    def _():
        thr_ref[0] = jnp.zeros(thr_ref.shape[1:], jnp.float32)
        ntie_ref[0] = jnp.zeros(ntie_ref.shape[1:], jnp.float32)
        flag_ref[0] = 0

    row = lax.broadcasted_iota(jnp.int32, (KP, CHUNK), 0)
    col = lax.broadcasted_iota(jnp.int32, (KP, CHUNK), 1)
    hs = lambda h: slice(h * LANES, (h + 1) * LANES)
    hp = lambda c: slice(2 * c * LANES, 2 * (c + 1) * LANES)
    nsteps = lambda n: (n + 2) // 2

    def key_rows(g):
        last = k1_ref.shape[1] // KP - 1
        return pl.ds(pl.multiple_of(jnp.minimum(g, last) * KP, KP), KP)

    blk_b = t + 1
    steps_b = jnp.where(t < last_q, nsteps(blk_b), 0)
    iters_b, tail_b = steps_b // 2, steps_b % 2
    qb_idx = blk_b * CHUNK + col
    iw_rows = [jnp.broadcast_to(iwt_ref[0, 0, h:h + 1, :], (KP, CHUNK)) for h in range(IDX_HEADS)]

    def score_pair(g):
        kblk = k2_ref[0, key_rows(g), :]
        sc = None
        for c in range(H // 2):
            s = _dot(kblk, qb_ref[0, 0, :, hp(c)])
            for hh in range(2):
                term = iw_rows[2 * c + hh] * jnp.maximum(s[:, hs(hh)], 0.0)
                sc = term if sc is None else sc + term
        k_idx = g * KP + row
        valid = (k_idx <= qb_idx) & (k_idx >= PAD)
        sc = jnp.where(valid, sc, NEG)
        key_ref[nxt, g] = sc
        bits = lax.bitcast_convert_type(sc, jnp.int32)
        key = jnp.where(bits < 0, jnp.int32(INT_MIN) - bits, bits)
        planes = _bit_planes(key ^ jnp.int32(INT_MIN))
        for p in range(32):
            plane_ref[p, g] = planes[p]

    def score_two(i):
        score_pair(2 * i)
        score_pair(2 * i + 1)

    blk_a = t
    steps_a = jnp.where(t > 0, nsteps(blk_a), 0)
    iters_a, tail_a = steps_a // 2, steps_a % 2
    far_a = jnp.where(t > 0, ((blk_a - 1) // 2) // 2, 0)
    rep = lambda x: jnp.tile(x, (KP // SUBLANES, 1))
    thr = thr_ref[cur]
    few = thr <= 0.999 * NEG
    thr_b = rep(jnp.where(few, 0.5 * NEG, thr))
    thr_sel_b = thr_b
    n_tie_b = rep(jnp.where(few, 0.0, ntie_ref[cur]))
    has_ties = flag_ref[cur] > 0
    va = acc_ref.shape[0]

    @pl.when(t > 0)
    def _():
        m_ref[...] = jnp.full_like(m_ref, NEG)
        acc_ref[...] = jnp.zeros_like(acc_ref)
        rhs_ref[0:CHUNK, :] = qa_ref[0, 0]

    def logits(g, dst_ref, ties):
        kblk = k1_ref[0, key_rows(g), :]
        if ties:
            dst_ref[...] = _dot(kblk, qa_ref[0, 0])
        else:
            off = jnp.where(key_ref[cur, g] >= thr_sel_b, 0.0, NEG).astype(bf)
            dst_ref[...] = _dot(jnp.concatenate([kblk, off], axis=1), rhs_ref[...])

    def attend(g, s_ref, ties_before, near, ties):
        if ties:
            key = key_ref[cur, g]
            eq = key == thr_b
            eqf = jnp.where(eq, 1.0, 0.0)
            before = ties_before + _dot(tri_ref[...], eqf.astype(bf))
            sel = (key > thr_b) | (eq & (before < n_tie_b))
            ties_before = ties_before + jnp.sum(eqf, axis=0, keepdims=True)
        ps, alphas = [], []
        for h in range(H):
            s = s_ref[:, hs(h)]
            if near:
                ta = jnp.clip(blk_a - 2 * g, 0, 2)
                tb = jnp.clip(blk_a - 2 * g - 1, 0, 2)
                s = s + jnp.concatenate([bias_ref[ta, :, hs(h)], bias_ref[tb, :, hs(h)]], axis=0)
            if ties:
                s = jnp.where(sel, s, NEG)
            m_prev = m_ref[h:h + 1, :]
            m_new = jnp.maximum(m_prev, jnp.max(s, axis=0, keepdims=True))
            m_ref[h:h + 1, :] = m_new
            alphas.append(jnp.exp2(m_prev - m_new))
            ps.append(jnp.exp2(s - m_new).astype(bf))
        pv = _dot(vt_ref[0, 0:va, key_rows(g)], jnp.concatenate(ps, axis=1))
        acc_ref[...] = acc_ref[...] * jnp.concatenate(alphas, axis=1) + pv
        return ties_before

    def attend_two(i, ties_before, near, ties, fused):
        logits(2 * i + 1, sb_ref, ties)
        ties_before = attend(2 * i, sa_ref, ties_before, near, ties)
        logits(2 * i + 2, sa_ref, ties)
        ties_before = attend(2 * i + 1, sb_ref, ties_before, near, ties)
        if fused:
            score_two(i)
        return ties_before

    zero_ties = jnp.zeros((KP, CHUNK), jnp.float32)
    loop = lambda lo, hi, **kw: (lambda c: lax.fori_loop(lo, hi, functools.partial(attend_two, **kw), c))
    fused_hi = jnp.minimum(iters_a, iters_b)

    @pl.when(has_ties)
    def _():
        logits(0, sa_ref, True)
        c = loop(0, far_a, near=False, ties=True, fused=False)(zero_ties)
        c = loop(far_a, iters_a, near=True, ties=True, fused=False)(c)

        @pl.when(tail_a == 1)
        def _():
            attend(2 * iters_a, sa_ref, c, True, True)

    @pl.when(jnp.logical_not(has_ties))
    def _():
        logits(0, sa_ref, False)
        far_hi = jnp.minimum(far_a, fused_hi)
        loop(0, far_hi, near=False, ties=False, fused=True)(zero_ties)
        loop(far_hi, fused_hi, near=True, ties=False, fused=True)(zero_ties)
        loop(fused_hi, iters_a, near=True, ties=False, fused=False)(zero_ties)

        @pl.when(tail_a == 1)
        def _():
            attend(2 * iters_a, sa_ref, zero_ties, True, False)

    lax.fori_loop(jnp.where(has_ties, 0, fused_hi), iters_b, lambda i, _: (score_two(i), 0)[1], 0)

    @pl.when(tail_b == 1)
    def _():
        score_pair(2 * iters_b)

    @pl.when(t > 0)
    def _():
        lo = lax.broadcasted_iota(jnp.int32, (CHUNK, CHUNK), 1) < HALF
        for c in range(H // 2):
            tiles = []
            for h in (2 * c, 2 * c + 1):
                a = acc_ref[:, hs(h)] * (1.0 / acc_ref[HALF:HALF + 1, hs(h)])
                a = jnp.concatenate([a, jnp.zeros((CHUNK - va, CHUNK), jnp.float32)], axis=0)
                tiles.append(a.T)
            y_ref[0, :, hs(c)] = jnp.where(lo, tiles[0], pltpu.roll(tiles[1], HALF, 1)).astype(y_ref.dtype)

    @pl.when(t < last_q)
    def _():
        npair_b = (blk_b + 2) // 2
        g_idx = lax.broadcasted_iota(jnp.int32, (G, 8, CHUNK), 0)
        alive0 = jnp.where(g_idx < npair_b, jnp.int32(-1), jnp.int32(0))

        def ones_in(words):
            return _sublane_total(lax.population_count(words).sum(axis=0))

        def radix(i, carry):
            alive, above, thr_u = carry
            hi_plane, lo_plane = plane_ref[2 * i], plane_ref[2 * i + 1]
            a1 = alive & hi_plane
            a0 = alive ^ a1
            a11, a01 = a1 & lo_plane, a0 & lo_plane
            c1, c11, c01 = ones_in(a1), ones_in(a11), ones_in(a01)
            hi = (above + c1) >= topk
            above = jnp.where(hi, above, above + c1)
            c_lo = jnp.where(hi, c11, c01)
            lo = (above + c_lo) >= topk
            above = jnp.where(lo, above, above + c_lo)
            bit = lax.shift_right_logical(jnp.int32(INT_MIN), 2 * i)
            thr_u = (thr_u | jnp.where(hi, bit, 0)
                     | jnp.where(lo, lax.shift_right_logical(bit, 1), 0))
            group = jnp.where(hi[None], a1, a0)
            with_lo = jnp.where(hi[None], a11, a01)
            alive = jnp.where(lo[None], with_lo, group ^ with_lo)
            return alive, above, thr_u

        zero8 = jnp.zeros((8, CHUNK), jnp.int32)
        thr_u = lax.fori_loop(0, 8, lambda i, c: radix(2 * i + 1, radix(2 * i, c)),
                              (alive0, zero8, zero8))[2]
        key_k = thr_u ^ jnp.int32(INT_MIN)
        as_score = lambda k: lax.bitcast_convert_type(
            jnp.where(k < 0, jnp.int32(INT_MIN) - k, k), jnp.float32)
        k_f = jnp.float32(topk)

        def count_ge(*ps):
            ps_b = [rep(p) for p in ps]

            def body(g, parts):
                sc = key_ref[nxt, g]
                return tuple(
                    part + jnp.where(sc >= p_b, 1.0, 0.0).reshape(KP // SUBLANES, SUBLANES, CHUNK).sum(axis=0)
                    for part, p_b in zip(parts, ps_b))

            zero = jnp.zeros((SUBLANES, CHUNK), jnp.float32)
            pairs = steps_b // 2
            parts = lax.fori_loop(0, pairs, lambda i, c: body(2 * i + 1, body(2 * i, c)), (zero,) * len(ps))
            parts = lax.fori_loop(2 * pairs, steps_b, body, parts)
            return [_sublane_total(c) for c in parts]

        def pending(lo, hi, c_lo, c_hi):
            mid = 0.5 * (lo + hi)
            final = (c_lo >= k_f) & (c_hi < k_f) & ((c_lo == k_f) | (mid <= lo) | (mid >= hi))
            return jnp.logical_not(final)

        def status(lo, hi, c_lo, c_hi):
            tied = (c_lo > k_f) & (lo > 0.999 * NEG)
            return jnp.max(jnp.where(pending(lo, hi, c_lo, c_hi), 2.0, jnp.where(tied, 1.0, 0.0)))

        def refine(state):
            lo, hi, c_lo, c_hi, step, it, _ = state
            low_bad, high_bad = c_lo < k_f, c_hi >= k_f
            p = jnp.where(low_bad, lo - step, jnp.where(high_bad, hi + step, 0.5 * (lo + hi)))
            c_p, = count_ge(p)
            todo = pending(lo, hi, c_lo, c_hi)
            down = todo & low_bad
            up = todo & jnp.logical_not(low_bad) & high_bad
            split = todo & jnp.logical_not(low_bad | high_bad)
            as_lo = down | (split & (c_p >= k_f))
            as_hi = up | (split & (c_p < k_f))
            lo, c_lo, hi, c_hi = (
                jnp.where(as_lo, p, jnp.where(up, hi, lo)),
                jnp.where(as_lo, c_p, jnp.where(up, c_hi, c_lo)),
                jnp.where(as_hi, p, jnp.where(down, lo, hi)),
                jnp.where(as_hi, c_p, jnp.where(down, c_lo, c_hi)))
            return (lo, hi, c_lo, c_hi, jnp.where(down | up, 2.0 * step, step), it + 1,
                    status(lo, hi, c_lo, c_hi))

        lo0, hi0 = as_score(key_k), as_score(key_k + 1)
        c_lo0, c_hi0 = count_ge(lo0, hi0)
        state = (lo0, hi0, c_lo0, c_hi0, jnp.maximum(jnp.abs(lo0), 1e-30) * 2.0 ** -20,
                 jnp.int32(0), status(lo0, hi0, c_lo0, c_hi0))
        lo, hi, c_lo, c_hi, _, _, left = lax.while_loop(
            lambda st: (st[6] > 1.5) & (st[5] < 200), refine, state)
        masked_n = lo <= 0.999 * NEG
        exact_n = c_lo == k_f
        thr_ref[nxt] = lo
        ntie_ref[nxt] = jnp.where(masked_n, 0.0, jnp.where(exact_n, jnp.float32(2 * KP * G), k_f - c_hi))
        flag_ref[nxt] = (left > 0.5).astype(jnp.int32)


def _sparse_attention(qit, iwt, k1, k2, vt, bias_tbl, tri, topk, nb):
    B, nbp, _, hw = qit.shape
    tp = nbp * CHUNK
    nstep = nbp // 2
    nstep += nstep % 2
    kspec = pl.BlockSpec((1, tp, LANES), lambda b, t: (b, 0, 0))
    acc_rows = DSA_DH + BF16_ROWS
    nxt = lambda t: jnp.minimum(t + 1, nb - 1)
    return pl.pallas_call(
        functools.partial(_dsa_kernel, topk=topk),
        grid=(B, nb),
        in_specs=[
            pl.BlockSpec((1, 1, LANES, hw), lambda b, t: (b, t, 0, 0)),
            pl.BlockSpec((1, 1, LANES, hw), lambda b, t: (b, nxt(t), 0, 0)),
            pl.BlockSpec((1, 1, IDX_HEADS, CHUNK), lambda b, t: (b, nxt(t), 0, 0)),
            kspec, kspec,
            pl.BlockSpec((1, LANES, tp), lambda b, t: (b, 0, 0)),
            pl.BlockSpec(bias_tbl.shape, lambda b, t: (0, 0, 0)),
            pl.BlockSpec(tri.shape, lambda b, t: (0, 0)),
        ],
        out_specs=pl.BlockSpec((1, CHUNK, DSA_HEADS * DSA_DH),
                               lambda b, t: (b, jnp.maximum(t - 1, 0), 0)),
        out_shape=jax.ShapeDtypeStruct((B, (nb - 1) * CHUNK, DSA_HEADS * DSA_DH), jnp.bfloat16),
        scratch_shapes=[
            pltpu.VMEM((2, nstep, 2 * CHUNK, CHUNK), jnp.float32),
            pltpu.VMEM((32, nstep, 8, CHUNK), jnp.int32),
            pltpu.VMEM((2, 8, CHUNK), jnp.float32),
            pltpu.VMEM((2, 8, CHUNK), jnp.float32),
            pltpu.SMEM((2,), jnp.int32),
            pltpu.VMEM((DSA_HEADS, CHUNK), jnp.float32),
            pltpu.VMEM((acc_rows, hw), jnp.float32),
            pltpu.VMEM((2 * CHUNK, hw), jnp.float32),
            pltpu.VMEM((2 * CHUNK, hw), jnp.float32),
            pltpu.VMEM((2 * CHUNK, hw), jnp.bfloat16),
        ],
        compiler_params=pltpu.CompilerParams(
            dimension_semantics=("arbitrary", "arbitrary"), vmem_limit_bytes=VMEM_LIMIT),
        name="sparse_attention",
    )(qit, qit, iwt, k1, k2, vt, bias_tbl, tri)


def _mlp_kernel(x_ref, yr_ref, yd_ref, wor_ref, wod_ref, n2_ref, w1_ref, w2_ref, o_ref):
    h1 = x_ref[...] + _dot(yr_ref[...], wor_ref[...]) + _dot(yd_ref[...], wod_ref[...])
    ms = jnp.mean(h1 * h1, axis=-1, keepdims=True)
    u = (h1 * lax.rsqrt(ms + EPS) * n2_ref[...]).astype(jnp.bfloat16)
    o_ref[...] = h1
    d_ff = w1_ref.shape[1]
    for c in range(d_ff // FFN_CHUNK):
        sl = slice(c * FFN_CHUNK, (c + 1) * FFN_CHUNK)
        f = jnp.maximum(_dot(u, w1_ref[:, sl]), 0.0)
        o_ref[...] += _dot((f * f).astype(jnp.bfloat16), w2_ref[sl, :])


def _out_mlp(x2, yr, yd, wor, wod, n2, w1, w2):
    rows, D = x2.shape
    const = lambda a: pl.BlockSpec(a.shape, lambda i: (0, 0), pipeline_mode=pl.Buffered(1))
    tile = lambda w: pl.BlockSpec((ROW_TILE, w), lambda i: (i, 0))
    return pl.pallas_call(
        _mlp_kernel,
        grid=(rows // ROW_TILE,),
        in_specs=[tile(D), tile(yr.shape[1]), tile(yd.shape[1]),
                  const(wor), const(wod), const(n2), const(w1), const(w2)],
        out_specs=tile(D),
        out_shape=jax.ShapeDtypeStruct((rows, D), jnp.float32),
        compiler_params=pltpu.CompilerParams(
            dimension_semantics=("arbitrary",), vmem_limit_bytes=VMEM_LIMIT),
        name="out_mlp",
    )(x2, yr, yd, wor, wod, n2, w1, w2)


def kernel(x, meta_tokens, norm1_w, w_in, ret_norm_w, q_norm_w, k_norm_w, rel_bias,
           w_out, norm2_w, w_ff1, w_ff2):
    B, L, D = x.shape
    assert L % (2 * CHUNK) == 0 and L % ROW_TILE == 0 and w_in.shape[0] == 1
    topk = min(TOPK_MAX, L // 4)
    nb = L // CHUNK + 1
    bf = jnp.bfloat16
    f32 = jnp.float32

    rw = RET_HEADS * RET_DK
    dw = DSA_HEADS * DSA_DH
    sizes = (rw, rw, rw, rw, dw, DSA_DH, DSA_DH, IDX_HEADS * DSA_DH, DSA_DH, IDX_HEADS)
    offs = np.concatenate([[0], np.cumsum(sizes)])
    col = lambda i: w_in[0][:, int(offs[i]):int(offs[i + 1])]
    wr = jnp.concatenate([col(0), col(1), col(2), col(3)], axis=1).astype(bf)
    wq = col(4).reshape(D, DSA_HEADS, DSA_DH)
    wiq = col(7).reshape(D, IDX_HEADS, DSA_DH)
    wqi = jnp.concatenate([wq, wiq], axis=2).reshape(D, DSA_HEADS * LANES).astype(bf)
    wk = jnp.concatenate([col(5), col(8)], axis=1).astype(bf)
    wv = jnp.concatenate([col(6), jnp.zeros((D, LANES - DSA_DH), f32)], axis=1).astype(bf)
    ww = jnp.concatenate([col(9), jnp.zeros((D, LANES - IDX_HEADS), f32)], axis=1).astype(bf)

    idx_scale = jnp.full((DSA_DH,), DSA_DH ** -0.5, f32)
    qscale = jnp.concatenate([q_norm_w[0].astype(f32) * (DSA_DH ** -0.5 * LOG2E), idx_scale])[None]
    kscale = jnp.concatenate([k_norm_w[0].astype(f32), jnp.ones((DSA_DH,), f32)])[None]
    meta_pad = jnp.concatenate([jnp.zeros((PAD, D), x.dtype), meta_tokens.astype(x.dtype)], axis=0)

    cos2, sin2 = _rope_tables(nb)
    bias_tbl = _bias_tables(rel_bias)
    tri = jnp.asarray(np.tril(np.ones((2 * CHUNK, 2 * CHUNK), np.float32), k=-1), dtype=bf)

    r_all, qit, k1, k2, vt, iwt = _inproj(x, meta_pad, norm1_w[0][None].astype(f32),
                                          wr, wqi, wk, wv, ww, qscale, kscale)
    y_ret = _retention(r_all, nb, cos2, sin2, ret_norm_w[0][None].astype(f32))
    y_dsa = _sparse_attention(qit, iwt, k1, k2, vt, bias_tbl, tri, topk, nb)

    wo = w_out[0].astype(bf)
    out = _out_mlp(x.reshape(B * L, D), y_ret.reshape(B * L, rw), y_dsa.reshape(B * L, dw),
                   wo[:rw], wo[rw:], norm2_w[0][None].astype(f32),
                   w_ff1[0].astype(bf), w_ff2[0].astype(bf))
    return out.reshape(B, L, D)
```
